```python
import jax
import jax.numpy as jnp
from jax import lax
import numpy as np

D_MODEL = 4096
BATCH = 1
SEQ = 8192
DEPTH = 4

N_EVEN = (DEPTH + 1) // 2
N_ODD = DEPTH // 2
NORM_EPS = 1e-6

SSD_INNER = D_MODEL // 2
SSD_HEAD_DIM = 64
SSD_HEADS = SSD_INNER // SSD_HEAD_DIM
SSD_GROUPS = 4
SSD_STATE = 128
SSD_CONV = 4
SSD_CHUNK = 256
SSD_BC = SSD_GROUPS * SSD_STATE
SSD_CONV_CH = SSD_INNER + 2 * SSD_BC

GM_INNER = D_MODEL // 2
GM_CHUNK = 128
GM_GROUPS = 16
GM_GROUP_DIM = GM_INNER // GM_GROUPS

OFF_XBC = SSD_INNER
OFF_DT = OFF_XBC + SSD_CONV_CH
OFF_U = OFF_DT + SSD_HEADS
OFF_V = OFF_U + GM_INNER
HYB_IN = OFF_V + GM_INNER
HYB_MIX = SSD_INNER + GM_INNER

ATT_HEAD_DIM = 128
ATT_HEADS = D_MODEL // ATT_HEAD_DIM
ATT_KV_HEADS = 8
ATT_Q_W = ATT_HEADS * ATT_HEAD_DIM
ATT_KV_W = ATT_KV_HEADS * ATT_HEAD_DIM
ATT_QKV = ATT_Q_W + 2 * ATT_KV_W
ROPE_THETA = 10000.0
MOBA_BLOCK = 256
MOBA_TOPK = 3
MOBA_QBLOCK = 32

MOE_GROUPS = 4
MOE_EXPERTS_PER_GROUP = 4
MOE_EXPERTS = MOE_GROUPS * MOE_EXPERTS_PER_GROUP
MOE_TOPK = 2
MOE_FF = 768
MOE_ROW_BLOCK = 128

kernel_name = 'hybrid_ssd_gmlp_moba_hmoe'


def _rms_normalize(x):
    xf = x.astype(jnp.float32)
    return (xf * lax.rsqrt(jnp.mean(xf * xf, axis=-1, keepdims=True) + NORM_EPS)).astype(x.dtype)


def rmsnorm(x, g):
    return _rms_normalize(x) * g


def layernorm(x, g, b):
    xf = x.astype(jnp.float32)
    mu = jnp.mean(xf, axis=-1, keepdims=True)
    xc = xf - mu
    var = jnp.mean(xc * xc, axis=-1, keepdims=True)
    return (xc * lax.rsqrt(var + NORM_EPS)).astype(x.dtype) * g + b


def pad_seq(a, mult, axis=1):
    n = a.shape[axis]
    pad = (-n) % mult
    if pad == 0:
        return a
    widths = [(0, 0)] * a.ndim
    widths[axis] = (0, pad)
    return jnp.pad(a, widths)


def causal_dwconv(u, w, b):
    k, c = w.shape
    y = lax.conv_general_dilated(u, w[:, None, :].astype(u.dtype), window_strides=(1,),
                                 padding=[(k - 1, 0)], dimension_numbers=('NWC', 'WIO', 'NWC'),
                                 feature_group_count=c)
    return y + b


def ssd_chunked_scan(xh, dt, a, bm, cm):
    bsz, t = xh.shape[0], xh.shape[1]
    g, r, p, n, l = SSD_GROUPS, SSD_HEADS // SSD_GROUPS, SSD_HEAD_DIM, SSD_STATE, SSD_CHUNK
    xdt = pad_seq(xh * dt[..., None].astype(xh.dtype), l)
    da = pad_seq(dt * a, l)
    bm = pad_seq(bm, l)
    cm = pad_seq(cm, l)
    nc = xdt.shape[1] // l
    X = xdt.reshape(bsz, nc, l, g, r, p)
    dA = da.reshape(bsz, nc, l, g, r)
    Bc = bm.reshape(bsz, nc, l, g, n)
    Cc = cm.reshape(bsz, nc, l, g, n)
    cs = jnp.cumsum(dA, axis=2)
    causal = jnp.tril(jnp.ones((l, l), dtype=bool))[:, :, None, None]
    seg = cs[:, :, :, None] - cs[:, :, None, :]
    decay = jnp.exp(jnp.where(causal, seg, -jnp.inf))
    cb = jnp.einsum('bclgn,bcsgn->bclsg', Cc, Bc)
    y_diag = jnp.einsum('bclsgr,bcsgrp->bclgrp', cb[..., None] * decay, X)
    to_end = jnp.exp(cs[:, :, -1:] - cs)
    states = jnp.einsum('bclgn,bclgr,bclgrp->bcgrpn', Bc, to_end, X)
    chunk_decay = jnp.exp(cs[:, :, -1])

    def carry_state(h, inp):
        s_c, d_c = inp
        return h * d_c[..., None, None] + s_c, h

    h0 = jnp.zeros((bsz, g, r, p, n), states.dtype)
    _, prev = lax.scan(carry_state, h0, (jnp.swapaxes(states, 0, 1), jnp.swapaxes(chunk_decay, 0, 1)))
    prev = jnp.swapaxes(prev, 0, 1)
    y_off = jnp.einsum('bclgn,bcgrpn,bclgr->bclgrp', Cc, prev, jnp.exp(cs))
    y = (y_diag + y_off).reshape(bsz, nc * l, SSD_HEADS, p)
    return y[:, :t]


def ssd_mixer(z, xbc, dt_raw, conv_w, conv_b, dt_bias, a_log, d_skip, norm_g):
    bsz, t, _ = z.shape
    xbc = jax.nn.silu(causal_dwconv(xbc, conv_w, conv_b))
    xs = xbc[..., :SSD_INNER]
    bm = xbc[..., SSD_INNER:SSD_INNER + SSD_BC].reshape(bsz, t, SSD_GROUPS, SSD_STATE)
    cm = xbc[..., SSD_INNER + SSD_BC:].reshape(bsz, t, SSD_GROUPS, SSD_STATE)
    xh = xs.reshape(bsz, t, SSD_HEADS, SSD_HEAD_DIM)
    dt = jax.nn.softplus(dt_raw.astype(jnp.float32) + dt_bias.astype(jnp.float32))
    a = -jnp.exp(a_log.astype(jnp.float32))
    y = ssd_chunked_scan(xh, dt, a, bm, cm).astype(z.dtype) + xh * d_skip[:, None]
    y = y.reshape(bsz, t, SSD_INNER) * jax.nn.silu(z)
    y = _rms_normalize(y.reshape(bsz, t, SSD_GROUPS, SSD_INNER // SSD_GROUPS)).reshape(bsz, t, SSD_INNER)
    return y * norm_g


def gmlp_mixer(u, v, ln_g, ln_b, ws, bs):
    bsz, t, _ = u.shape
    u = jax.nn.gelu(u, approximate=False)
    v = layernorm(jax.nn.gelu(v, approximate=False), ln_g, ln_b)
    nc = t // GM_CHUNK
    vc = v.reshape(bsz, nc, GM_CHUNK, GM_GROUPS, GM_GROUP_DIM)
    w = ws * jnp.tril(jnp.ones((GM_CHUNK, GM_CHUNK), ws.dtype))
    sv = jnp.einsum('gts,bcsgd->bctgd', w, vc) + bs.T[None, None, :, :, None]
    return u * sv.reshape(bsz, t, GM_INNER)


def ssd_gmlp_layer(xn, w_in, conv_w, conv_b, dt_bias, a_log, d_skip, ssd_norm_g, ln_g, ln_b, ws, bs, w_out):
    proj = xn @ w_in
    y_a = ssd_mixer(proj[..., :OFF_XBC], proj[..., OFF_XBC:OFF_DT], proj[..., OFF_DT:OFF_U],
                    conv_w, conv_b, dt_bias, a_log, d_skip, ssd_norm_g)
    y_b = gmlp_mixer(proj[..., OFF_U:OFF_V], proj[..., OFF_V:], ln_g, ln_b, ws, bs)
    return jnp.concatenate([y_a, y_b], axis=-1) @ w_out


def rope_tables(t):
    inv = 1.0 / (ROPE_THETA ** (jnp.arange(0, ATT_HEAD_DIM, 2, dtype=jnp.float32) / ATT_HEAD_DIM))
    ang = jnp.arange(t, dtype=jnp.float32)[:, None] * inv[None, :]
    return jnp.cos(ang), jnp.sin(ang)


def apply_rope(x, cos, sin):
    half = x.shape[-1] // 2
    x1 = x[..., :half]
    x2 = x[..., half:]
    c = cos[None, :, None, :]
    s = sin[None, :, None, :]
    return jnp.concatenate([x1 * c - x2 * s, x2 * c + x1 * s], axis=-1).astype(x.dtype)


def moba_attention(q, k, v):
    bsz, t, nh, dh = q.shape
    kvh = k.shape[2]
    rep = nh // kvh
    scale = dh ** -0.5
    kp = pad_seq(k, MOBA_BLOCK)
    vp = pad_seq(v, MOBA_BLOCK)
    nb = kp.shape[1] // MOBA_BLOCK
    kb = kp.reshape(bsz, nb, MOBA_BLOCK, kvh, dh).transpose(0, 3, 1, 2, 4)
    vb = vp.reshape(bsz, nb, MOBA_BLOCK, kvh, dh).transpose(0, 3, 1, 2, 4)
    kmean = jnp.mean(kb.astype(jnp.float32), axis=3)
    qg = q.reshape(bsz, t, kvh, rep, dh)
    gate = jnp.einsum('btgrd,bgnd->btgrn', qg.astype(jnp.float32), kmean).reshape(bsz, t, nh, nb)
    qblk = jnp.arange(t) // MOBA_BLOCK
    past = jnp.arange(nb)[None, :] < qblk[:, None]
    gate = jnp.where(past[None, :, None, :], gate, -jnp.inf)
    ksel = min(MOBA_TOPK, nb)
    _, sel_idx = lax.top_k(gate, ksel)
    sel_valid = jnp.arange(ksel)[None, :] < qblk[:, None]
    kv_of_head = jnp.arange(nh) // rep
    bidx = jnp.arange(bsz)

    def one_query_block(i):
        s0 = i * MOBA_QBLOCK
        qc = lax.dynamic_slice_in_dim(q, s0, MOBA_QBLOCK, axis=1)
        idx = lax.dynamic_slice_in_dim(sel_idx, s0, MOBA_QBLOCK, axis=1)
        valid = lax.dynamic_slice_in_dim(sel_valid, s0, MOBA_QBLOCK, axis=0)
        own = s0 // MOBA_BLOCK
        k_own = lax.dynamic_slice_in_dim(kp, own * MOBA_BLOCK, MOBA_BLOCK, axis=1)
        v_own = lax.dynamic_slice_in_dim(vp, own * MOBA_BLOCK, MOBA_BLOCK, axis=1)
        gi = (bidx[:, None, None, None], kv_of_head[None, None, :, None], idx)
        kg = kb[gi]
        vg = vb[gi]
        s_sel = jnp.einsum('bqhd,bqhksd->bqhks', qc, kg).astype(jnp.float32) * scale
        s_sel = jnp.where(valid[None, :, None, :, None], s_sel, -jnp.inf)
        qcg = qc.reshape(bsz, MOBA_QBLOCK, kvh, rep, dh)
        s_own = jnp.einsum('bqgrd,bsgd->bqgrs', qcg, k_own).astype(jnp.float32)
        s_own = s_own.reshape(bsz, MOBA_QBLOCK, nh, MOBA_BLOCK) * scale
        qpos = s0 + jnp.arange(MOBA_QBLOCK)
        kpos = own * MOBA_BLOCK + jnp.arange(MOBA_BLOCK)
        s_own = jnp.where((kpos[None, :] <= qpos[:, None])[None, :, None, :], s_own, -jnp.inf)
        n_sel = ksel * MOBA_BLOCK
        probs = jax.nn.softmax(jnp.concatenate([s_sel.reshape(bsz, MOBA_QBLOCK, nh, n_sel), s_own], axis=-1), axis=-1)
        p_sel = probs[..., :n_sel].reshape(bsz, MOBA_QBLOCK, nh, ksel, MOBA_BLOCK).astype(v.dtype)
        p_own = probs[..., n_sel:].reshape(bsz, MOBA_QBLOCK, kvh, rep, MOBA_BLOCK).astype(v.dtype)
        o_sel = jnp.einsum('bqhks,bqhksd->bqhd', p_sel, vg)
        o_own = jnp.einsum('bqgrs,bsgd->bqgrd', p_own, v_own).reshape(bsz, MOBA_QBLOCK, nh, dh)
        return o_sel + o_own

    out = lax.map(one_query_block, jnp.arange(t // MOBA_QBLOCK))
    return jnp.swapaxes(out, 0, 1).reshape(bsz, t, nh, dh)


def moba_layer(xn, w_qkv, q_norm, k_norm, w_out, cos, sin):
    bsz, t, _ = xn.shape
    qkv = xn @ w_qkv
    q = qkv[..., :ATT_Q_W].reshape(bsz, t, ATT_HEADS, ATT_HEAD_DIM)
    k = qkv[..., ATT_Q_W:ATT_Q_W + ATT_KV_W].reshape(bsz, t, ATT_KV_HEADS, ATT_HEAD_DIM)
    v = qkv[..., ATT_Q_W + ATT_KV_W:].reshape(bsz, t, ATT_KV_HEADS, ATT_HEAD_DIM)
    q = apply_rope(rmsnorm(q, q_norm), cos, sin)
    k = apply_rope(rmsnorm(k, k_norm), cos, sin)
    o = moba_attention(q, k, v)
    return o.reshape(bsz, t, ATT_Q_W) @ w_out


def routed_expert_ffn(xf, expert, token, weight, w_gate, w_up, w_down):
    n_assign = expert.shape[0]
    m = MOE_ROW_BLOCK
    order = jnp.argsort(expert)
    e_sorted = expert[order]
    counts = jnp.bincount(expert, length=MOE_EXPERTS)
    padded = (counts + m - 1) // m * m
    start = jnp.cumsum(counts) - counts
    pend = jnp.cumsum(padded)
    pstart = pend - padded
    dest = pstart[e_sorted] + jnp.arange(n_assign) - start[e_sorted]
    n_rows = ((n_assign + m - 1) // m + MOE_EXPERTS) * m
    row_token = jnp.zeros((n_rows,), jnp.int32).at[dest].set(token[order])
    row_weight = jnp.zeros((n_rows,), weight.dtype).at[dest].set(weight[order])
    n_blk = n_rows // m
    blk_expert = jnp.minimum(jnp.searchsorted(pend, jnp.arange(n_blk) * m, side='right'), MOE_EXPERTS - 1)
    xr = xf[row_token].reshape(n_blk, m, xf.shape[-1])

    def expert_block(args):
        xb, e = args
        hid = jax.nn.silu(xb @ w_gate[e]) * (xb @ w_up[e])
        return hid @ w_down[e]

    yr = lax.map(expert_block, (xr, blk_expert)).reshape(n_rows, xf.shape[-1])
    return jnp.zeros_like(xf).at[row_token].add(yr * row_weight[:, None].astype(yr.dtype))


def hier_moe(xn, w_group, b_group, w_expert, b_expert, w_gate, w_up, w_down):
    bsz, t, d = xn.shape
    n_tok = bsz * t
    xf = xn.reshape(n_tok, d)
    g_prob = jax.nn.softmax((xf @ w_group + b_group).astype(jnp.float32), axis=-1)
    g_w, g_idx = lax.top_k(g_prob, 1)
    e_logits = (xf @ w_expert + b_expert).astype(jnp.float32).reshape(n_tok, MOE_GROUPS, MOE_EXPERTS_PER_GROUP)
    sel = jnp.broadcast_to(g_idx[:, :, None], (n_tok, 1, MOE_EXPERTS_PER_GROUP))
    e_logits = jnp.take_along_axis(e_logits, sel, axis=1)[:, 0]
    e_w, e_idx = lax.top_k(jax.nn.softmax(e_logits, axis=-1), MOE_TOPK)
    e_w = e_w / jnp.sum(e_w, axis=-1, keepdims=True)
    weights = (g_w * e_w).reshape(-1)
    expert = (g_idx * MOE_EXPERTS_PER_GROUP + e_idx).reshape(-1).astype(jnp.int32)
    token = jnp.repeat(jnp.arange(n_tok, dtype=jnp.int32), MOE_TOPK)
    y = routed_expert_ffn(xf, expert, token, weights, w_gate, w_up, w_down)
    return y.reshape(bsz, t, d)


def setup_inputs(seed: int = 0) -> dict:
    key = jax.random.key(seed)
    ks = jax.random.split(key, 32)
    f32 = jnp.float32

    def nrm(k, shape, scale):
        return jax.random.normal(k, shape, f32) * scale

    dt0 = jnp.exp(jax.random.uniform(ks[6], (N_EVEN, SSD_HEADS), f32, float(np.log(1e-3)), float(np.log(1e-1))))
    return {
        'x': nrm(ks[0], (BATCH, SEQ, D_MODEL), 1.0),
        'norm_mix': 1.0 + nrm(ks[1], (DEPTH, D_MODEL), 0.02),
        'norm_ffn': 1.0 + nrm(ks[2], (DEPTH, D_MODEL), 0.02),
        'hyb_w_in': nrm(ks[3], (N_EVEN, D_MODEL, HYB_IN), D_MODEL ** -0.5),
        'ssd_conv_w': nrm(ks[4], (N_EVEN, SSD_CONV, SSD_CONV_CH), SSD_CONV ** -0.5),
        'ssd_conv_b': nrm(ks[5], (N_EVEN, SSD_CONV_CH), 0.02),
        'ssd_dt_bias': dt0 + jnp.log(-jnp.expm1(-dt0)),
        'ssd_a_log': jnp.log(jax.random.uniform(ks[7], (N_EVEN, SSD_HEADS), f32, 1.0, 16.0)),
        'ssd_d': 1.0 + nrm(ks[8], (N_EVEN, SSD_HEADS), 0.1),
        'ssd_norm': 1.0 + nrm(ks[9], (N_EVEN, SSD_INNER), 0.02),
        'gm_ln_g': 1.0 + nrm(ks[10], (N_EVEN, GM_INNER), 0.02),
        'gm_ln_b': nrm(ks[11], (N_EVEN, GM_INNER), 0.02),
        'gm_ws': nrm(ks[12], (N_EVEN, GM_GROUPS, GM_CHUNK, GM_CHUNK), 0.5 * GM_CHUNK ** -0.5),
        'gm_bs': 1.0 + nrm(ks[13], (N_EVEN, GM_GROUPS, GM_CHUNK), 0.1),
        'hyb_w_out': nrm(ks[14], (N_EVEN, HYB_MIX, D_MODEL), HYB_MIX ** -0.5),
        'att_w_qkv': nrm(ks[15], (N_ODD, D_MODEL, ATT_QKV), D_MODEL ** -0.5),
        'att_q_norm': 1.0 + nrm(ks[16], (N_ODD, ATT_HEAD_DIM), 0.02),
        'att_k_norm': 1.0 + nrm(ks[17], (N_ODD, ATT_HEAD_DIM), 0.02),
        'att_w_out': nrm(ks[18], (N_ODD, ATT_Q_W, D_MODEL), ATT_Q_W ** -0.5),
        'moe_w_group': nrm(ks[19], (DEPTH, D_MODEL, MOE_GROUPS), D_MODEL ** -0.5),
        'moe_b_group': nrm(ks[20], (DEPTH, MOE_GROUPS), 0.01),
        'moe_w_expert': nrm(ks[21], (DEPTH, D_MODEL, MOE_EXPERTS), D_MODEL ** -0.5),
        'moe_b_expert': nrm(ks[22], (DEPTH, MOE_EXPERTS), 0.01),
        'moe_w_gate': nrm(ks[23], (DEPTH, MOE_EXPERTS, D_MODEL, MOE_FF), D_MODEL ** -0.5),
        'moe_w_up': nrm(ks[24], (DEPTH, MOE_EXPERTS, D_MODEL, MOE_FF), D_MODEL ** -0.5),
        'moe_w_down': nrm(ks[25], (DEPTH, MOE_EXPERTS, MOE_FF, D_MODEL), MOE_FF ** -0.5),
    }


def reference(x, norm_mix, norm_ffn, hyb_w_in, ssd_conv_w, ssd_conv_b, ssd_dt_bias, ssd_a_log, ssd_d,
              ssd_norm, gm_ln_g, gm_ln_b, gm_ws, gm_bs, hyb_w_out, att_w_qkv, att_q_norm, att_k_norm,
              att_w_out, moe_w_group, moe_b_group, moe_w_expert, moe_b_expert, moe_w_gate, moe_w_up,
              moe_w_down):
    cos, sin = rope_tables(x.shape[1])
    h = x
    for layer in range(DEPTH):
        j = layer // 2
        xn = rmsnorm(h, norm_mix[layer])
        if layer % 2 == 0:
            mix = ssd_gmlp_layer(xn, hyb_w_in[j], ssd_conv_w[j], ssd_conv_b[j], ssd_dt_bias[j], ssd_a_log[j],
                                 ssd_d[j], ssd_norm[j], gm_ln_g[j], gm_ln_b[j], gm_ws[j], gm_bs[j], hyb_w_out[j])
        else:
            mix = moba_layer(xn, att_w_qkv[j], att_q_norm[j], att_k_norm[j], att_w_out[j], cos, sin)
        h = h + mix.astype(h.dtype)
        ffn = hier_moe(rmsnorm(h, norm_ffn[layer]), moe_w_group[layer], moe_b_group[layer], moe_w_expert[layer],
                       moe_b_expert[layer], moe_w_gate[layer], moe_w_up[layer], moe_w_down[layer])
        h = h + ffn.astype(h.dtype)
    return h
```

```python
import functools

import jax
import jax.numpy as jnp
from jax import lax
from jax.experimental import pallas as pl
from jax.experimental.pallas import tpu as pltpu

F32 = jnp.float32
BF16 = jnp.bfloat16
I32 = jnp.int32

D_MODEL = 4096
NORM_EPS = 1e-6

SSD_INNER = 2048
SSD_HEAD_DIM = 64
SSD_HEADS = 32
SSD_GROUPS = 4
SSD_STATE = 128
SSD_CONV = 4
SSD_CHUNK = 256
SSD_BC = SSD_GROUPS * SSD_STATE
SSD_CONV_CH = SSD_INNER + 2 * SSD_BC
SSD_GROUP_W = SSD_INNER // SSD_GROUPS
SSD_HEADS_PER_GROUP = SSD_HEADS // SSD_GROUPS

GM_INNER = 2048
GM_CHUNK = 128
GM_GROUPS = 16
GM_GROUP_DIM = 128

OFF_XBC = SSD_INNER
OFF_DT = OFF_XBC + SSD_CONV_CH
OFF_U = OFF_DT + SSD_HEADS
OFF_V = OFF_U + GM_INNER

ATT_HEAD_DIM = 128
ATT_HEADS = 32
ATT_KV_HEADS = 8
ATT_REP = ATT_HEADS // ATT_KV_HEADS
ATT_Q_W = ATT_HEADS * ATT_HEAD_DIM
ATT_KV_W = ATT_KV_HEADS * ATT_HEAD_DIM
ROPE_THETA = 10000.0
MOBA_BLOCK = 256
MOBA_TOPK = 3

MOE_GROUPS = 4
MOE_EPG = 4
MOE_EXPERTS = 16
MOE_TOPK = 2
MOE_FF = 768

LANES = 128
VMEM_LIMIT_BYTES = 56 * 1024 * 1024

MM_TM = 512
MM_TN = 1024
NORM_TM = 256
ATT_TQ = 128
MOE_TM = 256
HIGHEST = lax.Precision.HIGHEST


def _cparams(*sem):
    return pltpu.CompilerParams(dimension_semantics=sem, vmem_limit_bytes=VMEM_LIMIT_BYTES)


def _silu(x):
    return x / (1.0 + jnp.exp(-x))


def _softplus(x):
    return jnp.maximum(x, 0.0) + jnp.log1p(jnp.exp(-jnp.abs(x)))


def _gelu_erf(x):
    return 0.5 * x * (1.0 + lax.erf(x * (2.0 ** -0.5)))


def _rmsnorm_body(x_ref, g_ref, o_ref):
    x = x_ref[...]
    ms = jnp.mean(x * x, axis=-1, keepdims=True)
    o_ref[...] = (x * lax.rsqrt(ms + NORM_EPS) * g_ref[...]).astype(o_ref.dtype)


def rmsnorm_call(x, g):
    t, d = x.shape
    return pl.pallas_call(
        _rmsnorm_body,
        grid=(t // NORM_TM,),
        in_specs=[pl.BlockSpec((NORM_TM, d), lambda i: (i, 0)),
                  pl.BlockSpec((1, d), lambda i: (0, 0))],
        out_specs=pl.BlockSpec((NORM_TM, d), lambda i: (i, 0)),
        out_shape=jax.ShapeDtypeStruct((t, d), BF16),
        compiler_params=_cparams("parallel"),
        name="rmsnorm",
    )(x, g.reshape(1, d))


def _matmul_body(*refs, n_parts, has_res):
    a_refs = refs[:n_parts]
    w_refs = refs[n_parts:2 * n_parts]
    o_ref = refs[-1]
    acc = jnp.dot(a_refs[0][...], w_refs[0][...], preferred_element_type=F32)
    for p in range(1, n_parts):
        acc = acc + jnp.dot(a_refs[p][...], w_refs[p][...], preferred_element_type=F32)
    if has_res:
        acc = acc + refs[2 * n_parts][...]
    o_ref[...] = acc.astype(o_ref.dtype)


def matmul_call(a_parts, w, res=None, out_dtype=F32, tn=MM_TN, name="matmul"):
    n_parts = len(a_parts)
    m, kp = a_parts[0].shape
    n = w.shape[1]
    tm = MM_TM
    in_specs = [pl.BlockSpec((tm, kp), lambda j, i: (i, 0)) for _ in a_parts]
    in_specs += [pl.BlockSpec((kp, tn), functools.partial(lambda j, i, p: (p, j), p=p))
                 for p in range(n_parts)]
    args = list(a_parts) + [w] * n_parts
    if res is not None:
        in_specs.append(pl.BlockSpec((tm, tn), lambda j, i: (i, j)))
        args.append(res)
    return pl.pallas_call(
        functools.partial(_matmul_body, n_parts=n_parts, has_res=res is not None),
        grid=(n // tn, m // tm),
        in_specs=in_specs,
        out_specs=pl.BlockSpec((tm, tn), lambda j, i: (i, j)),
        out_shape=jax.ShapeDtypeStruct((m, n), out_dtype),
        compiler_params=_cparams("parallel", "parallel"),
        name=name,
    )(*args)


def _ssd_body(z_ref, xs_ref, b_ref, c_ref, dt_ref, dtt_ref,
              wx_ref, bx_ref, wb_ref, bb_ref, wc_ref, bc_ref,
              dtb_ref, dtbt_ref, alog_ref, alogt_ref, dskip_ref, ng_ref, expand_ref,
              o_ref, state_ref, tx_ref, tb_ref, tc_ref):
    c = pl.program_id(1)
    L = SSD_CHUNK
    HP = SSD_HEADS_PER_GROUP

    @pl.when(c == 0)
    def _():
        state_ref[...] = jnp.zeros_like(state_ref)
        tx_ref[...] = jnp.zeros_like(tx_ref)
        tb_ref[...] = jnp.zeros_like(tb_ref)
        tc_ref[...] = jnp.zeros_like(tc_ref)

    def conv_silu(cur_ref, tail_ref, w_ref, bias_ref):
        cur = cur_ref[...].astype(F32)
        ext = jnp.concatenate([tail_ref[...], cur], axis=0)
        w = w_ref[...]
        acc = bias_ref[...] + w[SSD_CONV - 1:SSD_CONV, :] * cur
        for j in range(1, SSD_CONV):
            acc = acc + w[SSD_CONV - 1 - j:SSD_CONV - j, :] * ext[8 - j:8 - j + L, :]
        tail_ref[...] = cur[L - 8:L, :]
        return _silu(acc)

    xs = conv_silu(xs_ref, tx_ref, wx_ref, bx_ref)
    bm = conv_silu(b_ref, tb_ref, wb_ref, bb_ref)
    cm = conv_silu(c_ref, tc_ref, wc_ref, bc_ref)

    dt = _softplus(dt_ref[...] + dtb_ref[...])
    da = dt * (-jnp.exp(alog_ref[...]))
    dtt = _softplus(dtt_ref[...] + dtbt_ref[...])
    dat = dtt * (-jnp.exp(alogt_ref[...]))

    row = lax.broadcasted_iota(I32, (L, L), 0)
    col = lax.broadcasted_iota(I32, (L, L), 1)
    causal = row >= col
    tril = causal.astype(F32)
    cs = jnp.dot(tril, da, precision=HIGHEST, preferred_element_type=F32)
    cst = jnp.dot(dat, (row <= col).astype(F32), precision=HIGHEST,
                  preferred_element_type=F32)
    cs_last = cs[L - 1:L, :]
    to_end = jnp.exp(cs_last - cs)
    ecs = jnp.exp(cs)

    expand = expand_ref[...]
    dt_e = jnp.dot(dt, expand, precision=HIGHEST, preferred_element_type=F32)
    to_end_e = jnp.dot(to_end, expand, precision=HIGHEST, preferred_element_type=F32)
    ecs_e = jnp.dot(ecs, expand, precision=HIGHEST, preferred_element_type=F32)

    x = xs * dt_e
    bm16 = bm.astype(BF16)
    cm16 = cm.astype(BF16)
    cb = lax.dot_general(cm16, bm16, (((1,), (1,)), ((), ())), preferred_element_type=F32)

    prev = state_ref[...]
    y = jnp.dot(cm16, prev.astype(BF16), preferred_element_type=F32) * ecs_e

    lane = lax.broadcasted_iota(I32, (L, SSD_GROUP_W), 1)
    for h in range(HP):
        seg = cs[:, h:h + 1] - cst[h:h + 1, :]
        decay = jnp.exp(jnp.where(causal, seg, -jnp.inf))
        m16 = (cb * decay).astype(BF16)
        in_head = (lane >= h * SSD_HEAD_DIM) & (lane < (h + 1) * SSD_HEAD_DIM)
        xh = jnp.where(in_head, x, 0.0).astype(BF16)
        y = y + jnp.dot(m16, xh, preferred_element_type=F32)

    xw = (x * to_end_e).astype(BF16)
    state_ref[...] = prev * ecs_e[L - 1:L, :] + jnp.dot(bm.T.astype(BF16), xw,
                                                        preferred_element_type=F32)

    y = y + xs * dskip_ref[...]
    y = y * _silu(z_ref[...].astype(F32))
    ms = jnp.mean(y * y, axis=-1, keepdims=True)
    o_ref[...] = (y * lax.rsqrt(ms + NORM_EPS) * ng_ref[...]).astype(o_ref.dtype)


def ssd_call(proj, dt_raw, dtt_raw, conv_w, conv_b, dt_bias, a_log, d_skip, norm_g):
    t = proj.shape[0]
    L = SSD_CHUNK
    gw = SSD_GROUP_W
    xs0 = (SSD_INNER + 2 * GM_INNER) // gw
    b0 = (SSD_INNER + 2 * GM_INNER + SSD_INNER) // SSD_STATE
    c0 = b0 + SSD_GROUPS
    hp = SSD_HEADS_PER_GROUP

    dtb = jnp.zeros((SSD_GROUPS, LANES), F32).at[:, :hp].set(dt_bias.reshape(SSD_GROUPS, hp))
    alog = jnp.zeros((SSD_GROUPS, LANES), F32).at[:, :hp].set(a_log.reshape(SSD_GROUPS, hp))
    dtb = dtb.reshape(1, SSD_GROUPS * LANES)
    alog = alog.reshape(1, SSD_GROUPS * LANES)
    dtbt = dt_bias.reshape(SSD_HEADS, 1)
    alogt = a_log.reshape(SSD_HEADS, 1)
    dskip = jnp.repeat(d_skip, SSD_HEAD_DIM).reshape(1, SSD_INNER)
    expand = (jnp.arange(LANES)[:, None] == (jnp.arange(gw)[None, :] // SSD_HEAD_DIM)).astype(F32)
    cw = conv_w.astype(F32)
    cbias = conv_b.reshape(1, SSD_CONV_CH).astype(F32)
    xb0 = SSD_INNER // SSD_STATE
    cb0 = xb0 + SSD_GROUPS

    in_specs = [
        pl.BlockSpec((L, gw), lambda g, c: (c, g)),
        pl.BlockSpec((L, gw), lambda g, c: (c, xs0 + g)),
        pl.BlockSpec((L, SSD_STATE), lambda g, c: (c, b0 + g)),
        pl.BlockSpec((L, SSD_STATE), lambda g, c: (c, c0 + g)),
        pl.BlockSpec((L, LANES), lambda g, c: (c, g)),
        pl.BlockSpec((hp, L), lambda g, c: (g, c)),
        pl.BlockSpec((SSD_CONV, gw), lambda g, c: (0, g)),
        pl.BlockSpec((1, gw), lambda g, c: (0, g)),
        pl.BlockSpec((SSD_CONV, SSD_STATE), lambda g, c: (0, xb0 + g)),
        pl.BlockSpec((1, SSD_STATE), lambda g, c: (0, xb0 + g)),
        pl.BlockSpec((SSD_CONV, SSD_STATE), lambda g, c: (0, cb0 + g)),
        pl.BlockSpec((1, SSD_STATE), lambda g, c: (0, cb0 + g)),
        pl.BlockSpec((1, LANES), lambda g, c: (0, g)),
        pl.BlockSpec((hp, 1), lambda g, c: (g, 0)),
        pl.BlockSpec((1, LANES), lambda g, c: (0, g)),
        pl.BlockSpec((hp, 1), lambda g, c: (g, 0)),
        pl.BlockSpec((1, gw), lambda g, c: (0, g)),
        pl.BlockSpec((1, gw), lambda g, c: (0, g)),
        pl.BlockSpec((LANES, gw), lambda g, c: (0, 0)),
    ]
    return pl.pallas_call(
        _ssd_body,
        grid=(SSD_GROUPS, t // L),
        in_specs=in_specs,
        out_specs=pl.BlockSpec((L, gw), lambda g, c: (c, g)),
        out_shape=jax.ShapeDtypeStruct((t, SSD_INNER), BF16),
        scratch_shapes=[pltpu.VMEM((SSD_STATE, gw), F32),
                        pltpu.VMEM((8, gw), F32),
                        pltpu.VMEM((8, SSD_STATE), F32),
                        pltpu.VMEM((8, SSD_STATE), F32)],
        compiler_params=_cparams("parallel", "arbitrary"),
        name="ssd",
    )(proj, proj, proj, proj, dt_raw, dtt_raw, cw, cbias, cw, cbias, cw, cbias,
      dtb, dtbt, alog, alogt, dskip, norm_g.reshape(1, SSD_INNER), expand)


def _gmlp_body(u_ref, v_ref, g_ref, b_ref, ws_ref, bst_ref, o_ref):
    u = _gelu_erf(u_ref[...].astype(F32))
    v = _gelu_erf(v_ref[...].astype(F32))
    mu = jnp.mean(v, axis=-1, keepdims=True)
    vc = v - mu
    var = jnp.mean(vc * vc, axis=-1, keepdims=True)
    vn = (vc * lax.rsqrt(var + NORM_EPS) * g_ref[...] + b_ref[...]).astype(BF16)
    row = lax.broadcasted_iota(I32, (GM_CHUNK, GM_CHUNK), 0)
    col = lax.broadcasted_iota(I32, (GM_CHUNK, GM_CHUNK), 1)
    causal = row >= col
    for g in range(GM_GROUPS):
        sl = slice(g * GM_GROUP_DIM, (g + 1) * GM_GROUP_DIM)
        w = jnp.where(causal, ws_ref[g], jnp.zeros((), BF16))
        sv = jnp.dot(w, vn[:, sl], preferred_element_type=F32) + bst_ref[:, sl]
        o_ref[:, sl] = (u[:, sl] * sv).astype(o_ref.dtype)


def gmlp_call(proj, ln_g, ln_b, ws, bs):
    t = proj.shape[0]
    ucol = SSD_INNER // GM_INNER
    bst = jnp.repeat(bs.T, GM_GROUP_DIM, axis=1)
    return pl.pallas_call(
        _gmlp_body,
        grid=(t // GM_CHUNK,),
        in_specs=[pl.BlockSpec((GM_CHUNK, GM_INNER), lambda i: (i, ucol)),
                  pl.BlockSpec((GM_CHUNK, GM_INNER), lambda i: (i, ucol + 1)),
                  pl.BlockSpec((1, GM_INNER), lambda i: (0, 0)),
                  pl.BlockSpec((1, GM_INNER), lambda i: (0, 0)),
                  pl.BlockSpec((GM_GROUPS, GM_CHUNK, GM_CHUNK), lambda i: (0, 0, 0)),
                  pl.BlockSpec((GM_CHUNK, GM_INNER), lambda i: (0, 0))],
        out_specs=pl.BlockSpec((GM_CHUNK, GM_INNER), lambda i: (i, 0)),
        out_shape=jax.ShapeDtypeStruct((t, GM_INNER), BF16),
        compiler_params=_cparams("parallel"),
        name="gmlp",
    )(proj, proj, ln_g.reshape(1, GM_INNER), ln_b.reshape(1, GM_INNER), ws.astype(BF16), bst)


def _rope_norm(x, gain, cosf, sinf):
    ms = jnp.mean(x * x, axis=-1, keepdims=True)
    xn = x * lax.rsqrt(ms + NORM_EPS) * gain
    return xn * cosf + pltpu.roll(xn, ATT_HEAD_DIM // 2, axis=1) * sinf


def _qprep_body(x_ref, g_ref, cos_ref, sin_ref, o_ref):
    y = _rope_norm(x_ref[...].astype(F32), g_ref[...], cos_ref[...], sin_ref[...])
    o_ref[...] = (y * (ATT_HEAD_DIM ** -0.5)).astype(o_ref.dtype)


def _kprep_body(x_ref, g_ref, cos_ref, sin_ref, o_ref, mean_ref):
    y = _rope_norm(x_ref[...].astype(F32), g_ref[...], cos_ref[...], sin_ref[...])
    o_ref[...] = y.astype(o_ref.dtype)
    mean_ref[...] = jnp.mean(y, axis=0, keepdims=True)


def qk_prep_call(qkv, q_norm, k_norm, cosf, sinf):
    t = qkv.shape[0]
    tb = MOBA_BLOCK
    common = [pl.BlockSpec((1, ATT_HEAD_DIM), lambda i, h: (0, 0)),
              pl.BlockSpec((tb, ATT_HEAD_DIM), lambda i, h: (i, 0)),
              pl.BlockSpec((tb, ATT_HEAD_DIM), lambda i, h: (i, 0))]
    q = pl.pallas_call(
        _qprep_body,
        grid=(t // tb, ATT_HEADS),
        in_specs=[pl.BlockSpec((tb, ATT_HEAD_DIM), lambda i, h: (i, h))] + common,
        out_specs=pl.BlockSpec((tb, ATT_HEAD_DIM), lambda i, h: (i, h)),
        out_shape=jax.ShapeDtypeStruct((t, ATT_Q_W), BF16),
        compiler_params=_cparams("parallel", "parallel"),
        name="q_prep",
    )(qkv, q_norm.reshape(1, ATT_HEAD_DIM), cosf, sinf)
    k, kmean = pl.pallas_call(
        _kprep_body,
        grid=(t // tb, ATT_KV_HEADS),
        in_specs=[pl.BlockSpec((tb, ATT_HEAD_DIM), lambda i, h: (i, ATT_HEADS + h))] + common,
        out_specs=[pl.BlockSpec((tb, ATT_HEAD_DIM), lambda i, h: (i, h)),
                   pl.BlockSpec((None, None, 1, ATT_HEAD_DIM), lambda i, h: (i, h, 0, 0))],
        out_shape=[jax.ShapeDtypeStruct((t, ATT_KV_W), BF16),
                   jax.ShapeDtypeStruct((t // tb, ATT_KV_HEADS, 1, ATT_HEAD_DIM), F32)],
        compiler_params=_cparams("parallel", "parallel"),
        name="k_prep",
    )(qkv, k_norm.reshape(1, ATT_HEAD_DIM), cosf, sinf)
    return q, k, kmean


def _attn_body(q_ref, k_ref, v_ref, km_ref, o_ref, m_ref, l_ref, acc_ref, *, tq):
    i = pl.program_id(1)
    R = ATT_REP * tq
    blk = MOBA_BLOCK
    qblk = (i * tq) // blk
    q4 = q_ref[...]
    q = jnp.concatenate([q4[:, r * ATT_HEAD_DIM:(r + 1) * ATT_HEAD_DIM] for r in range(ATT_REP)],
                        axis=0)

    nt = (((1,), (1,)), ((), ()))
    km = km_ref[...]
    km_hi = km.astype(BF16)
    km_lo = (km - km_hi.astype(F32)).astype(BF16)
    gate = (lax.dot_general(q, km_hi, nt, preferred_element_type=F32)
            + lax.dot_general(q, km_lo, nt, preferred_element_type=F32))
    bid = lax.broadcasted_iota(I32, (R, LANES), 1)
    gate = jnp.where(bid < qblk, gate, -jnp.inf)
    sels = []
    for kk in range(MOBA_TOPK):
        mx = jnp.max(gate, axis=1, keepdims=True)
        idx = jnp.min(jnp.where(gate == mx, bid, LANES), axis=1, keepdims=True)
        sels.append(jnp.where(kk < qblk, idx, -1))
        gate = jnp.where(bid == idx, -jnp.inf, gate)

    own0 = pl.multiple_of(qblk * blk, blk)
    s = lax.dot_general(q, k_ref[pl.ds(own0, blk), :], nt, preferred_element_type=F32)
    qpos = i * tq + lax.broadcasted_iota(I32, (R, blk), 0) % tq
    kpos = qblk * blk + lax.broadcasted_iota(I32, (R, blk), 1)
    s = jnp.where(kpos <= qpos, s, -jnp.inf)
    m0 = jnp.max(s, axis=1, keepdims=True)
    p = jnp.exp(s - m0)
    m_ref[...] = m0
    l_ref[...] = jnp.sum(p, axis=1, keepdims=True)
    acc_ref[...] = jnp.dot(p.astype(BF16), v_ref[pl.ds(own0, blk), :], preferred_element_type=F32)

    def past_block(j, carry):
        j0 = pl.multiple_of(j * blk, blk)
        sj = lax.dot_general(q, k_ref[pl.ds(j0, blk), :], nt, preferred_element_type=F32)
        chosen = (sels[0] == j) | (sels[1] == j) | (sels[2] == j)
        sj = jnp.where(chosen, sj, -jnp.inf)
        m_old = m_ref[...]
        m_new = jnp.maximum(m_old, jnp.max(sj, axis=1, keepdims=True))
        alpha = jnp.exp(m_old - m_new)
        pj = jnp.exp(sj - m_new)
        m_ref[...] = m_new
        l_ref[...] = alpha * l_ref[...] + jnp.sum(pj, axis=1, keepdims=True)
        acc_ref[...] = alpha * acc_ref[...] + jnp.dot(pj.astype(BF16), v_ref[pl.ds(j0, blk), :],
                                                      preferred_element_type=F32)
        return carry

    lax.fori_loop(0, qblk, past_block, 0)

    out = acc_ref[...] / l_ref[...]
    for r in range(ATT_REP):
        o_ref[:, r * ATT_HEAD_DIM:(r + 1) * ATT_HEAD_DIM] = out[r * tq:(r + 1) * tq, :].astype(o_ref.dtype)


def attention_call(q, k, qkv, kmean):
    t = q.shape[0]
    tq = ATT_TQ
    nb = t // MOBA_BLOCK
    gw = ATT_REP * ATT_HEAD_DIM
    km = jnp.transpose(kmean.reshape(nb, ATT_KV_HEADS, ATT_HEAD_DIM), (1, 0, 2))
    km = jnp.pad(km, ((0, 0), (0, LANES - nb), (0, 0)))
    vcol = (ATT_Q_W + ATT_KV_W) // ATT_HEAD_DIM
    R = ATT_REP * tq
    return pl.pallas_call(
        functools.partial(_attn_body, tq=tq),
        grid=(ATT_KV_HEADS, t // tq),
        in_specs=[pl.BlockSpec((tq, gw), lambda g, i: (i, g)),
                  pl.BlockSpec((t, ATT_HEAD_DIM), lambda g, i: (0, g)),
                  pl.BlockSpec((t, ATT_HEAD_DIM), lambda g, i: (0, vcol + g)),
                  pl.BlockSpec((None, LANES, ATT_HEAD_DIM), lambda g, i: (g, 0, 0))],
        out_specs=pl.BlockSpec((tq, gw), lambda g, i: (i, g)),
        out_shape=jax.ShapeDtypeStruct((t, ATT_Q_W), BF16),
        scratch_shapes=[pltpu.VMEM((R, 1), F32), pltpu.VMEM((R, 1), F32),
                        pltpu.VMEM((R, ATT_HEAD_DIM), F32)],
        compiler_params=_cparams("parallel", "parallel"),
        name="moba_attention",
    )(q, k, qkv, km)


def _router_body(h_ref, g_ref, wr_ref, br_ref, xn_ref, re_ref, rw_ref):
    h = h_ref[...]
    ms = jnp.mean(h * h, axis=-1, keepdims=True)
    xn = h * lax.rsqrt(ms + NORM_EPS) * g_ref[...]
    xn_ref[...] = xn
    logits = jnp.dot(xn, wr_ref[...], precision=HIGHEST, preferred_element_type=F32) + br_ref[...]
    lane = lax.broadcasted_iota(I32, logits.shape, 1)
    neg = -jnp.inf

    is_g = lane < MOE_GROUPS
    gl = jnp.where(is_g, logits, neg)
    ge = jnp.exp(gl - jnp.max(gl, axis=1, keepdims=True))
    gp = ge / jnp.sum(ge, axis=1, keepdims=True)
    g_w = jnp.max(gp, axis=1, keepdims=True)
    g_idx = jnp.min(jnp.where(is_g & (gp == g_w), lane, LANES), axis=1, keepdims=True)

    e_lo = MOE_GROUPS + MOE_EPG * g_idx
    is_e = (lane >= e_lo) & (lane < e_lo + MOE_EPG)
    el = jnp.where(is_e, logits, neg)
    ee = jnp.exp(el - jnp.max(el, axis=1, keepdims=True))
    ep = jnp.where(is_e, ee / jnp.sum(ee, axis=1, keepdims=True), -1.0)
    p1 = jnp.max(ep, axis=1, keepdims=True)
    i1 = jnp.min(jnp.where(ep == p1, lane, LANES), axis=1, keepdims=True)
    ep2 = jnp.where(lane == i1, -1.0, ep)
    p2 = jnp.max(ep2, axis=1, keepdims=True)
    i2 = jnp.min(jnp.where(ep2 == p2, lane, LANES), axis=1, keepdims=True)
    den = p1 + p2
    w1 = g_w * (p1 / den)
    w2 = g_w * (p2 / den)
    re_ref[...] = jnp.where(lane == 0, i1 - MOE_GROUPS, jnp.where(lane == 1, i2 - MOE_GROUPS, 0))
    rw_ref[...] = jnp.where(lane == 0, w1, jnp.where(lane == 1, w2, 0.0))


def router_call(h, gain, w_group, b_group, w_expert, b_expert):
    t, d = h.shape
    nr = MOE_GROUPS + MOE_EXPERTS
    wr = jnp.zeros((d, LANES), F32).at[:, :MOE_GROUPS].set(w_group).at[:, MOE_GROUPS:nr].set(w_expert)
    br = jnp.zeros((1, LANES), F32).at[0, :MOE_GROUPS].set(b_group).at[0, MOE_GROUPS:nr].set(b_expert)
    return pl.pallas_call(
        _router_body,
        grid=(t // NORM_TM,),
        in_specs=[pl.BlockSpec((NORM_TM, d), lambda i: (i, 0)),
                  pl.BlockSpec((1, d), lambda i: (0, 0)),
                  pl.BlockSpec((d, LANES), lambda i: (0, 0)),
                  pl.BlockSpec((1, LANES), lambda i: (0, 0))],
        out_specs=[pl.BlockSpec((NORM_TM, d), lambda i: (i, 0)),
                   pl.BlockSpec((NORM_TM, LANES), lambda i: (i, 0)),
                   pl.BlockSpec((NORM_TM, LANES), lambda i: (i, 0))],
        out_shape=[jax.ShapeDtypeStruct((t, d), F32),
                   jax.ShapeDtypeStruct((t, LANES), I32),
                   jax.ShapeDtypeStruct((t, LANES), F32)],
        compiler_params=_cparams("parallel"),
        name="norm_router",
    )(h, gain.reshape(1, d), wr, br)


def _row_copy(src_ref, dst_ref, sem, src_row, dst_row):
    return pltpu.make_async_copy(src_ref.at[pl.ds(src_row, 1), :], dst_ref.at[pl.ds(dst_row, 1), :], sem)


def _gather_rows(idx_ref, src_ref, dst_ref, sem, n):
    def start(r, c):
        _row_copy(src_ref, dst_ref, sem, idx_ref[0, r], r).start()
        return c

    def wait(r, c):
        _row_copy(src_ref, dst_ref, sem, 0, r).wait()
        return c

    lax.fori_loop(0, n, start, 0)
    lax.fori_loop(0, n, wait, 0)


def _dispatch_body(idx_ref, src_ref, o_ref, sem):
    _gather_rows(idx_ref, src_ref, o_ref, sem, o_ref.shape[0])


def dispatch_call(src, row_token):
    n_rows = row_token.shape[0]
    d = src.shape[1]
    tm = MOE_TM
    return pl.pallas_call(
        _dispatch_body,
        grid=(n_rows // tm,),
        in_specs=[pl.BlockSpec((None, 1, tm), lambda i: (i, 0, 0), memory_space=pltpu.SMEM),
                  pl.BlockSpec(memory_space=pl.ANY)],
        out_specs=pl.BlockSpec((tm, d), lambda i: (i, 0)),
        out_shape=jax.ShapeDtypeStruct((n_rows, d), src.dtype),
        scratch_shapes=[pltpu.SemaphoreType.DMA(())],
        compiler_params=_cparams("arbitrary"),
        name="moe_dispatch",
    )(row_token.reshape(n_rows // tm, 1, tm), src)


def _ffn_body(be_ref, nv_ref, x_ref, wg_ref, wu_ref, wd_ref, o_ref):
    @pl.when(pl.program_id(0) < nv_ref[0])
    def _():
        x = x_ref[...].astype(BF16)
        gate = jnp.dot(x, wg_ref[...], preferred_element_type=F32)
        up = jnp.dot(x, wu_ref[...], preferred_element_type=F32)
        hid = (_silu(gate) * up).astype(BF16)
        o_ref[...] = jnp.dot(hid, wd_ref[...], preferred_element_type=F32).astype(o_ref.dtype)


def ffn_call(xr, blk_expert, n_valid, w_gate, w_up, w_down):
    n_rows, d = xr.shape
    tm = MOE_TM
    ff = w_gate.shape[2]

    def row_map(b, be, nv):
        return (jnp.minimum(b, nv[0] - 1), 0)

    def w_map(b, be, nv):
        return (be[jnp.minimum(b, nv[0] - 1)], 0, 0)

    grid_spec = pltpu.PrefetchScalarGridSpec(
        num_scalar_prefetch=2,
        grid=(n_rows // tm,),
        in_specs=[pl.BlockSpec((tm, d), row_map),
                  pl.BlockSpec((None, d, ff), w_map),
                  pl.BlockSpec((None, d, ff), w_map),
                  pl.BlockSpec((None, ff, d), w_map)],
        out_specs=pl.BlockSpec((tm, d), row_map),
    )
    return pl.pallas_call(
        _ffn_body,
        grid_spec=grid_spec,
        out_shape=jax.ShapeDtypeStruct((n_rows, d), F32),
        compiler_params=_cparams("arbitrary"),
        name="moe_ffn",
    )(blk_expert, n_valid, xr, w_gate, w_up, w_down)


def _combine_body(pos_ref, y_ref, h_ref, rw_ref, g_ref, *out_refs_and_scratch, with_norm):
    if with_norm:
        ho_ref, xn_ref, buf_ref, sem = out_refs_and_scratch
    else:
        ho_ref, buf_ref, sem = out_refs_and_scratch
    tb = ho_ref.shape[0]
    _gather_rows(pos_ref, y_ref, buf_ref, sem, MOE_TOPK * tb)
    rw = rw_ref[...]
    ffn = rw[:, 0:1] * buf_ref[0:tb, :] + rw[:, 1:2] * buf_ref[tb:2 * tb, :]
    hn = h_ref[...] + ffn
    ho_ref[...] = hn
    if with_norm:
        ms = jnp.mean(hn * hn, axis=-1, keepdims=True)
        xn_ref[...] = (hn * lax.rsqrt(ms + NORM_EPS) * g_ref[...]).astype(xn_ref.dtype)


def combine_call(y, pos, h, route_w, next_gain):
    t, d = h.shape
    tb = NORM_TM // 2
    with_norm = next_gain is not None
    gain = (next_gain if with_norm else jnp.ones((d,), F32)).reshape(1, d)
    out_specs = [pl.BlockSpec((tb, d), lambda i: (i, 0))]
    out_shape = [jax.ShapeDtypeStruct((t, d), F32)]
    if with_norm:
        out_specs.append(pl.BlockSpec((tb, d), lambda i: (i, 0)))
        out_shape.append(jax.ShapeDtypeStruct((t, d), BF16))
    outs = pl.pallas_call(
        functools.partial(_combine_body, with_norm=with_norm),
        grid=(t // tb,),
        in_specs=[pl.BlockSpec((None, 1, MOE_TOPK * tb), lambda i: (i, 0, 0), memory_space=pltpu.SMEM),
                  pl.BlockSpec(memory_space=pl.ANY),
                  pl.BlockSpec((tb, d), lambda i: (i, 0)),
                  pl.BlockSpec((tb, LANES), lambda i: (i, 0)),
                  pl.BlockSpec((1, d), lambda i: (0, 0))],
        out_specs=out_specs,
        out_shape=out_shape,
        scratch_shapes=[pltpu.VMEM((MOE_TOPK * tb, d), F32), pltpu.SemaphoreType.DMA(())],
        compiler_params=_cparams("arbitrary"),
        name="moe_combine",
    )(pos, y, h, route_w, gain)
    return (outs[0], outs[1]) if with_norm else (outs[0], None)


def _dispatch_tables(route_e, t):
    tm = MOE_TM
    n_assign = MOE_TOPK * t
    expert = route_e[:, :MOE_TOPK].reshape(n_assign)
    onehot = (expert[:, None] == jnp.arange(MOE_EXPERTS, dtype=I32)[None, :]).astype(I32)
    rank = jnp.take_along_axis(jnp.cumsum(onehot, axis=0), expert[:, None], axis=1)[:, 0] - 1
    counts = jnp.sum(onehot, axis=0)
    padded = (counts + tm - 1) // tm * tm
    pend = jnp.cumsum(padded)
    pstart = pend - padded
    dest = pstart[expert] + rank
    n_rows = n_assign + MOE_EXPERTS * tm
    token = jnp.arange(n_assign, dtype=I32) // MOE_TOPK
    row_token = jnp.zeros((n_rows,), I32).at[dest].set(token)
    n_blk = n_rows // tm
    blk_expert = jnp.minimum(jnp.searchsorted(pend, jnp.arange(n_blk, dtype=I32) * tm, side='right'),
                             MOE_EXPERTS - 1).astype(I32)
    n_valid = (pend[-1] // tm).astype(I32).reshape(1)
    tb = NORM_TM // 2
    pos = dest.reshape(t // tb, tb, MOE_TOPK).transpose(0, 2, 1).reshape(t // tb, 1, MOE_TOPK * tb)
    return row_token, blk_expert, n_valid, pos.astype(I32)


def moe_layer(h, gain, w_group, b_group, w_expert, b_expert, w_gate, w_up, w_down, next_gain):
    t = h.shape[0]
    xn, route_e, route_w = router_call(h, gain, w_group, b_group, w_expert, b_expert)
    row_token, blk_expert, n_valid, pos = _dispatch_tables(route_e, t)
    xr = dispatch_call(xn, row_token)
    y = ffn_call(xr, blk_expert, n_valid, w_gate.astype(BF16), w_up.astype(BF16), w_down.astype(BF16))
    return combine_call(y, pos, h, route_w, next_gain)


def hybrid_layer(h, xn, w_in, conv_w, conv_b, dt_bias, a_log, d_skip, ssd_norm, ln_g, ln_b, ws, bs, w_out):
    w_main = jnp.concatenate([w_in[:, :OFF_XBC], w_in[:, OFF_U:], w_in[:, OFF_XBC:OFF_DT]],
                             axis=1).astype(BF16)
    hp = SSD_HEADS_PER_GROUP
    w_dt = jnp.zeros((D_MODEL, SSD_GROUPS, LANES), F32).at[:, :, :hp].set(
        w_in[:, OFF_DT:OFF_U].reshape(D_MODEL, SSD_GROUPS, hp)).reshape(D_MODEL, SSD_GROUPS * LANES)
    proj = matmul_call([xn], w_main, out_dtype=BF16, name="hyb_in_proj")
    dt_raw = matmul_call([xn], w_dt.astype(BF16), out_dtype=F32, tn=SSD_GROUPS * LANES, name="hyb_dt_proj")
    dtt_raw = dt_raw.reshape(-1, SSD_GROUPS, LANES)[:, :, :hp].reshape(-1, SSD_HEADS).T
    y_a = ssd_call(proj, dt_raw, dtt_raw, conv_w, conv_b, dt_bias, a_log, d_skip, ssd_norm)
    y_b = gmlp_call(proj, ln_g, ln_b, ws, bs)
    return matmul_call([y_a, y_b], w_out.astype(BF16), res=h, name="hyb_out_proj")


def moba_layer(h, xn, w_qkv, q_norm, k_norm, w_out, cosf, sinf):
    qkv = matmul_call([xn], w_qkv.astype(BF16), out_dtype=BF16, name="att_qkv_proj")
    q, k, kmean = qk_prep_call(qkv, q_norm, k_norm, cosf, sinf)
    o = attention_call(q, k, qkv, kmean)
    return matmul_call([o], w_out.astype(BF16), res=h, name="att_out_proj")


def _rope_tables(t):
    inv = 1.0 / (ROPE_THETA ** (jnp.arange(0, ATT_HEAD_DIM, 2, dtype=F32) / ATT_HEAD_DIM))
    ang = jnp.arange(t, dtype=F32)[:, None] * inv[None, :]
    cos, sin = jnp.cos(ang), jnp.sin(ang)
    return jnp.concatenate([cos, cos], axis=1), jnp.concatenate([-sin, sin], axis=1)


def kernel(x, norm_mix, norm_ffn, hyb_w_in, ssd_conv_w, ssd_conv_b, ssd_dt_bias, ssd_a_log, ssd_d, ssd_norm, gm_ln_g, gm_ln_b, gm_ws, gm_bs, hyb_w_out, att_w_qkv, att_q_norm, att_k_norm, att_w_out, moe_w_group, moe_b_group, moe_w_expert, moe_b_expert, moe_w_gate, moe_w_up, moe_w_down):
    bsz, t, d = x.shape
    depth = norm_mix.shape[0]
    cosf, sinf = _rope_tables(t)
    outs = []
    for b in range(bsz):
        h = x[b]
        xn = rmsnorm_call(h, norm_mix[0])
        for layer in range(depth):
            j = layer // 2
            if layer % 2 == 0:
                h = hybrid_layer(h, xn, hyb_w_in[j], ssd_conv_w[j], ssd_conv_b[j], ssd_dt_bias[j],
                                 ssd_a_log[j], ssd_d[j], ssd_norm[j], gm_ln_g[j], gm_ln_b[j],
                                 gm_ws[j], gm_bs[j], hyb_w_out[j])
            else:
                h = moba_layer(h, xn, att_w_qkv[j], att_q_norm[j], att_k_norm[j], att_w_out[j], cosf, sinf)
            next_gain = norm_mix[layer + 1] if layer + 1 < depth else None
            h, xn = moe_layer(h, norm_ffn[layer], moe_w_group[layer], moe_b_group[layer],
                              moe_w_expert[layer], moe_b_expert[layer], moe_w_gate[layer],
                              moe_w_up[layer], moe_w_down[layer], next_gain)
        outs.append(h)
    return jnp.stack(outs, axis=0)
```

```python
import functools

import jax
import jax.numpy as jnp
from jax import lax
from jax.experimental import pallas as pl
from jax.experimental.pallas import tpu as pltpu

F32 = jnp.float32
BF16 = jnp.bfloat16
I32 = jnp.int32

D_MODEL = 4096
NORM_EPS = 1e-6

SSD_INNER = 2048
SSD_HEAD_DIM = 64
SSD_HEADS = 32
SSD_GROUPS = 4
SSD_STATE = 128
SSD_CONV = 4
SSD_CHUNK = 256
SSD_BC = SSD_GROUPS * SSD_STATE
SSD_CONV_CH = SSD_INNER + 2 * SSD_BC
SSD_GROUP_W = SSD_INNER // SSD_GROUPS
SSD_HEADS_PER_GROUP = SSD_HEADS // SSD_GROUPS

GM_INNER = 2048
GM_CHUNK = 128
GM_GROUPS = 16
GM_GROUP_DIM = 128

OFF_XBC = SSD_INNER
OFF_DT = OFF_XBC + SSD_CONV_CH
OFF_U = OFF_DT + SSD_HEADS
OFF_V = OFF_U + GM_INNER

ATT_HEAD_DIM = 128
ATT_HEADS = 32
ATT_KV_HEADS = 8
ATT_REP = ATT_HEADS // ATT_KV_HEADS
ATT_Q_W = ATT_HEADS * ATT_HEAD_DIM
ATT_KV_W = ATT_KV_HEADS * ATT_HEAD_DIM
ROPE_THETA = 10000.0
MOBA_BLOCK = 256
MOBA_TOPK = 3

MOE_GROUPS = 4
MOE_EPG = 4
MOE_EXPERTS = 16
MOE_TOPK = 2
MOE_FF = 768

LANES = 128
VMEM_LIMIT_BYTES = 56 * 1024 * 1024

MM_TM = 512
MM_TN = 1024
NORM_TM = 256
ATT_KAUG_W = ATT_HEAD_DIM + LANES
ATT_VT_ROWS = ATT_HEAD_DIM + 16
MASK_BIAS = -1e30
ATT_NB = 2
QK_SCALE_LOG2E = ATT_HEAD_DIM ** -0.5 * 1.4426950408889634
MOE_TM = 256
HIGHEST = lax.Precision.HIGHEST


def _cparams(*sem, flags=None):
    return pltpu.CompilerParams(dimension_semantics=sem, vmem_limit_bytes=VMEM_LIMIT_BYTES, flags=flags)


def _silu(x):
    return x / (1.0 + jnp.exp(-x))


def _softplus(x):
    return jnp.maximum(x, 0.0) + jnp.log1p(jnp.exp(-jnp.abs(x)))


def _gelu_erf(x):
    return 0.5 * x * (1.0 + lax.erf(x * (2.0 ** -0.5)))


def _rmsnorm_body(x_ref, g_ref, o_ref):
    x = x_ref[...]
    ms = jnp.mean(x * x, axis=-1, keepdims=True)
    o_ref[...] = (x * lax.rsqrt(ms + NORM_EPS) * g_ref[...]).astype(o_ref.dtype)


def rmsnorm_call(x, g):
    t, d = x.shape
    return pl.pallas_call(
        _rmsnorm_body,
        grid=(t // NORM_TM,),
        in_specs=[pl.BlockSpec((NORM_TM, d), lambda i: (i, 0)),
                  pl.BlockSpec((1, d), lambda i: (0, 0))],
        out_specs=pl.BlockSpec((NORM_TM, d), lambda i: (i, 0)),
        out_shape=jax.ShapeDtypeStruct((t, d), BF16),
        compiler_params=_cparams("parallel"),
        name="rmsnorm",
    )(x, g.reshape(1, d))


def _matmul_body(*refs, n_parts, has_res):
    a_refs = refs[:n_parts]
    w_refs = refs[n_parts:2 * n_parts]
    o_ref = refs[-1]
    acc = jnp.dot(a_refs[0][...], w_refs[0][...], preferred_element_type=F32)
    for p in range(1, n_parts):
        acc = acc + jnp.dot(a_refs[p][...], w_refs[p][...], preferred_element_type=F32)
    if has_res:
        acc = acc + refs[2 * n_parts][...]
    o_ref[...] = acc.astype(o_ref.dtype)


def matmul_call(a_parts, w, res=None, out_dtype=F32, tn=MM_TN, name="matmul"):
    n_parts = len(a_parts)
    m, kp = a_parts[0].shape
    n = w.shape[1]
    tm = MM_TM
    in_specs = [pl.BlockSpec((tm, kp), lambda j, i: (i, 0)) for _ in a_parts]
    in_specs += [pl.BlockSpec((kp, tn), functools.partial(lambda j, i, p: (p, j), p=p))
                 for p in range(n_parts)]
    args = list(a_parts) + [w] * n_parts
    if res is not None:
        in_specs.append(pl.BlockSpec((tm, tn), lambda j, i: (i, j)))
        args.append(res)
    return pl.pallas_call(
        functools.partial(_matmul_body, n_parts=n_parts, has_res=res is not None),
        grid=(n // tn, m // tm),
        in_specs=in_specs,
        out_specs=pl.BlockSpec((tm, tn), lambda j, i: (i, j)),
        out_shape=jax.ShapeDtypeStruct((m, n), out_dtype),
        compiler_params=_cparams("parallel", "parallel"),
        name=name,
    )(*args)


def _ssd_body(z_ref, xs_ref, b_ref, c_ref, dt_ref, dtt_ref,
              wx_ref, bx_ref, wb_ref, bb_ref, wc_ref, bc_ref,
              dtb_ref, dtbt_ref, alog_ref, alogt_ref, dskip_ref, ng_ref, expand_ref,
              o_ref, state_ref, tx_ref, tb_ref, tc_ref):
    c = pl.program_id(1)
    L = SSD_CHUNK
    HP = SSD_HEADS_PER_GROUP

    @pl.when(c == 0)
    def _():
        state_ref[...] = jnp.zeros_like(state_ref)
        tx_ref[...] = jnp.zeros_like(tx_ref)
        tb_ref[...] = jnp.zeros_like(tb_ref)
        tc_ref[...] = jnp.zeros_like(tc_ref)

    def conv_silu(cur_ref, tail_ref, w_ref, bias_ref):
        cur = cur_ref[...].astype(F32)
        ext = jnp.concatenate([tail_ref[...], cur], axis=0)
        w = w_ref[...]
        acc = bias_ref[...] + w[SSD_CONV - 1:SSD_CONV, :] * cur
        for j in range(1, SSD_CONV):
            acc = acc + w[SSD_CONV - 1 - j:SSD_CONV - j, :] * ext[8 - j:8 - j + L, :]
        tail_ref[...] = cur[L - 8:L, :]
        return _silu(acc)

    xs = conv_silu(xs_ref, tx_ref, wx_ref, bx_ref)
    bm = conv_silu(b_ref, tb_ref, wb_ref, bb_ref)
    cm = conv_silu(c_ref, tc_ref, wc_ref, bc_ref)

    dt = _softplus(dt_ref[...] + dtb_ref[...])
    da = dt * (-jnp.exp(alog_ref[...]))
    dtt = _softplus(dtt_ref[...] + dtbt_ref[...])
    dat = dtt * (-jnp.exp(alogt_ref[...]))

    row = lax.broadcasted_iota(I32, (L, L), 0)
    col = lax.broadcasted_iota(I32, (L, L), 1)
    causal = row >= col
    tril = causal.astype(F32)
    cs = jnp.dot(tril, da, precision=HIGHEST, preferred_element_type=F32)
    cst = jnp.dot(dat, (row <= col).astype(F32), precision=HIGHEST,
                  preferred_element_type=F32)
    cs_last = cs[L - 1:L, :]
    to_end = jnp.exp(cs_last - cs)
    ecs = jnp.exp(cs)

    expand = expand_ref[...]
    dt_e = jnp.dot(dt, expand, precision=HIGHEST, preferred_element_type=F32)
    to_end_e = jnp.dot(to_end, expand, precision=HIGHEST, preferred_element_type=F32)
    ecs_e = jnp.dot(ecs, expand, precision=HIGHEST, preferred_element_type=F32)

    x = xs * dt_e
    bm16 = bm.astype(BF16)
    cm16 = cm.astype(BF16)
    cb = lax.dot_general(cm16, bm16, (((1,), (1,)), ((), ())), preferred_element_type=F32)

    prev = state_ref[...]
    y = jnp.dot(cm16, prev.astype(BF16), preferred_element_type=F32) * ecs_e

    lane = lax.broadcasted_iota(I32, (L, SSD_GROUP_W), 1)
    for h in range(HP):
        seg = cs[:, h:h + 1] - cst[h:h + 1, :]
        decay = jnp.exp(jnp.where(causal, seg, -jnp.inf))
        m16 = (cb * decay).astype(BF16)
        in_head = (lane >= h * SSD_HEAD_DIM) & (lane < (h + 1) * SSD_HEAD_DIM)
        xh = jnp.where(in_head, x, 0.0).astype(BF16)
        y = y + jnp.dot(m16, xh, preferred_element_type=F32)

    xw = (x * to_end_e).astype(BF16)
    state_ref[...] = prev * ecs_e[L - 1:L, :] + jnp.dot(bm.T.astype(BF16), xw,
                                                        preferred_element_type=F32)

    y = y + xs * dskip_ref[...]
    y = y * _silu(z_ref[...].astype(F32))
    ms = jnp.mean(y * y, axis=-1, keepdims=True)
    o_ref[...] = (y * lax.rsqrt(ms + NORM_EPS) * ng_ref[...]).astype(o_ref.dtype)


def ssd_call(proj, dt_raw, dtt_raw, conv_w, conv_b, dt_bias, a_log, d_skip, norm_g):
    t = proj.shape[0]
    L = SSD_CHUNK
    gw = SSD_GROUP_W
    xs0 = (SSD_INNER + 2 * GM_INNER) // gw
    b0 = (SSD_INNER + 2 * GM_INNER + SSD_INNER) // SSD_STATE
    c0 = b0 + SSD_GROUPS
    hp = SSD_HEADS_PER_GROUP

    dtb = jnp.zeros((SSD_GROUPS, LANES), F32).at[:, :hp].set(dt_bias.reshape(SSD_GROUPS, hp))
    alog = jnp.zeros((SSD_GROUPS, LANES), F32).at[:, :hp].set(a_log.reshape(SSD_GROUPS, hp))
    dtb = dtb.reshape(1, SSD_GROUPS * LANES)
    alog = alog.reshape(1, SSD_GROUPS * LANES)
    dtbt = dt_bias.reshape(SSD_HEADS, 1)
    alogt = a_log.reshape(SSD_HEADS, 1)
    dskip = jnp.repeat(d_skip, SSD_HEAD_DIM).reshape(1, SSD_INNER)
    expand = (jnp.arange(LANES)[:, None] == (jnp.arange(gw)[None, :] // SSD_HEAD_DIM)).astype(F32)
    cw = conv_w.astype(F32)
    cbias = conv_b.reshape(1, SSD_CONV_CH).astype(F32)
    xb0 = SSD_INNER // SSD_STATE
    cb0 = xb0 + SSD_GROUPS

    in_specs = [
        pl.BlockSpec((L, gw), lambda g, c: (c, g)),
        pl.BlockSpec((L, gw), lambda g, c: (c, xs0 + g)),
        pl.BlockSpec((L, SSD_STATE), lambda g, c: (c, b0 + g)),
        pl.BlockSpec((L, SSD_STATE), lambda g, c: (c, c0 + g)),
        pl.BlockSpec((L, LANES), lambda g, c: (c, g)),
        pl.BlockSpec((hp, L), lambda g, c: (g, c)),
        pl.BlockSpec((SSD_CONV, gw), lambda g, c: (0, g)),
        pl.BlockSpec((1, gw), lambda g, c: (0, g)),
        pl.BlockSpec((SSD_CONV, SSD_STATE), lambda g, c: (0, xb0 + g)),
        pl.BlockSpec((1, SSD_STATE), lambda g, c: (0, xb0 + g)),
        pl.BlockSpec((SSD_CONV, SSD_STATE), lambda g, c: (0, cb0 + g)),
        pl.BlockSpec((1, SSD_STATE), lambda g, c: (0, cb0 + g)),
        pl.BlockSpec((1, LANES), lambda g, c: (0, g)),
        pl.BlockSpec((hp, 1), lambda g, c: (g, 0)),
        pl.BlockSpec((1, LANES), lambda g, c: (0, g)),
        pl.BlockSpec((hp, 1), lambda g, c: (g, 0)),
        pl.BlockSpec((1, gw), lambda g, c: (0, g)),
        pl.BlockSpec((1, gw), lambda g, c: (0, g)),
        pl.BlockSpec((LANES, gw), lambda g, c: (0, 0)),
    ]
    return pl.pallas_call(
        _ssd_body,
        grid=(SSD_GROUPS, t // L),
        in_specs=in_specs,
        out_specs=pl.BlockSpec((L, gw), lambda g, c: (c, g)),
        out_shape=jax.ShapeDtypeStruct((t, SSD_INNER), BF16),
        scratch_shapes=[pltpu.VMEM((SSD_STATE, gw), F32),
                        pltpu.VMEM((8, gw), F32),
                        pltpu.VMEM((8, SSD_STATE), F32),
                        pltpu.VMEM((8, SSD_STATE), F32)],
        compiler_params=_cparams("parallel", "arbitrary"),
        name="ssd",
    )(proj, proj, proj, proj, dt_raw, dtt_raw, cw, cbias, cw, cbias, cw, cbias,
      dtb, dtbt, alog, alogt, dskip, norm_g.reshape(1, SSD_INNER), expand)


def _gmlp_body(u_ref, v_ref, g_ref, b_ref, ws_ref, bst_ref, o_ref):
    u = _gelu_erf(u_ref[...].astype(F32))
    v = _gelu_erf(v_ref[...].astype(F32))
    mu = jnp.mean(v, axis=-1, keepdims=True)
    vc = v - mu
    var = jnp.mean(vc * vc, axis=-1, keepdims=True)
    vn = (vc * lax.rsqrt(var + NORM_EPS) * g_ref[...] + b_ref[...]).astype(BF16)
    row = lax.broadcasted_iota(I32, (GM_CHUNK, GM_CHUNK), 0)
    col = lax.broadcasted_iota(I32, (GM_CHUNK, GM_CHUNK), 1)
    causal = row >= col
    for g in range(GM_GROUPS):
        sl = slice(g * GM_GROUP_DIM, (g + 1) * GM_GROUP_DIM)
        w = jnp.where(causal, ws_ref[g], jnp.zeros((), BF16))
        sv = jnp.dot(w, vn[:, sl], preferred_element_type=F32) + bst_ref[:, sl]
        o_ref[:, sl] = (u[:, sl] * sv).astype(o_ref.dtype)


def gmlp_call(proj, ln_g, ln_b, ws, bs):
    t = proj.shape[0]
    ucol = SSD_INNER // GM_INNER
    bst = jnp.repeat(bs.T, GM_GROUP_DIM, axis=1)
    return pl.pallas_call(
        _gmlp_body,
        grid=(t // GM_CHUNK,),
        in_specs=[pl.BlockSpec((GM_CHUNK, GM_INNER), lambda i: (i, ucol)),
                  pl.BlockSpec((GM_CHUNK, GM_INNER), lambda i: (i, ucol + 1)),
                  pl.BlockSpec((1, GM_INNER), lambda i: (0, 0)),
                  pl.BlockSpec((1, GM_INNER), lambda i: (0, 0)),
                  pl.BlockSpec((GM_GROUPS, GM_CHUNK, GM_CHUNK), lambda i: (0, 0, 0)),
                  pl.BlockSpec((GM_CHUNK, GM_INNER), lambda i: (0, 0))],
        out_specs=pl.BlockSpec((GM_CHUNK, GM_INNER), lambda i: (i, 0)),
        out_shape=jax.ShapeDtypeStruct((t, GM_INNER), BF16),
        compiler_params=_cparams("parallel"),
        name="gmlp",
    )(proj, proj, ln_g.reshape(1, GM_INNER), ln_b.reshape(1, GM_INNER), ws.astype(BF16), bst)


def _rope_norm(x, gain, cosf, sinf):
    ms = jnp.mean(x * x, axis=-1, keepdims=True)
    xn = x * lax.rsqrt(ms + NORM_EPS) * gain
    return xn * cosf + pltpu.roll(xn, ATT_HEAD_DIM // 2, axis=1) * sinf


def _qprep_body(x_ref, g_ref, cos_ref, sin_ref, o_ref):
    for r in range(ATT_REP):
        x = x_ref[:, r * ATT_HEAD_DIM:(r + 1) * ATT_HEAD_DIM].astype(F32)
        y = _rope_norm(x, g_ref[...], cos_ref[...], sin_ref[...]) * QK_SCALE_LOG2E
        o_ref[r] = y.T.astype(o_ref.dtype)


def _kprep_body(x_ref, v_ref, g_ref, cos_ref, sin_ref, o_ref, vt_ref, mean_ref):
    y = _rope_norm(x_ref[...].astype(F32), g_ref[...], cos_ref[...], sin_ref[...])
    tb = y.shape[0]
    lane = lax.broadcasted_iota(I32, (tb, LANES), 1)
    o_ref[:, :ATT_HEAD_DIM] = y.astype(o_ref.dtype)
    o_ref[:, ATT_HEAD_DIM:] = jnp.where(lane == pl.program_id(0), 1.0, 0.0).astype(o_ref.dtype)
    mean_ref[...] = jnp.mean(y, axis=0, keepdims=True)
    vt_ref[:ATT_HEAD_DIM, :] = v_ref[...].astype(F32).T.astype(vt_ref.dtype)
    vt_ref[ATT_HEAD_DIM:, :] = jnp.ones((ATT_VT_ROWS - ATT_HEAD_DIM, tb), vt_ref.dtype)


def qk_prep_call(qkv, q_norm, k_norm, cosf, sinf):
    t = qkv.shape[0]
    tb = MOBA_BLOCK
    gw = ATT_REP * ATT_HEAD_DIM
    common = [pl.BlockSpec((1, ATT_HEAD_DIM), lambda i, h: (0, 0)),
              pl.BlockSpec((tb, ATT_HEAD_DIM), lambda i, h: (i, 0)),
              pl.BlockSpec((tb, ATT_HEAD_DIM), lambda i, h: (i, 0))]
    qt = pl.pallas_call(
        _qprep_body,
        grid=(t // tb, ATT_KV_HEADS),
        in_specs=[pl.BlockSpec((tb, gw), lambda i, h: (i, h))] + common,
        out_specs=pl.BlockSpec((ATT_REP, ATT_HEAD_DIM, tb), lambda i, h: (h, 0, i)),
        out_shape=jax.ShapeDtypeStruct((ATT_HEADS, ATT_HEAD_DIM, t), BF16),
        compiler_params=_cparams("parallel", "parallel"),
        name="q_prep",
    )(qkv, q_norm.reshape(1, ATT_HEAD_DIM), cosf, sinf)
    vcol = (ATT_Q_W + ATT_KV_W) // ATT_HEAD_DIM
    k, vt, kmean = pl.pallas_call(
        _kprep_body,
        grid=(t // tb, ATT_KV_HEADS),
        in_specs=[pl.BlockSpec((tb, ATT_HEAD_DIM), lambda i, h: (i, ATT_HEADS + h)),
                  pl.BlockSpec((tb, ATT_HEAD_DIM), lambda i, h: (i, vcol + h))] + common,
        out_specs=[pl.BlockSpec((tb, ATT_KAUG_W), lambda i, h: (i, h)),
                   pl.BlockSpec((None, None, ATT_VT_ROWS, tb), lambda i, h: (h, i, 0, 0)),
                   pl.BlockSpec((None, None, 1, ATT_HEAD_DIM), lambda i, h: (i, h, 0, 0))],
        out_shape=[jax.ShapeDtypeStruct((t, ATT_KV_HEADS * ATT_KAUG_W), BF16),
                   jax.ShapeDtypeStruct((ATT_KV_HEADS, t // tb, ATT_VT_ROWS, tb), BF16),
                   jax.ShapeDtypeStruct((t // tb, ATT_KV_HEADS, 1, ATT_HEAD_DIM), F32)],
        compiler_params=_cparams("parallel", "parallel"),
        name="k_prep",
    )(qkv, qkv, k_norm.reshape(1, ATT_HEAD_DIM), cosf, sinf)
    return qt, k, vt, kmean


def _attn_body(qt_ref, k_ref, vt_ref, km_ref, o_ref, acc_ref, qa_ref, s_ref, *, nb):
    qblk = pl.program_id(1)
    blk = MOBA_BLOCK
    hd = ATT_HEAD_DIM
    n_blocks = k_ref.shape[0] // blk
    nbp = (n_blocks + 7) // 8 * 8
    own0 = pl.multiple_of(qblk * blk, blk)
    km = km_ref[...]
    km_hi = km.astype(BF16)
    km_lo = (km - km_hi.astype(F32)).astype(BF16)
    bid = lax.broadcasted_iota(I32, (nbp, blk), 0)
    causal = lax.broadcasted_iota(I32, (blk, blk), 0) <= lax.broadcasted_iota(I32, (blk, blk), 1)

    m0 = []
    for r in range(ATT_REP):
        qt = qt_ref[r]
        gate = (jnp.dot(km_hi, qt, preferred_element_type=F32)
                + jnp.dot(km_lo, qt, preferred_element_type=F32))[:nbp]
        gate = jnp.where(bid < qblk, gate, -jnp.inf)
        picked = bid < 0
        for kk in range(MOBA_TOPK):
            mx = jnp.max(gate, axis=0, keepdims=True)
            idx = jnp.min(jnp.where(gate == mx, bid, LANES), axis=0, keepdims=True)
            hit = bid == idx
            picked = picked | (hit & (kk < qblk))
            gate = jnp.where(hit, -jnp.inf, gate)
        bias = jnp.where(picked, 0.0, MASK_BIAS).astype(BF16)
        qa_ref[r, :hd, :] = qt
        qa_ref[r, hd:hd + nbp, :] = bias
        if nbp < LANES:
            qa_ref[r, hd + nbp:, :] = jnp.zeros((LANES - nbp, blk), BF16)
        s = jnp.dot(k_ref[pl.ds(own0, blk), :hd], qt, preferred_element_type=F32)
        s = jnp.where(causal, s, -jnp.inf)
        m = jnp.max(s, axis=0, keepdims=True)
        p = jnp.exp2(s - m)
        m0.append(m)
        acc_ref[r] = jnp.dot(vt_ref[qblk], p.astype(BF16), preferred_element_type=F32)

    def blocks_of(jj):
        return [jnp.minimum(jj * nb + u, n_blocks - 1) for u in range(nb)]

    def head_scores(r, blocks):
        return [jnp.dot(k_ref[pl.ds(pl.multiple_of(ja * blk, blk), blk), :], qa_ref[r],
                        preferred_element_type=F32) for ja in blocks]

    for u, su in enumerate(head_scores(0, blocks_of(0))):
        s_ref[u] = su

    def past_blocks(jj, ms):
        blocks = blocks_of(jj)
        vts = [vt_ref[ja] for ja in blocks]
        m_out = []
        pending = [s_ref[u] for u in range(nb)]
        for r in range(ATT_REP):
            scores = pending
            if r + 1 < ATT_REP:
                pending = head_scores(r + 1, blocks)
            else:
                for u, su in enumerate(head_scores(0, blocks_of(jj + 1))):
                    s_ref[u] = su
            m_new = ms[r]
            for su in scores:
                m_new = jnp.maximum(m_new, jnp.max(su, axis=0, keepdims=True))
            alpha = jnp.exp2(ms[r] - m_new)
            pv = None
            for u in range(nb):
                pu = jnp.exp2(scores[u] - m_new).astype(BF16)
                d = jnp.dot(vts[u], pu, preferred_element_type=F32)
                pv = d if pv is None else pv + d
            acc_ref[r] = alpha * acc_ref[r] + pv
            m_out.append(m_new)
        return tuple(m_out)

    lax.fori_loop(0, (qblk + nb - 1) // nb, past_blocks, tuple(m0))

    for r in range(ATT_REP):
        out = acc_ref[r, :hd, :] / acc_ref[r, hd:hd + 1, :]
        o_ref[:, r * hd:(r + 1) * hd] = out.T.astype(o_ref.dtype)


def attention_call(qt, k, vt, kmean):
    t = k.shape[0]
    tq = MOBA_BLOCK
    n_blocks = t // MOBA_BLOCK
    km = jnp.transpose(kmean.reshape(n_blocks, ATT_KV_HEADS, ATT_HEAD_DIM), (1, 0, 2))
    km = jnp.pad(km, ((0, 0), (0, LANES - n_blocks), (0, 0)))
    return pl.pallas_call(
        functools.partial(_attn_body, nb=ATT_NB),
        grid=(ATT_KV_HEADS, t // tq),
        in_specs=[pl.BlockSpec((ATT_REP, ATT_HEAD_DIM, tq), lambda g, i: (g, 0, i)),
                  pl.BlockSpec((t, ATT_KAUG_W), lambda g, i: (0, g)),
                  pl.BlockSpec((None, n_blocks, ATT_VT_ROWS, MOBA_BLOCK), lambda g, i: (g, 0, 0, 0)),
                  pl.BlockSpec((None, LANES, ATT_HEAD_DIM), lambda g, i: (g, 0, 0))],
        out_specs=pl.BlockSpec((tq, ATT_REP * ATT_HEAD_DIM), lambda g, i: (i, g)),
        out_shape=jax.ShapeDtypeStruct((t, ATT_Q_W), BF16),
        scratch_shapes=[pltpu.VMEM((ATT_REP, ATT_VT_ROWS, tq), F32),
                        pltpu.VMEM((ATT_REP, ATT_KAUG_W, tq), BF16),
                        pltpu.VMEM((ATT_NB, MOBA_BLOCK, tq), F32)],
        compiler_params=_cparams("parallel", "parallel"),
        name="moba_attention",
    )(qt, k, vt, km)


def _router_body(h_ref, g_ref, wr_ref, br_ref, xn_ref, re_ref, rw_ref):
    h = h_ref[...]
    ms = jnp.mean(h * h, axis=-1, keepdims=True)
    xn = h * lax.rsqrt(ms + NORM_EPS) * g_ref[...]
    xn_ref[...] = xn
    logits = jnp.dot(xn, wr_ref[...], precision=HIGHEST, preferred_element_type=F32) + br_ref[...]
    lane = lax.broadcasted_iota(I32, logits.shape, 1)
    neg = -jnp.inf

    is_g = lane < MOE_GROUPS
    gl = jnp.where(is_g, logits, neg)
    ge = jnp.exp(gl - jnp.max(gl, axis=1, keepdims=True))
    gp = ge / jnp.sum(ge, axis=1, keepdims=True)
    g_w = jnp.max(gp, axis=1, keepdims=True)
    g_idx = jnp.min(jnp.where(is_g & (gp == g_w), lane, LANES), axis=1, keepdims=True)

    e_lo = MOE_GROUPS + MOE_EPG * g_idx
    is_e = (lane >= e_lo) & (lane < e_lo + MOE_EPG)
    el = jnp.where(is_e, logits, neg)
    ee = jnp.exp(el - jnp.max(el, axis=1, keepdims=True))
    ep = jnp.where(is_e, ee / jnp.sum(ee, axis=1, keepdims=True), -1.0)
    p1 = jnp.max(ep, axis=1, keepdims=True)
    i1 = jnp.min(jnp.where(ep == p1, lane, LANES), axis=1, keepdims=True)
    ep2 = jnp.where(lane == i1, -1.0, ep)
    p2 = jnp.max(ep2, axis=1, keepdims=True)
    i2 = jnp.min(jnp.where(ep2 == p2, lane, LANES), axis=1, keepdims=True)
    den = p1 + p2
    w1 = g_w * (p1 / den)
    w2 = g_w * (p2 / den)
    re_ref[...] = jnp.where(lane == 0, i1 - MOE_GROUPS, jnp.where(lane == 1, i2 - MOE_GROUPS, 0))
    rw_ref[...] = jnp.where(lane == 0, w1, jnp.where(lane == 1, w2, 0.0))


def router_call(h, gain, w_group, b_group, w_expert, b_expert):
    t, d = h.shape
    nr = MOE_GROUPS + MOE_EXPERTS
    wr = jnp.zeros((d, LANES), F32).at[:, :MOE_GROUPS].set(w_group).at[:, MOE_GROUPS:nr].set(w_expert)
    br = jnp.zeros((1, LANES), F32).at[0, :MOE_GROUPS].set(b_group).at[0, MOE_GROUPS:nr].set(b_expert)
    return pl.pallas_call(
        _router_body,
        grid=(t // NORM_TM,),
        in_specs=[pl.BlockSpec((NORM_TM, d), lambda i: (i, 0)),
                  pl.BlockSpec((1, d), lambda i: (0, 0)),
                  pl.BlockSpec((d, LANES), lambda i: (0, 0)),
                  pl.BlockSpec((1, LANES), lambda i: (0, 0))],
        out_specs=[pl.BlockSpec((NORM_TM, d), lambda i: (i, 0)),
                   pl.BlockSpec((NORM_TM, LANES), lambda i: (i, 0)),
                   pl.BlockSpec((NORM_TM, LANES), lambda i: (i, 0))],
        out_shape=[jax.ShapeDtypeStruct((t, d), F32),
                   jax.ShapeDtypeStruct((t, LANES), I32),
                   jax.ShapeDtypeStruct((t, LANES), F32)],
        compiler_params=_cparams("parallel"),
        name="norm_router",
    )(h, gain.reshape(1, d), wr, br)


def _row_copy(src_ref, dst_ref, sem, src_row, dst_row):
    return pltpu.make_async_copy(src_ref.at[pl.ds(src_row, 1), :], dst_ref.at[pl.ds(dst_row, 1), :], sem)


def _gather_rows(idx_ref, src_ref, dst_ref, sem, n):
    def start(r, c):
        _row_copy(src_ref, dst_ref, sem, idx_ref[0, r], r).start()
        return c

    def wait(r, c):
        _row_copy(src_ref, dst_ref, sem, 0, r).wait()
        return c

    lax.fori_loop(0, n, start, 0)
    lax.fori_loop(0, n, wait, 0)


def _dispatch_body(idx_ref, src_ref, o_ref, sem):
    _gather_rows(idx_ref, src_ref, o_ref, sem, o_ref.shape[0])


def dispatch_call(src, row_token):
    n_rows = row_token.shape[0]
    d = src.shape[1]
    tm = MOE_TM
    return pl.pallas_call(
        _dispatch_body,
        grid=(n_rows // tm,),
        in_specs=[pl.BlockSpec((None, 1, tm), lambda i: (i, 0, 0), memory_space=pltpu.SMEM),
                  pl.BlockSpec(memory_space=pl.ANY)],
        out_specs=pl.BlockSpec((tm, d), lambda i: (i, 0)),
        out_shape=jax.ShapeDtypeStruct((n_rows, d), src.dtype),
        scratch_shapes=[pltpu.SemaphoreType.DMA(())],
        compiler_params=_cparams("arbitrary"),
        name="moe_dispatch",
    )(row_token.reshape(n_rows // tm, 1, tm), src)


def _ffn_body(be_ref, nv_ref, x_ref, wg_ref, wu_ref, wd_ref, o_ref):
    valid = pl.program_id(0) < nv_ref[0]

    @pl.when(valid)
    def _():
        x = x_ref[...].astype(BF16)
        gate = jnp.dot(x, wg_ref[...], preferred_element_type=F32)
        up = jnp.dot(x, wu_ref[...], preferred_element_type=F32)
        hid = (_silu(gate) * up).astype(BF16)
        o_ref[...] = jnp.dot(hid, wd_ref[...], preferred_element_type=F32).astype(o_ref.dtype)

    @pl.when(jnp.logical_not(valid))
    def _():
        o_ref[...] = jnp.zeros_like(o_ref)


def ffn_call(xr, blk_expert, n_valid, w_gate, w_up, w_down):
    n_rows, d = xr.shape
    tm = MOE_TM
    ff = w_gate.shape[2]

    def row_map(b, be, nv):
        return (jnp.minimum(b, nv[0] - 1), 0)

    def w_map(b, be, nv):
        return (be[jnp.minimum(b, nv[0] - 1)], 0, 0)

    grid_spec = pltpu.PrefetchScalarGridSpec(
        num_scalar_prefetch=2,
        grid=(n_rows // tm,),
        in_specs=[pl.BlockSpec((tm, d), row_map),
                  pl.BlockSpec((None, d, ff), w_map),
                  pl.BlockSpec((None, d, ff), w_map),
                  pl.BlockSpec((None, ff, d), w_map)],
        out_specs=pl.BlockSpec((tm, d), lambda b, be, nv: (b, 0)),
    )
    return pl.pallas_call(
        _ffn_body,
        grid_spec=grid_spec,
        out_shape=jax.ShapeDtypeStruct((n_rows, d), F32),
        compiler_params=_cparams("arbitrary"),
        name="moe_ffn",
    )(blk_expert, n_valid, xr, w_gate, w_up, w_down)


def _combine_body(pos_ref, y_ref, h_ref, rw_ref, g_ref, *out_refs_and_scratch, with_norm):
    if with_norm:
        ho_ref, xn_ref, buf_ref, sem = out_refs_and_scratch
    else:
        ho_ref, buf_ref, sem = out_refs_and_scratch
    tb = ho_ref.shape[0]
    _gather_rows(pos_ref, y_ref, buf_ref, sem, MOE_TOPK * tb)
    rw = rw_ref[...]
    ffn = rw[:, 0:1] * buf_ref[0:tb, :] + rw[:, 1:2] * buf_ref[tb:2 * tb, :]
    hn = h_ref[...] + ffn
    ho_ref[...] = hn
    if with_norm:
        ms = jnp.mean(hn * hn, axis=-1, keepdims=True)
        xn_ref[...] = (hn * lax.rsqrt(ms + NORM_EPS) * g_ref[...]).astype(xn_ref.dtype)


def combine_call(y, pos, h, route_w, next_gain):
    t, d = h.shape
    tb = NORM_TM // 2
    with_norm = next_gain is not None
    gain = (next_gain if with_norm else jnp.ones((d,), F32)).reshape(1, d)
    out_specs = [pl.BlockSpec((tb, d), lambda i: (i, 0))]
    out_shape = [jax.ShapeDtypeStruct((t, d), F32)]
    if with_norm:
        out_specs.append(pl.BlockSpec((tb, d), lambda i: (i, 0)))
        out_shape.append(jax.ShapeDtypeStruct((t, d), BF16))
    outs = pl.pallas_call(
        functools.partial(_combine_body, with_norm=with_norm),
        grid=(t // tb,),
        in_specs=[pl.BlockSpec((None, 1, MOE_TOPK * tb), lambda i: (i, 0, 0), memory_space=pltpu.SMEM),
                  pl.BlockSpec(memory_space=pl.ANY),
                  pl.BlockSpec((tb, d), lambda i: (i, 0)),
                  pl.BlockSpec((tb, LANES), lambda i: (i, 0)),
                  pl.BlockSpec((1, d), lambda i: (0, 0))],
        out_specs=out_specs,
        out_shape=out_shape,
        scratch_shapes=[pltpu.VMEM((MOE_TOPK * tb, d), F32), pltpu.SemaphoreType.DMA(())],
        compiler_params=_cparams("arbitrary"),
        name="moe_combine",
    )(pos, y, h, route_w, gain)
    return (outs[0], outs[1]) if with_norm else (outs[0], None)


def _dispatch_tables(route_e, t):
    tm = MOE_TM
    n_assign = MOE_TOPK * t
    expert = route_e[:, :MOE_TOPK].reshape(n_assign)
    onehot = (expert[:, None] == jnp.arange(MOE_EXPERTS, dtype=I32)[None, :]).astype(I32)
    rank = jnp.take_along_axis(jnp.cumsum(onehot, axis=0), expert[:, None], axis=1)[:, 0] - 1
    counts = jnp.sum(onehot, axis=0)
    padded = (counts + tm - 1) // tm * tm
    pend = jnp.cumsum(padded)
    pstart = pend - padded
    dest = pstart[expert] + rank
    n_rows = n_assign + MOE_EXPERTS * tm
    token = jnp.arange(n_assign, dtype=I32) // MOE_TOPK
    row_token = jnp.zeros((n_rows,), I32).at[dest].set(token)
    n_blk = n_rows // tm
    blk_expert = jnp.minimum(jnp.searchsorted(pend, jnp.arange(n_blk, dtype=I32) * tm, side='right'),
                             MOE_EXPERTS - 1).astype(I32)
    n_valid = (pend[-1] // tm).astype(I32).reshape(1)
    tb = NORM_TM // 2
    pos = dest.reshape(t // tb, tb, MOE_TOPK).transpose(0, 2, 1).reshape(t // tb, 1, MOE_TOPK * tb)
    return row_token, blk_expert, n_valid, pos.astype(I32)


def moe_layer(h, gain, w_group, b_group, w_expert, b_expert, w_gate, w_up, w_down, next_gain):
    t = h.shape[0]
    xn, route_e, route_w = router_call(h, gain, w_group, b_group, w_expert, b_expert)
    row_token, blk_expert, n_valid, pos = _dispatch_tables(route_e, t)
    xr = dispatch_call(xn, row_token)
    y = ffn_call(xr, blk_expert, n_valid, w_gate.astype(BF16), w_up.astype(BF16), w_down.astype(BF16))
    return combine_call(y, pos, h, route_w, next_gain)


def hybrid_layer(h, xn, w_in, conv_w, conv_b, dt_bias, a_log, d_skip, ssd_norm, ln_g, ln_b, ws, bs, w_out):
    w_main = jnp.concatenate([w_in[:, :OFF_XBC], w_in[:, OFF_U:], w_in[:, OFF_XBC:OFF_DT]],
                             axis=1).astype(BF16)
    hp = SSD_HEADS_PER_GROUP
    w_dt = jnp.zeros((D_MODEL, SSD_GROUPS, LANES), F32).at[:, :, :hp].set(
        w_in[:, OFF_DT:OFF_U].reshape(D_MODEL, SSD_GROUPS, hp)).reshape(D_MODEL, SSD_GROUPS * LANES)
    proj = matmul_call([xn], w_main, out_dtype=BF16, name="hyb_in_proj")
    dt_raw = matmul_call([xn], w_dt.astype(BF16), out_dtype=F32, tn=SSD_GROUPS * LANES, name="hyb_dt_proj")
    dtt_raw = dt_raw.reshape(-1, SSD_GROUPS, LANES)[:, :, :hp].reshape(-1, SSD_HEADS).T
    y_a = ssd_call(proj, dt_raw, dtt_raw, conv_w, conv_b, dt_bias, a_log, d_skip, ssd_norm)
    y_b = gmlp_call(proj, ln_g, ln_b, ws, bs)
    return matmul_call([y_a, y_b], w_out.astype(BF16), res=h, name="hyb_out_proj")


def moba_layer(h, xn, w_qkv, q_norm, k_norm, w_out, cosf, sinf):
    qkv = matmul_call([xn], w_qkv.astype(BF16), out_dtype=BF16, name="att_qkv_proj")
    qt, k, vt, kmean = qk_prep_call(qkv, q_norm, k_norm, cosf, sinf)
    o = attention_call(qt, k, vt, kmean)
    return matmul_call([o], w_out.astype(BF16), res=h, name="att_out_proj")


def _rope_tables(t):
    inv = 1.0 / (ROPE_THETA ** (jnp.arange(0, ATT_HEAD_DIM, 2, dtype=F32) / ATT_HEAD_DIM))
    ang = jnp.arange(t, dtype=F32)[:, None] * inv[None, :]
    cos, sin = jnp.cos(ang), jnp.sin(ang)
    return jnp.concatenate([cos, cos], axis=1), jnp.concatenate([-sin, sin], axis=1)


def kernel(x, norm_mix, norm_ffn, hyb_w_in, ssd_conv_w, ssd_conv_b, ssd_dt_bias, ssd_a_log, ssd_d, ssd_norm, gm_ln_g, gm_ln_b, gm_ws, gm_bs, hyb_w_out, att_w_qkv, att_q_norm, att_k_norm, att_w_out, moe_w_group, moe_b_group, moe_w_expert, moe_b_expert, moe_w_gate, moe_w_up, moe_w_down):
    bsz, t, d = x.shape
    depth = norm_mix.shape[0]
    cosf, sinf = _rope_tables(t)
    outs = []
    for b in range(bsz):
        h = x[b]
        xn = rmsnorm_call(h, norm_mix[0])
        for layer in range(depth):
            j = layer // 2
            if layer % 2 == 0:
                h = hybrid_layer(h, xn, hyb_w_in[j], ssd_conv_w[j], ssd_conv_b[j], ssd_dt_bias[j],
                                 ssd_a_log[j], ssd_d[j], ssd_norm[j], gm_ln_g[j], gm_ln_b[j],
                                 gm_ws[j], gm_bs[j], hyb_w_out[j])
            else:
                h = moba_layer(h, xn, att_w_qkv[j], att_q_norm[j], att_k_norm[j], att_w_out[j], cosf, sinf)
            next_gain = norm_mix[layer + 1] if layer + 1 < depth else None
            h, xn = moe_layer(h, norm_ffn[layer], moe_w_group[layer], moe_b_group[layer],
                              moe_w_expert[layer], moe_b_expert[layer], moe_w_gate[layer],
                              moe_w_up[layer], moe_w_down[layer], next_gain)
        outs.append(h)
    return jnp.stack(outs, axis=0)
```

```python
import functools

import jax
import jax.numpy as jnp
from jax import lax
from jax.experimental import pallas as pl
from jax.experimental.pallas import tpu as pltpu

F32 = jnp.float32
BF16 = jnp.bfloat16
I32 = jnp.int32

D_MODEL = 4096
NORM_EPS = 1e-6

SSD_INNER = 2048
SSD_HEAD_DIM = 64
SSD_HEADS = 32
SSD_GROUPS = 4
SSD_STATE = 128
SSD_CONV = 4
SSD_CHUNK = 256
SSD_BC = SSD_GROUPS * SSD_STATE
SSD_CONV_CH = SSD_INNER + 2 * SSD_BC
SSD_GROUP_W = SSD_INNER // SSD_GROUPS
SSD_HEADS_PER_GROUP = SSD_HEADS // SSD_GROUPS

GM_INNER = 2048
GM_CHUNK = 128
GM_GROUPS = 16
GM_GROUP_DIM = 128

OFF_XBC = SSD_INNER
OFF_DT = OFF_XBC + SSD_CONV_CH
OFF_U = OFF_DT + SSD_HEADS
OFF_V = OFF_U + GM_INNER

ATT_HEAD_DIM = 128
ATT_HEADS = 32
ATT_KV_HEADS = 8
ATT_REP = ATT_HEADS // ATT_KV_HEADS
ATT_Q_W = ATT_HEADS * ATT_HEAD_DIM
ATT_KV_W = ATT_KV_HEADS * ATT_HEAD_DIM
ROPE_THETA = 10000.0
MOBA_BLOCK = 256
MOBA_TOPK = 3

MOE_GROUPS = 4
MOE_EPG = 4
MOE_EXPERTS = 16
MOE_TOPK = 2
MOE_FF = 768

LANES = 128
VMEM_LIMIT_BYTES = 56 * 1024 * 1024

MM_TM = 512
MM_TN = 1024
NORM_TM = 256
ATT_KAUG_W = ATT_HEAD_DIM + LANES
ATT_VT_ROWS = ATT_HEAD_DIM + 16
MASK_BIAS = -1e30
ATT_NB = 2
QK_SCALE_LOG2E = ATT_HEAD_DIM ** -0.5 * 1.4426950408889634
MOE_TM = 256
HIGHEST = lax.Precision.HIGHEST


def _cparams(*sem, flags=None):
    return pltpu.CompilerParams(dimension_semantics=sem, vmem_limit_bytes=VMEM_LIMIT_BYTES, flags=flags)


def _silu(x):
    return x / (1.0 + jnp.exp(-x))


def _softplus(x):
    return jnp.maximum(x, 0.0) + jnp.log1p(jnp.exp(-jnp.abs(x)))


def _gelu_erf(x):
    return 0.5 * x * (1.0 + lax.erf(x * (2.0 ** -0.5)))


def _rmsnorm_body(x_ref, g_ref, o_ref):
    x = x_ref[...]
    ms = jnp.mean(x * x, axis=-1, keepdims=True)
    o_ref[...] = (x * lax.rsqrt(ms + NORM_EPS) * g_ref[...]).astype(o_ref.dtype)


def rmsnorm_call(x, g):
    t, d = x.shape
    return pl.pallas_call(
        _rmsnorm_body,
        grid=(t // NORM_TM,),
        in_specs=[pl.BlockSpec((NORM_TM, d), lambda i: (i, 0)),
                  pl.BlockSpec((1, d), lambda i: (0, 0))],
        out_specs=pl.BlockSpec((NORM_TM, d), lambda i: (i, 0)),
        out_shape=jax.ShapeDtypeStruct((t, d), BF16),
        compiler_params=_cparams("parallel"),
        name="rmsnorm",
    )(x, g.reshape(1, d))


def _matmul_body(*refs, n_parts, has_res):
    a_refs = refs[:n_parts]
    w_refs = refs[n_parts:2 * n_parts]
    o_ref = refs[-1]
    acc = jnp.dot(a_refs[0][...], w_refs[0][...], preferred_element_type=F32)
    for p in range(1, n_parts):
        acc = acc + jnp.dot(a_refs[p][...], w_refs[p][...], preferred_element_type=F32)
    if has_res:
        acc = acc + refs[2 * n_parts][...]
    o_ref[...] = acc.astype(o_ref.dtype)


def matmul_call(a_parts, w, res=None, out_dtype=F32, tn=MM_TN, n_cols=None, name="matmul"):
    n_parts = len(a_parts)
    m, kp = a_parts[0].shape
    n = w.shape[1] if n_cols is None else n_cols
    tm = MM_TM
    in_specs = [pl.BlockSpec((tm, kp), lambda j, i: (i, 0)) for _ in a_parts]
    in_specs += [pl.BlockSpec((kp, tn), functools.partial(lambda j, i, p: (p, j), p=p))
                 for p in range(n_parts)]
    args = list(a_parts) + [w] * n_parts
    if res is not None:
        in_specs.append(pl.BlockSpec((tm, tn), lambda j, i: (i, j)))
        args.append(res)
    return pl.pallas_call(
        functools.partial(_matmul_body, n_parts=n_parts, has_res=res is not None),
        grid=(n // tn, m // tm),
        in_specs=in_specs,
        out_specs=pl.BlockSpec((tm, tn), lambda j, i: (i, j)),
        out_shape=jax.ShapeDtypeStruct((m, n), out_dtype),
        compiler_params=_cparams("parallel", "parallel"),
        name=name,
    )(*args)


def _ssd_body(z_ref, xs_ref, b_ref, c_ref, dt_ref, dtt_ref,
              wx_ref, bx_ref, wb_ref, bb_ref, wc_ref, bc_ref,
              dtb_ref, dtbt_ref, alog_ref, alogt_ref, dskip_ref, ng_ref, expand_ref,
              o_ref, state_ref, tx_ref, tb_ref, tc_ref):
    c = pl.program_id(1)
    L = SSD_CHUNK
    HP = SSD_HEADS_PER_GROUP

    @pl.when(c == 0)
    def _():
        state_ref[...] = jnp.zeros_like(state_ref)
        tx_ref[...] = jnp.zeros_like(tx_ref)
        tb_ref[...] = jnp.zeros_like(tb_ref)
        tc_ref[...] = jnp.zeros_like(tc_ref)

    def conv_silu(cur_ref, tail_ref, w_ref, bias_ref):
        cur = cur_ref[...].astype(F32)
        ext = jnp.concatenate([tail_ref[...], cur], axis=0)
        w = w_ref[...]
        acc = bias_ref[...] + w[SSD_CONV - 1:SSD_CONV, :] * cur
        for j in range(1, SSD_CONV):
            acc = acc + w[SSD_CONV - 1 - j:SSD_CONV - j, :] * ext[8 - j:8 - j + L, :]
        tail_ref[...] = cur[L - 8:L, :]
        return _silu(acc)

    xs = conv_silu(xs_ref, tx_ref, wx_ref, bx_ref)
    bm = conv_silu(b_ref, tb_ref, wb_ref, bb_ref)
    cm = conv_silu(c_ref, tc_ref, wc_ref, bc_ref)

    dt = _softplus(dt_ref[...] + dtb_ref[...])
    da = dt * (-jnp.exp(alog_ref[...]))
    dtt = _softplus(dtt_ref[...] + dtbt_ref[...])
    dat = dtt * (-jnp.exp(alogt_ref[...]))

    row = lax.broadcasted_iota(I32, (L, L), 0)
    col = lax.broadcasted_iota(I32, (L, L), 1)
    causal = row >= col
    tril = causal.astype(F32)
    cs = jnp.dot(tril, da, precision=HIGHEST, preferred_element_type=F32)
    cst = jnp.dot(dat, (row <= col).astype(F32), precision=HIGHEST,
                  preferred_element_type=F32)
    cs_last = cs[L - 1:L, :]
    to_end = jnp.exp(cs_last - cs)
    ecs = jnp.exp(cs)

    expand = expand_ref[...]
    dt_e = jnp.dot(dt, expand, precision=HIGHEST, preferred_element_type=F32)
    to_end_e = jnp.dot(to_end, expand, precision=HIGHEST, preferred_element_type=F32)
    ecs_e = jnp.dot(ecs, expand, precision=HIGHEST, preferred_element_type=F32)

    x = xs * dt_e
    bm16 = bm.astype(BF16)
    cm16 = cm.astype(BF16)
    cb = lax.dot_general(cm16, bm16, (((1,), (1,)), ((), ())), preferred_element_type=F32)

    prev = state_ref[...]
    y = jnp.dot(cm16, prev.astype(BF16), preferred_element_type=F32) * ecs_e

    lane = lax.broadcasted_iota(I32, (L, SSD_GROUP_W), 1)
    for h in range(HP):
        seg = cs[:, h:h + 1] - cst[h:h + 1, :]
        decay = jnp.exp(jnp.where(causal, seg, -jnp.inf))
        m16 = (cb * decay).astype(BF16)
        in_head = (lane >= h * SSD_HEAD_DIM) & (lane < (h + 1) * SSD_HEAD_DIM)
        xh = jnp.where(in_head, x, 0.0).astype(BF16)
        y = y + jnp.dot(m16, xh, preferred_element_type=F32)

    xw = (x * to_end_e).astype(BF16)
    state_ref[...] = prev * ecs_e[L - 1:L, :] + jnp.dot(bm.T.astype(BF16), xw,
                                                        preferred_element_type=F32)

    y = y + xs * dskip_ref[...]
    y = y * _silu(z_ref[...].astype(F32))
    ms = jnp.mean(y * y, axis=-1, keepdims=True)
    o_ref[...] = (y * lax.rsqrt(ms + NORM_EPS) * ng_ref[...]).astype(o_ref.dtype)


def ssd_call(proj, dt_raw, dtt_raw, conv_w, conv_b, dt_bias, a_log, d_skip, norm_g):
    t = proj.shape[0]
    L = SSD_CHUNK
    gw = SSD_GROUP_W
    xs0 = SSD_INNER // gw
    b0 = 2 * SSD_INNER // SSD_STATE
    c0 = b0 + SSD_GROUPS
    hp = SSD_HEADS_PER_GROUP

    dtb = jnp.zeros((SSD_GROUPS, LANES), F32).at[:, :hp].set(dt_bias.reshape(SSD_GROUPS, hp))
    alog = jnp.zeros((SSD_GROUPS, LANES), F32).at[:, :hp].set(a_log.reshape(SSD_GROUPS, hp))
    dtb = dtb.reshape(1, SSD_GROUPS * LANES)
    alog = alog.reshape(1, SSD_GROUPS * LANES)
    dtbt = dt_bias.reshape(SSD_HEADS, 1)
    alogt = a_log.reshape(SSD_HEADS, 1)
    dskip = jnp.repeat(d_skip, SSD_HEAD_DIM).reshape(1, SSD_INNER)
    expand = (jnp.arange(LANES)[:, None] == (jnp.arange(gw)[None, :] // SSD_HEAD_DIM)).astype(F32)
    cw = conv_w.astype(F32)
    cbias = conv_b.reshape(1, SSD_CONV_CH).astype(F32)
    xb0 = SSD_INNER // SSD_STATE
    cb0 = xb0 + SSD_GROUPS

    in_specs = [
        pl.BlockSpec((L, gw), lambda g, c: (c, g)),
        pl.BlockSpec((L, gw), lambda g, c: (c, xs0 + g)),
        pl.BlockSpec((L, SSD_STATE), lambda g, c: (c, b0 + g)),
        pl.BlockSpec((L, SSD_STATE), lambda g, c: (c, c0 + g)),
        pl.BlockSpec((L, LANES), lambda g, c: (c, g)),
        pl.BlockSpec((hp, L), lambda g, c: (g, c)),
        pl.BlockSpec((SSD_CONV, gw), lambda g, c: (0, g)),
        pl.BlockSpec((1, gw), lambda g, c: (0, g)),
        pl.BlockSpec((SSD_CONV, SSD_STATE), lambda g, c: (0, xb0 + g)),
        pl.BlockSpec((1, SSD_STATE), lambda g, c: (0, xb0 + g)),
        pl.BlockSpec((SSD_CONV, SSD_STATE), lambda g, c: (0, cb0 + g)),
        pl.BlockSpec((1, SSD_STATE), lambda g, c: (0, cb0 + g)),
        pl.BlockSpec((1, LANES), lambda g, c: (0, g)),
        pl.BlockSpec((hp, 1), lambda g, c: (g, 0)),
        pl.BlockSpec((1, LANES), lambda g, c: (0, g)),
        pl.BlockSpec((hp, 1), lambda g, c: (g, 0)),
        pl.BlockSpec((1, gw), lambda g, c: (0, g)),
        pl.BlockSpec((1, gw), lambda g, c: (0, g)),
        pl.BlockSpec((LANES, gw), lambda g, c: (0, 0)),
    ]
    return pl.pallas_call(
        _ssd_body,
        grid=(SSD_GROUPS, t // L),
        in_specs=in_specs,
        out_specs=pl.BlockSpec((L, gw), lambda g, c: (c, g)),
        out_shape=jax.ShapeDtypeStruct((t, SSD_INNER), BF16),
        scratch_shapes=[pltpu.VMEM((SSD_STATE, gw), F32),
                        pltpu.VMEM((8, gw), F32),
                        pltpu.VMEM((8, SSD_STATE), F32),
                        pltpu.VMEM((8, SSD_STATE), F32)],
        compiler_params=_cparams("parallel", "arbitrary"),
        name="ssd",
    )(proj, proj, proj, proj, dt_raw, dtt_raw, cw, cbias, cw, cbias, cw, cbias,
      dtb, dtbt, alog, alogt, dskip, norm_g.reshape(1, SSD_INNER), expand)


def _gmlp_body(u_ref, v_ref, g_ref, b_ref, ws_ref, bst_ref, o_ref):
    u = _gelu_erf(u_ref[...].astype(F32))
    v = _gelu_erf(v_ref[...].astype(F32))
    mu = jnp.mean(v, axis=-1, keepdims=True)
    vc = v - mu
    var = jnp.mean(vc * vc, axis=-1, keepdims=True)
    vn = (vc * lax.rsqrt(var + NORM_EPS) * g_ref[...] + b_ref[...]).astype(BF16)
    row = lax.broadcasted_iota(I32, (GM_CHUNK, GM_CHUNK), 0)
    col = lax.broadcasted_iota(I32, (GM_CHUNK, GM_CHUNK), 1)
    causal = row >= col
    for g in range(GM_GROUPS):
        sl = slice(g * GM_GROUP_DIM, (g + 1) * GM_GROUP_DIM)
        w = jnp.where(causal, ws_ref[g], jnp.zeros((), BF16))
        sv = jnp.dot(w, vn[:, sl], preferred_element_type=F32) + bst_ref[:, sl]
        o_ref[:, sl] = (u[:, sl] * sv).astype(o_ref.dtype)


def gmlp_call(proj, ln_g, ln_b, ws, bs):
    t = proj.shape[0]
    ucol = 0
    bst =jnp.repeat(bs.T, GM_GROUP_DIM, axis=1)
    return pl.pallas_call(
        _gmlp_body,
        grid=(t // GM_CHUNK,),
        in_specs=[pl.BlockSpec((GM_CHUNK, GM_INNER), lambda i: (i, ucol)),
                  pl.BlockSpec((GM_CHUNK, GM_INNER), lambda i: (i, ucol + 1)),
                  pl.BlockSpec((1, GM_INNER), lambda i: (0, 0)),
                  pl.BlockSpec((1, GM_INNER), lambda i: (0, 0)),
                  pl.BlockSpec((GM_GROUPS, GM_CHUNK, GM_CHUNK), lambda i: (0, 0, 0)),
                  pl.BlockSpec((GM_CHUNK, GM_INNER), lambda i: (0, 0))],
        out_specs=pl.BlockSpec((GM_CHUNK, GM_INNER), lambda i: (i, 0)),
        out_shape=jax.ShapeDtypeStruct((t, GM_INNER), BF16),
        compiler_params=_cparams("parallel"),
        name="gmlp",
    )(proj, proj, ln_g.reshape(1, GM_INNER), ln_b.reshape(1, GM_INNER), ws.astype(BF16), bst)


def _rope_norm(x, gain, cosf, sinf):
    ms = jnp.mean(x * x, axis=-1, keepdims=True)
    xn = x * lax.rsqrt(ms + NORM_EPS) * gain
    return xn * cosf + pltpu.roll(xn, ATT_HEAD_DIM // 2, axis=1) * sinf


def _qprep_body(x_ref, g_ref, cos_ref, sin_ref, o_ref):
    for r in range(ATT_REP):
        x = x_ref[:, r * ATT_HEAD_DIM:(r + 1) * ATT_HEAD_DIM].astype(F32)
        y = _rope_norm(x, g_ref[...], cos_ref[...], sin_ref[...]) * QK_SCALE_LOG2E
        o_ref[r] = y.T.astype(o_ref.dtype)


def _kprep_body(x_ref, v_ref, g_ref, cos_ref, sin_ref, o_ref, vt_ref, mean_ref):
    y = _rope_norm(x_ref[...].astype(F32), g_ref[...], cos_ref[...], sin_ref[...])
    tb = y.shape[0]
    lane = lax.broadcasted_iota(I32, (tb, LANES), 1)
    o_ref[:, :ATT_HEAD_DIM] = y.astype(o_ref.dtype)
    o_ref[:, ATT_HEAD_DIM:] = jnp.where(lane == pl.program_id(0), 1.0, 0.0).astype(o_ref.dtype)
    mean_ref[...] = jnp.mean(y, axis=0, keepdims=True)
    vt_ref[:ATT_HEAD_DIM, :] = v_ref[...].astype(F32).T.astype(vt_ref.dtype)
    vt_ref[ATT_HEAD_DIM:, :] = jnp.ones((ATT_VT_ROWS - ATT_HEAD_DIM, tb), vt_ref.dtype)


def qk_prep_call(qkv, q_norm, k_norm, cosf, sinf):
    t = qkv.shape[0]
    tb = MOBA_BLOCK
    gw = ATT_REP * ATT_HEAD_DIM
    common = [pl.BlockSpec((1, ATT_HEAD_DIM), lambda i, h: (0, 0)),
              pl.BlockSpec((tb, ATT_HEAD_DIM), lambda i, h: (i, 0)),
              pl.BlockSpec((tb, ATT_HEAD_DIM), lambda i, h: (i, 0))]
    qt = pl.pallas_call(
        _qprep_body,
        grid=(t // tb, ATT_KV_HEADS),
        in_specs=[pl.BlockSpec((tb, gw), lambda i, h: (i, h))] + common,
        out_specs=pl.BlockSpec((ATT_REP, ATT_HEAD_DIM, tb), lambda i, h: (h, 0, i)),
        out_shape=jax.ShapeDtypeStruct((ATT_HEADS, ATT_HEAD_DIM, t), BF16),
        compiler_params=_cparams("parallel", "parallel"),
        name="q_prep",
    )(qkv, q_norm.reshape(1, ATT_HEAD_DIM), cosf, sinf)
    vcol = (ATT_Q_W + ATT_KV_W) // ATT_HEAD_DIM
    k, vt, kmean = pl.pallas_call(
        _kprep_body,
        grid=(t // tb, ATT_KV_HEADS),
        in_specs=[pl.BlockSpec((tb, ATT_HEAD_DIM), lambda i, h: (i, ATT_HEADS + h)),
                  pl.BlockSpec((tb, ATT_HEAD_DIM), lambda i, h: (i, vcol + h))] + common,
        out_specs=[pl.BlockSpec((tb, ATT_KAUG_W), lambda i, h: (i, h)),
                   pl.BlockSpec((None, None, ATT_VT_ROWS, tb), lambda i, h: (h, i, 0, 0)),
                   pl.BlockSpec((None, None, 1, ATT_HEAD_DIM), lambda i, h: (i, h, 0, 0))],
        out_shape=[jax.ShapeDtypeStruct((t, ATT_KV_HEADS * ATT_KAUG_W), BF16),
                   jax.ShapeDtypeStruct((ATT_KV_HEADS, t // tb, ATT_VT_ROWS, tb), BF16),
                   jax.ShapeDtypeStruct((t // tb, ATT_KV_HEADS, 1, ATT_HEAD_DIM), F32)],
        compiler_params=_cparams("parallel", "parallel"),
        name="k_prep",
    )(qkv, qkv, k_norm.reshape(1, ATT_HEAD_DIM), cosf, sinf)
    return qt, k, vt, kmean


def _attn_body(qt_ref, k_ref, vt_ref, km_ref, o_ref, acc_ref, qa_ref, s_ref, *, nb):
    qblk = pl.program_id(1)
    blk = MOBA_BLOCK
    hd = ATT_HEAD_DIM
    n_blocks = k_ref.shape[0] // blk
    nbp = (n_blocks + 7) // 8 * 8
    own0 = pl.multiple_of(qblk * blk, blk)
    km = km_ref[...]
    km_hi = km.astype(BF16)
    km_lo = (km - km_hi.astype(F32)).astype(BF16)
    bid = lax.broadcasted_iota(I32, (nbp, blk), 0)
    causal = lax.broadcasted_iota(I32, (blk, blk), 0) <= lax.broadcasted_iota(I32, (blk, blk), 1)

    m0 = []
    for r in range(ATT_REP):
        qt = qt_ref[r]
        gate = (jnp.dot(km_hi, qt, preferred_element_type=F32)
                + jnp.dot(km_lo, qt, preferred_element_type=F32))[:nbp]
        gate = jnp.where(bid < qblk, gate, -jnp.inf)
        picked = bid < 0
        for kk in range(MOBA_TOPK):
            mx = jnp.max(gate, axis=0, keepdims=True)
            idx = jnp.min(jnp.where(gate == mx, bid, LANES), axis=0, keepdims=True)
            hit = bid == idx
            picked = picked | (hit & (kk < qblk))
            gate = jnp.where(hit, -jnp.inf, gate)
        bias = jnp.where(picked, 0.0, MASK_BIAS).astype(BF16)
        qa_ref[r, :hd, :] = qt
        qa_ref[r, hd:hd + nbp, :] = bias
        if nbp < LANES:
            qa_ref[r, hd + nbp:, :] = jnp.zeros((LANES - nbp, blk), BF16)
        s = jnp.dot(k_ref[pl.ds(own0, blk), :hd], qt, preferred_element_type=F32)
        s = jnp.where(causal, s, -jnp.inf)
        m = jnp.max(s, axis=0, keepdims=True)
        p = jnp.exp2(s - m)
        m0.append(m)
        acc_ref[r] = jnp.dot(vt_ref[qblk], p.astype(BF16), preferred_element_type=F32)

    def blocks_of(jj):
        return [jnp.minimum(jj * nb + u, n_blocks - 1) for u in range(nb)]

    def head_scores(r, blocks):
        return [jnp.dot(k_ref[pl.ds(pl.multiple_of(ja * blk, blk), blk), :], qa_ref[r],
                        preferred_element_type=F32) for ja in blocks]

    for u, su in enumerate(head_scores(0, blocks_of(0))):
        s_ref[u] = su

    def past_blocks(jj, ms):
        blocks = blocks_of(jj)
        vts = [vt_ref[ja] for ja in blocks]
        m_out = []
        pending = [s_ref[u] for u in range(nb)]
        for r in range(ATT_REP):
            scores = pending
            if r + 1 < ATT_REP:
                pending = head_scores(r + 1, blocks)
            else:
                for u, su in enumerate(head_scores(0, blocks_of(jj + 1))):
                    s_ref[u] = su
            m_new = ms[r]
            for su in scores:
                m_new = jnp.maximum(m_new, jnp.max(su, axis=0, keepdims=True))
            alpha = jnp.exp2(ms[r] - m_new)
            pv = None
            for u in range(nb):
                pu = jnp.exp2(scores[u] - m_new).astype(BF16)
                d = jnp.dot(vts[u], pu, preferred_element_type=F32)
                pv = d if pv is None else pv + d
            acc_ref[r] = alpha * acc_ref[r] + pv
            m_out.append(m_new)
        return tuple(m_out)

    lax.fori_loop(0, (qblk + nb - 1) // nb, past_blocks, tuple(m0))

    for r in range(ATT_REP):
        out = acc_ref[r, :hd, :] / acc_ref[r, hd:hd + 1, :]
        o_ref[:, r * hd:(r + 1) * hd] = out.T.astype(o_ref.dtype)


def attention_call(qt, k, vt, kmean):
    t = k.shape[0]
    tq = MOBA_BLOCK
    n_blocks = t // MOBA_BLOCK
    km = jnp.transpose(kmean.reshape(n_blocks, ATT_KV_HEADS, ATT_HEAD_DIM), (1, 0, 2))
    km = jnp.pad(km, ((0, 0), (0, LANES - n_blocks), (0, 0)))
    return pl.pallas_call(
        functools.partial(_attn_body, nb=ATT_NB),
        grid=(ATT_KV_HEADS, t // tq),
        in_specs=[pl.BlockSpec((ATT_REP, ATT_HEAD_DIM, tq), lambda g, i: (g, 0, i)),
                  pl.BlockSpec((t, ATT_KAUG_W), lambda g, i: (0, g)),
                  pl.BlockSpec((None, n_blocks, ATT_VT_ROWS, MOBA_BLOCK), lambda g, i: (g, 0, 0, 0)),
                  pl.BlockSpec((None, LANES, ATT_HEAD_DIM), lambda g, i: (g, 0, 0))],
        out_specs=pl.BlockSpec((tq, ATT_REP * ATT_HEAD_DIM), lambda g, i: (i, g)),
        out_shape=jax.ShapeDtypeStruct((t, ATT_Q_W), BF16),
        scratch_shapes=[pltpu.VMEM((ATT_REP, ATT_VT_ROWS, tq), F32),
                        pltpu.VMEM((ATT_REP, ATT_KAUG_W, tq), BF16),
                        pltpu.VMEM((ATT_NB, MOBA_BLOCK, tq), F32)],
        compiler_params=_cparams("parallel", "parallel"),
        name="moba_attention",
    )(qt, k, vt, km)


def _router_body(h_ref, g_ref, wr_ref, br_ref, xn_ref, re_ref, rw_ref):
    h = h_ref[...]
    ms = jnp.mean(h * h, axis=-1, keepdims=True)
    xn = h * lax.rsqrt(ms + NORM_EPS) * g_ref[...]
    xn_ref[...] = xn
    logits = jnp.dot(xn, wr_ref[...], precision=HIGHEST, preferred_element_type=F32) + br_ref[...]
    lane = lax.broadcasted_iota(I32, logits.shape, 1)
    neg = -jnp.inf

    is_g = lane < MOE_GROUPS
    gl = jnp.where(is_g, logits, neg)
    ge = jnp.exp(gl - jnp.max(gl, axis=1, keepdims=True))
    gp = ge / jnp.sum(ge, axis=1, keepdims=True)
    g_w = jnp.max(gp, axis=1, keepdims=True)
    g_idx = jnp.min(jnp.where(is_g & (gp == g_w), lane, LANES), axis=1, keepdims=True)

    e_lo = MOE_GROUPS + MOE_EPG * g_idx
    is_e = (lane >= e_lo) & (lane < e_lo + MOE_EPG)
    el = jnp.where(is_e, logits, neg)
    ee = jnp.exp(el - jnp.max(el, axis=1, keepdims=True))
    ep = jnp.where(is_e, ee / jnp.sum(ee, axis=1, keepdims=True), -1.0)
    p1 = jnp.max(ep, axis=1, keepdims=True)
    i1 = jnp.min(jnp.where(ep == p1, lane, LANES), axis=1, keepdims=True)
    ep2 = jnp.where(lane == i1, -1.0, ep)
    p2 = jnp.max(ep2, axis=1, keepdims=True)
    i2 = jnp.min(jnp.where(ep2 == p2, lane, LANES), axis=1, keepdims=True)
    den = p1 + p2
    w1 = g_w * (p1 / den)
    w2 = g_w * (p2 / den)
    re_ref[...] = jnp.where(lane == 0, i1 - MOE_GROUPS, jnp.where(lane == 1, i2 - MOE_GROUPS, 0))
    rw_ref[...] = jnp.where(lane == 0, w1, jnp.where(lane == 1, w2, 0.0))


def router_call(h, gain, w_group, b_group, w_expert, b_expert):
    t, d = h.shape
    nr = MOE_GROUPS + MOE_EXPERTS
    wr = jnp.zeros((d, LANES), F32).at[:, :MOE_GROUPS].set(w_group).at[:, MOE_GROUPS:nr].set(w_expert)
    br = jnp.zeros((1, LANES), F32).at[0, :MOE_GROUPS].set(b_group).at[0, MOE_GROUPS:nr].set(b_expert)
    return pl.pallas_call(
        _router_body,
        grid=(t // NORM_TM,),
        in_specs=[pl.BlockSpec((NORM_TM, d), lambda i: (i, 0)),
                  pl.BlockSpec((1, d), lambda i: (0, 0)),
                  pl.BlockSpec((d, LANES), lambda i: (0, 0)),
                  pl.BlockSpec((1, LANES), lambda i: (0, 0))],
        out_specs=[pl.BlockSpec((NORM_TM, d), lambda i: (i, 0)),
                   pl.BlockSpec((NORM_TM, LANES), lambda i: (i, 0)),
                   pl.BlockSpec((NORM_TM, LANES), lambda i: (i, 0))],
        out_shape=[jax.ShapeDtypeStruct((t, d), F32),
                   jax.ShapeDtypeStruct((t, LANES), I32),
                   jax.ShapeDtypeStruct((t, LANES), F32)],
        compiler_params=_cparams("parallel"),
        name="norm_router",
    )(h, gain.reshape(1, d), wr, br)


def _row_copy(src_ref, dst_ref, sem, src_row, dst_row):
    return pltpu.make_async_copy(src_ref.at[pl.ds(src_row, 1), :], dst_ref.at[pl.ds(dst_row, 1), :], sem)


def _start_row_gather(idx_ref, src_ref, dst_ref, sem, n):
    for r in range(n):
        _row_copy(src_ref, dst_ref, sem, idx_ref[0, r], r).start(priority=r % 2)


def _wait_row_gather(src_ref, dst_ref, sem, n):
    for r in range(n):
        _row_copy(src_ref, dst_ref, sem, 0, r).wait()


def _ffn_body(be_ref, nv_ref, tok0_ref, tokn_ref, x_hbm, wg_ref, wu_ref, wd_ref, o_ref, xbuf, sems):
    b = pl.program_id(0)
    nv = nv_ref[0]
    last = pl.num_programs(0) - 1
    tm = o_ref.shape[0]
    slot = b % 2
    valid = b < nv

    @pl.when(b == 0)
    def _():
        _start_row_gather(tok0_ref, x_hbm, xbuf.at[0], sems.at[0], tm)

    @pl.when(b <= nv)
    def _():
        _wait_row_gather(x_hbm, xbuf.at[slot], sems.at[slot], tm)

    @pl.when(valid)
    def _():
        _start_row_gather(tokn_ref, x_hbm, xbuf.at[1 - slot], sems.at[1 - slot], tm)
        x = xbuf[slot].astype(BF16)
        gate = jnp.dot(x, wg_ref[...], preferred_element_type=F32)
        up = jnp.dot(x, wu_ref[...], preferred_element_type=F32)
        hid = (_silu(gate) * up).astype(BF16)
        o_ref[...] = jnp.dot(hid, wd_ref[...], preferred_element_type=F32).astype(o_ref.dtype)

    @pl.when(jnp.logical_not(valid))
    def _():
        o_ref[...] = jnp.zeros_like(o_ref)

    @pl.when(valid & (b == last))
    def _():
        _wait_row_gather(x_hbm, xbuf.at[1 - slot], sems.at[1 - slot], tm)


def ffn_call(xn, row_token, blk_expert, n_valid, w_gate, w_up, w_down):
    t, d = xn.shape
    ff = w_gate.shape[2]
    tm = MOE_TM
    n_rows = row_token.shape[0]
    n_blk = n_rows // tm

    def w_map(b, be, nv):
        return (be[jnp.minimum(b, nv[0] - 1)], 0, 0)

    tok_spec = functools.partial(pl.BlockSpec, (None, 1, tm), memory_space=pltpu.SMEM)
    grid_spec = pltpu.PrefetchScalarGridSpec(
        num_scalar_prefetch=2,
        grid=(n_blk,),
        in_specs=[tok_spec(index_map=lambda b, be, nv: (0, 0, 0)),
                  tok_spec(index_map=lambda b, be, nv: (jnp.minimum(b + 1, n_blk - 1), 0, 0)),
                  pl.BlockSpec(memory_space=pl.ANY),
                  pl.BlockSpec((None, d, ff), w_map),
                  pl.BlockSpec((None, d, ff), w_map),
                  pl.BlockSpec((None, ff, d), w_map)],
        out_specs=pl.BlockSpec((tm, d), lambda b, be, nv: (b, 0)),
        scratch_shapes=[pltpu.VMEM((2, tm, d), xn.dtype), pltpu.SemaphoreType.DMA((2,))],
    )
    tok = row_token.reshape(n_blk, 1, tm)
    return pl.pallas_call(
        _ffn_body,
        grid_spec=grid_spec,
        out_shape=jax.ShapeDtypeStruct((n_rows, d), F32),
        compiler_params=_cparams("arbitrary"),
        name="moe_ffn",
    )(blk_expert, n_valid, tok, tok, xn, w_gate, w_up, w_down)


def _combine_body(pos_ref, posn_ref, y_ref, h_ref, rw_ref, g_ref, *out_refs_and_scratch, with_norm):
    if with_norm:
        ho_ref, xn_ref, buf_ref, sems = out_refs_and_scratch
    else:
        ho_ref, buf_ref, sems = out_refs_and_scratch
    tb = ho_ref.shape[0]
    n = MOE_TOPK * tb
    i = pl.program_id(0)
    slot = i % 2

    @pl.when(i == 0)
    def _():
        _start_row_gather(pos_ref, y_ref, buf_ref.at[0], sems.at[0], n)

    _wait_row_gather(y_ref, buf_ref.at[slot], sems.at[slot], n)

    @pl.when(i + 1 < pl.num_programs(0))
    def _():
        _start_row_gather(posn_ref, y_ref, buf_ref.at[1 - slot], sems.at[1 - slot], n)

    rw = rw_ref[...]
    ffn = rw[:, 0:1] * buf_ref[slot, 0:tb, :] + rw[:, 1:2] * buf_ref[slot, tb:2 * tb, :]
    hn = h_ref[...] + ffn
    ho_ref[...] = hn
    if with_norm:
        ms = jnp.mean(hn * hn, axis=-1, keepdims=True)
        xn_ref[...] = (hn * lax.rsqrt(ms + NORM_EPS) * g_ref[...]).astype(xn_ref.dtype)


def combine_call(y, pos, h, route_w, next_gain):
    t, d = h.shape
    tb = NORM_TM // 2
    n_steps = t // tb
    with_norm = next_gain is not None
    gain = (next_gain if with_norm else jnp.ones((d,), F32)).reshape(1, d)
    out_specs = [pl.BlockSpec((tb, d), lambda i: (i, 0))]
    out_shape = [jax.ShapeDtypeStruct((t, d), F32)]
    if with_norm:
        out_specs.append(pl.BlockSpec((tb, d), lambda i: (i, 0)))
        out_shape.append(jax.ShapeDtypeStruct((t, d), BF16))
    pos_spec = functools.partial(pl.BlockSpec, (None, 1, MOE_TOPK * tb), memory_space=pltpu.SMEM)
    outs = pl.pallas_call(
        functools.partial(_combine_body, with_norm=with_norm),
        grid=(n_steps,),
        in_specs=[pos_spec(index_map=lambda i: (i, 0, 0)),
                  pos_spec(index_map=lambda i: (jnp.minimum(i + 1, n_steps - 1), 0, 0)),
                  pl.BlockSpec(memory_space=pl.ANY),
                  pl.BlockSpec((tb, d), lambda i: (i, 0)),
                  pl.BlockSpec((tb, LANES), lambda i: (i, 0)),
                  pl.BlockSpec((1, d), lambda i: (0, 0))],
        out_specs=out_specs,
        out_shape=out_shape,
        scratch_shapes=[pltpu.VMEM((2, MOE_TOPK * tb, d), F32), pltpu.SemaphoreType.DMA((2,))],
        compiler_params=_cparams("arbitrary"),
        name="moe_combine",
    )(pos, pos, y, h, route_w, gain)
    return (outs[0], outs[1]) if with_norm else (outs[0], None)


def _dispatch_tables(route_e, t):
    tm = MOE_TM
    n_assign = MOE_TOPK * t
    expert = route_e[:, :MOE_TOPK].reshape(n_assign)
    onehot = (expert[:, None] == jnp.arange(MOE_EXPERTS, dtype=I32)[None, :]).astype(I32)
    rank = jnp.take_along_axis(jnp.cumsum(onehot, axis=0), expert[:, None], axis=1)[:, 0] - 1
    counts = jnp.sum(onehot, axis=0)
    padded = (counts + tm - 1) // tm * tm
    pend = jnp.cumsum(padded)
    pstart = pend - padded
    dest = pstart[expert] + rank
    n_rows = n_assign + MOE_EXPERTS * tm
    token = jnp.arange(n_assign, dtype=I32) // MOE_TOPK
    row_token = jnp.zeros((n_rows,), I32).at[dest].set(token)
    n_blk = n_rows // tm
    blk_expert = jnp.minimum(jnp.searchsorted(pend, jnp.arange(n_blk, dtype=I32) * tm, side='right'),
                             MOE_EXPERTS - 1).astype(I32)
    n_valid = (pend[-1] // tm).astype(I32).reshape(1)
    tb = NORM_TM // 2
    pos = dest.reshape(t // tb, tb, MOE_TOPK).transpose(0, 2, 1).reshape(t // tb, 1, MOE_TOPK * tb)
    return row_token, blk_expert, n_valid, pos.astype(I32)


def moe_layer(h, gain, w_group, b_group, w_expert, b_expert, w_gate, w_up, w_down, next_gain):
    t = h.shape[0]
    xn, route_e, route_w = router_call(h, gain, w_group, b_group, w_expert, b_expert)
    row_token, blk_expert, n_valid, pos = _dispatch_tables(route_e, t)
    y = ffn_call(xn, row_token, blk_expert, n_valid,
                 w_gate.astype(BF16), w_up.astype(BF16), w_down.astype(BF16))
    return combine_call(y, pos, h, route_w, next_gain)


def hybrid_layer(h, xn, w_in, conv_w, conv_b, dt_bias, a_log, d_skip, ssd_norm, ln_g, ln_b, ws, bs, w_out):
    w16 = w_in.astype(BF16)
    hp = SSD_HEADS_PER_GROUP
    w_dt = jnp.zeros((D_MODEL, SSD_GROUPS, LANES), BF16).at[:, :, :hp].set(
        w16[:, OFF_DT:OFF_U].reshape(D_MODEL, SSD_GROUPS, hp)).reshape(D_MODEL, SSD_GROUPS * LANES)
    proj_a = matmul_call([xn], w16, out_dtype=BF16, n_cols=OFF_DT, name="hyb_in_proj_ssd")
    proj_b = matmul_call([xn], w16[:, OFF_U:], out_dtype=BF16, name="hyb_in_proj_gmlp")
    dt_raw = matmul_call([xn], w_dt, out_dtype=F32, tn=SSD_GROUPS * LANES, name="hyb_dt_proj")
    dtt_raw = dt_raw.reshape(-1, SSD_GROUPS, LANES)[:, :, :hp].reshape(-1, SSD_HEADS).T
    y_a = ssd_call(proj_a, dt_raw, dtt_raw, conv_w, conv_b, dt_bias, a_log, d_skip, ssd_norm)
    y_b = gmlp_call(proj_b, ln_g, ln_b, ws, bs)
    return matmul_call([y_a, y_b], w_out.astype(BF16), res=h, name="hyb_out_proj")


def moba_layer(h, xn, w_qkv, q_norm, k_norm, w_out, cosf, sinf):
    qkv = matmul_call([xn], w_qkv.astype(BF16), out_dtype=BF16, name="att_qkv_proj")
    qt, k, vt, kmean = qk_prep_call(qkv, q_norm, k_norm, cosf, sinf)
    o = attention_call(qt, k, vt, kmean)
    return matmul_call([o], w_out.astype(BF16), res=h, name="att_out_proj")


def _rope_tables(t):
    inv = 1.0 / (ROPE_THETA ** (jnp.arange(0, ATT_HEAD_DIM, 2, dtype=F32) / ATT_HEAD_DIM))
    ang = jnp.arange(t, dtype=F32)[:, None] * inv[None, :]
    cos, sin = jnp.cos(ang), jnp.sin(ang)
    return jnp.concatenate([cos, cos], axis=1), jnp.concatenate([-sin, sin], axis=1)


def kernel(x, norm_mix, norm_ffn, hyb_w_in, ssd_conv_w, ssd_conv_b, ssd_dt_bias, ssd_a_log, ssd_d, ssd_norm, gm_ln_g, gm_ln_b, gm_ws, gm_bs, hyb_w_out, att_w_qkv, att_q_norm, att_k_norm, att_w_out, moe_w_group, moe_b_group, moe_w_expert, moe_b_expert, moe_w_gate, moe_w_up, moe_w_down):
    bsz, t, d = x.shape
    depth = norm_mix.shape[0]
    cosf, sinf = _rope_tables(t)
    outs = []
    for b in range(bsz):
        h = x[b]
        xn = rmsnorm_call(h, norm_mix[0])
        for layer in range(depth):
            j = layer // 2
            if layer % 2 == 0:
                h = hybrid_layer(h, xn, hyb_w_in[j], ssd_conv_w[j], ssd_conv_b[j], ssd_dt_bias[j],
                                 ssd_a_log[j], ssd_d[j], ssd_norm[j], gm_ln_g[j], gm_ln_b[j],
                                 gm_ws[j], gm_bs[j], hyb_w_out[j])
            else:
                h = moba_layer(h, xn, att_w_qkv[j], att_q_norm[j], att_k_norm[j], att_w_out[j], cosf, sinf)
            next_gain = norm_mix[layer + 1] if layer + 1 < depth else None
            h, xn = moe_layer(h, norm_ffn[layer], moe_w_group[layer], moe_b_group[layer],
                              moe_w_expert[layer], moe_b_expert[layer], moe_w_gate[layer],
                              moe_w_up[layer], moe_w_down[layer], next_gain)
        outs.append(h)
    return jnp.stack(outs, axis=0)
```

```python
import functools

import jax
import jax.numpy as jnp
from jax import lax
from jax.experimental import pallas as pl
from jax.experimental.pallas import tpu as pltpu

F32 = jnp.float32
BF16 = jnp.bfloat16
I32 = jnp.int32

D_MODEL = 4096
NORM_EPS = 1e-6

SSD_INNER = 2048
SSD_HEAD_DIM = 64
SSD_HEADS = 32
SSD_GROUPS = 4
SSD_STATE = 128
SSD_CONV = 4
SSD_CHUNK = 256
SSD_BC = SSD_GROUPS * SSD_STATE
SSD_CONV_CH = SSD_INNER + 2 * SSD_BC
SSD_GROUP_W = SSD_INNER // SSD_GROUPS
SSD_HEADS_PER_GROUP = SSD_HEADS // SSD_GROUPS

GM_INNER = 2048
GM_CHUNK = 128
GM_GROUPS = 16
GM_GROUP_DIM = 128

OFF_XBC = SSD_INNER
OFF_DT = OFF_XBC + SSD_CONV_CH
OFF_U = OFF_DT + SSD_HEADS
OFF_V = OFF_U + GM_INNER

ATT_HEAD_DIM = 128
ATT_HEADS = 32
ATT_KV_HEADS = 8
ATT_REP = ATT_HEADS // ATT_KV_HEADS
ATT_Q_W = ATT_HEADS * ATT_HEAD_DIM
ATT_KV_W = ATT_KV_HEADS * ATT_HEAD_DIM
ROPE_THETA = 10000.0
MOBA_BLOCK = 256
MOBA_TOPK = 3

MOE_GROUPS = 4
MOE_EPG = 4
MOE_EXPERTS = 16
MOE_TOPK = 2
MOE_FF = 768

LANES = 128
VMEM_LIMIT_BYTES = 56 * 1024 * 1024

MM_TM = 512
MM_TN = 1024
NORM_TM = 256
ATT_KAUG_W = ATT_HEAD_DIM + LANES
ATT_VT_ROWS = ATT_HEAD_DIM + 16
MASK_BIAS = -1e30
ATT_NB = 2
QK_SCALE_LOG2E = ATT_HEAD_DIM ** -0.5 * 1.4426950408889634
MOE_TM = 256
HIGHEST = lax.Precision.HIGHEST


def _cparams(*sem, flags=None):
    return pltpu.CompilerParams(dimension_semantics=sem, vmem_limit_bytes=VMEM_LIMIT_BYTES, flags=flags)


def _silu(x):
    return x / (1.0 + jnp.exp(-x))


def _softplus(x):
    return jnp.maximum(x, 0.0) + jnp.log1p(jnp.exp(-jnp.abs(x)))


def _gelu_erf(x):
    return 0.5 * x * (1.0 + lax.erf(x * (2.0 ** -0.5)))


def _rmsnorm_body(x_ref, g_ref, o_ref):
    x = x_ref[...]
    ms = jnp.mean(x * x, axis=-1, keepdims=True)
    o_ref[...] = (x * lax.rsqrt(ms + NORM_EPS) * g_ref[...]).astype(o_ref.dtype)


def rmsnorm_call(x, g):
    t, d = x.shape
    return pl.pallas_call(
        _rmsnorm_body,
        grid=(t // NORM_TM,),
        in_specs=[pl.BlockSpec((NORM_TM, d), lambda i: (i, 0)),
                  pl.BlockSpec((1, d), lambda i: (0, 0))],
        out_specs=pl.BlockSpec((NORM_TM, d), lambda i: (i, 0)),
        out_shape=jax.ShapeDtypeStruct((t, d), BF16),
        compiler_params=_cparams("parallel"),
        name="rmsnorm",
    )(x, g.reshape(1, d))


def _matmul_body(*refs, n_parts, has_res):
    a_refs = refs[:n_parts]
    w_refs = refs[n_parts:2 * n_parts]
    o_ref = refs[-1]
    acc = jnp.dot(a_refs[0][...], w_refs[0][...], preferred_element_type=F32)
    for p in range(1, n_parts):
        acc = acc + jnp.dot(a_refs[p][...], w_refs[p][...], preferred_element_type=F32)
    if has_res:
        acc = acc + refs[2 * n_parts][...]
    o_ref[...] = acc.astype(o_ref.dtype)


def matmul_call(a_parts, w, res=None, out_dtype=F32, tn=MM_TN, n_cols=None, layer=None, name="matmul"):
    n_parts = len(a_parts)
    m, kp = a_parts[0].shape
    n = w.shape[-1] if n_cols is None else n_cols
    tm = MM_TM
    in_specs = [pl.BlockSpec((tm, kp), lambda j, i: (i, 0)) for _ in a_parts]
    if layer is None:
        in_specs += [pl.BlockSpec((kp, tn), functools.partial(lambda j, i, p: (p, j), p=p))
                     for p in range(n_parts)]
    else:
        in_specs += [pl.BlockSpec((None, kp, tn), functools.partial(lambda j, i, p: (layer, p, j), p=p))
                     for p in range(n_parts)]
    args = list(a_parts) + [w] * n_parts
    if res is not None:
        in_specs.append(pl.BlockSpec((tm, tn), lambda j, i: (i, j)))
        args.append(res)
    return pl.pallas_call(
        functools.partial(_matmul_body, n_parts=n_parts, has_res=res is not None),
        grid=(n // tn, m // tm),
        in_specs=in_specs,
        out_specs=pl.BlockSpec((tm, tn), lambda j, i: (i, j)),
        out_shape=jax.ShapeDtypeStruct((m, n), out_dtype),
        compiler_params=_cparams("parallel", "parallel"),
        name=name,
    )(*args)


def _ssd_body(z_ref, xs_ref, b_ref, c_ref, dt_ref, dtt_ref,
              wx_ref, bx_ref, wb_ref, bb_ref, wc_ref, bc_ref,
              dtb_ref, dtbt_ref, alog_ref, alogt_ref, dskip_ref, ng_ref, expand_ref,
              o_ref, state_ref, tx_ref, tb_ref, tc_ref):
    c = pl.program_id(1)
    L = SSD_CHUNK
    HP = SSD_HEADS_PER_GROUP

    @pl.when(c == 0)
    def _():
        state_ref[...] = jnp.zeros_like(state_ref)
        tx_ref[...] = jnp.zeros_like(tx_ref)
        tb_ref[...] = jnp.zeros_like(tb_ref)
        tc_ref[...] = jnp.zeros_like(tc_ref)

    def conv_silu(cur_ref, tail_ref, w_ref, bias_ref):
        cur = cur_ref[...].astype(F32)
        ext = jnp.concatenate([tail_ref[...], cur], axis=0)
        w = w_ref[...]
        acc = bias_ref[...] + w[SSD_CONV - 1:SSD_CONV, :] * cur
        for j in range(1, SSD_CONV):
            acc = acc + w[SSD_CONV - 1 - j:SSD_CONV - j, :] * ext[8 - j:8 - j + L, :]
        tail_ref[...] = cur[L - 8:L, :]
        return _silu(acc)

    xs = conv_silu(xs_ref, tx_ref, wx_ref, bx_ref)
    bm = conv_silu(b_ref, tb_ref, wb_ref, bb_ref)
    cm = conv_silu(c_ref, tc_ref, wc_ref, bc_ref)

    dt = _softplus(dt_ref[...] + dtb_ref[...])
    da = dt * (-jnp.exp(alog_ref[...]))
    dtt = _softplus(dtt_ref[...] + dtbt_ref[...])
    dat = dtt * (-jnp.exp(alogt_ref[...]))

    row = lax.broadcasted_iota(I32, (L, L), 0)
    col = lax.broadcasted_iota(I32, (L, L), 1)
    causal = row >= col
    tril = causal.astype(F32)
    cs = jnp.dot(tril, da, precision=HIGHEST, preferred_element_type=F32)
    cst = jnp.dot(dat, (row <= col).astype(F32), precision=HIGHEST,
                  preferred_element_type=F32)
    cs_last = cs[L - 1:L, :]
    to_end = jnp.exp(cs_last - cs)
    ecs = jnp.exp(cs)

    expand = expand_ref[...]
    dt_e = jnp.dot(dt, expand, precision=HIGHEST, preferred_element_type=F32)
    to_end_e = jnp.dot(to_end, expand, precision=HIGHEST, preferred_element_type=F32)
    ecs_e = jnp.dot(ecs, expand, precision=HIGHEST, preferred_element_type=F32)

    x = xs * dt_e
    bm16 = bm.astype(BF16)
    cm16 = cm.astype(BF16)
    cb = lax.dot_general(cm16, bm16, (((1,), (1,)), ((), ())), preferred_element_type=F32)

    prev = state_ref[...]
    y = jnp.dot(cm16, prev.astype(BF16), preferred_element_type=F32) * ecs_e

    lane = lax.broadcasted_iota(I32, (L, SSD_GROUP_W), 1)
    for h in range(HP):
        seg = cs[:, h:h + 1] - cst[h:h + 1, :]
        decay = jnp.exp(jnp.where(causal, seg, -jnp.inf))
        m16 = (cb * decay).astype(BF16)
        in_head = (lane >= h * SSD_HEAD_DIM) & (lane < (h + 1) * SSD_HEAD_DIM)
        xh = jnp.where(in_head, x, 0.0).astype(BF16)
        y = y + jnp.dot(m16, xh, preferred_element_type=F32)

    xw = (x * to_end_e).astype(BF16)
    state_ref[...] = prev * ecs_e[L - 1:L, :] + jnp.dot(bm.T.astype(BF16), xw,
                                                        preferred_element_type=F32)

    y = y + xs * dskip_ref[...]
    y = y * _silu(z_ref[...].astype(F32))
    ms = jnp.mean(y * y, axis=-1, keepdims=True)
    o_ref[...] = (y * lax.rsqrt(ms + NORM_EPS) * ng_ref[...]).astype(o_ref.dtype)


def ssd_call(proj, dt_raw, dtt_raw, conv_w, conv_b, dt_bias, a_log, d_skip, norm_g):
    t = proj.shape[0]
    L = SSD_CHUNK
    gw = SSD_GROUP_W
    xs0 = SSD_INNER // gw
    b0 = 2 * SSD_INNER // SSD_STATE
    c0 = b0 + SSD_GROUPS
    hp = SSD_HEADS_PER_GROUP

    dtb = jnp.zeros((SSD_GROUPS, LANES), F32).at[:, :hp].set(dt_bias.reshape(SSD_GROUPS, hp))
    alog = jnp.zeros((SSD_GROUPS, LANES), F32).at[:, :hp].set(a_log.reshape(SSD_GROUPS, hp))
    dtb = dtb.reshape(1, SSD_GROUPS * LANES)
    alog = alog.reshape(1, SSD_GROUPS * LANES)
    dtbt = dt_bias.reshape(SSD_HEADS, 1)
    alogt = a_log.reshape(SSD_HEADS, 1)
    dskip = jnp.repeat(d_skip, SSD_HEAD_DIM).reshape(1, SSD_INNER)
    expand = (jnp.arange(LANES)[:, None] == (jnp.arange(gw)[None, :] // SSD_HEAD_DIM)).astype(F32)
    cw = conv_w.astype(F32)
    cbias = conv_b.reshape(1, SSD_CONV_CH).astype(F32)
    xb0 = SSD_INNER // SSD_STATE
    cb0 = xb0 + SSD_GROUPS

    in_specs = [
        pl.BlockSpec((L, gw), lambda g, c: (c, g)),
        pl.BlockSpec((L, gw), lambda g, c: (c, xs0 + g)),
        pl.BlockSpec((L, SSD_STATE), lambda g, c: (c, b0 + g)),
        pl.BlockSpec((L, SSD_STATE), lambda g, c: (c, c0 + g)),
        pl.BlockSpec((L, LANES), lambda g, c: (c, g)),
        pl.BlockSpec((hp, L), lambda g, c: (g, c)),
        pl.BlockSpec((SSD_CONV, gw), lambda g, c: (0, g)),
        pl.BlockSpec((1, gw), lambda g, c: (0, g)),
        pl.BlockSpec((SSD_CONV, SSD_STATE), lambda g, c: (0, xb0 + g)),
        pl.BlockSpec((1, SSD_STATE), lambda g, c: (0, xb0 + g)),
        pl.BlockSpec((SSD_CONV, SSD_STATE), lambda g, c: (0, cb0 + g)),
        pl.BlockSpec((1, SSD_STATE), lambda g, c: (0, cb0 + g)),
        pl.BlockSpec((1, LANES), lambda g, c: (0, g)),
        pl.BlockSpec((hp, 1), lambda g, c: (g, 0)),
        pl.BlockSpec((1, LANES), lambda g, c: (0, g)),
        pl.BlockSpec((hp, 1), lambda g, c: (g, 0)),
        pl.BlockSpec((1, gw), lambda g, c: (0, g)),
        pl.BlockSpec((1, gw), lambda g, c: (0, g)),
        pl.BlockSpec((LANES, gw), lambda g, c: (0, 0)),
    ]
    return pl.pallas_call(
        _ssd_body,
        grid=(SSD_GROUPS, t // L),
        in_specs=in_specs,
        out_specs=pl.BlockSpec((L, gw), lambda g, c: (c, g)),
        out_shape=jax.ShapeDtypeStruct((t, SSD_INNER), BF16),
        scratch_shapes=[pltpu.VMEM((SSD_STATE, gw), F32),
                        pltpu.VMEM((8, gw), F32),
                        pltpu.VMEM((8, SSD_STATE), F32),
                        pltpu.VMEM((8, SSD_STATE), F32)],
        compiler_params=_cparams("parallel", "arbitrary"),
        name="ssd",
    )(proj, proj, proj, proj, dt_raw, dtt_raw, cw, cbias, cw, cbias, cw, cbias,
      dtb, dtbt, alog, alogt, dskip, norm_g.reshape(1, SSD_INNER), expand)


def _gmlp_body(u_ref, v_ref, g_ref, b_ref, ws_ref, bst_ref, o_ref):
    u = _gelu_erf(u_ref[...].astype(F32))
    v = _gelu_erf(v_ref[...].astype(F32))
    mu = jnp.mean(v, axis=-1, keepdims=True)
    vc = v - mu
    var = jnp.mean(vc * vc, axis=-1, keepdims=True)
    vn = (vc * lax.rsqrt(var + NORM_EPS) * g_ref[...] + b_ref[...]).astype(BF16)
    row = lax.broadcasted_iota(I32, (GM_CHUNK, GM_CHUNK), 0)
    col = lax.broadcasted_iota(I32, (GM_CHUNK, GM_CHUNK), 1)
    causal = row >= col
    for g in range(GM_GROUPS):
        sl = slice(g * GM_GROUP_DIM, (g + 1) * GM_GROUP_DIM)
        w = jnp.where(causal, ws_ref[g], jnp.zeros((), BF16))
        sv = jnp.dot(w, vn[:, sl], preferred_element_type=F32) + bst_ref[:, sl]
        o_ref[:, sl] = (u[:, sl] * sv).astype(o_ref.dtype)


def gmlp_call(proj, ln_g, ln_b, ws, bs):
    t = proj.shape[0]
    ucol = 0
    bst =jnp.repeat(bs.T, GM_GROUP_DIM, axis=1)
    return pl.pallas_call(
        _gmlp_body,
        grid=(t // GM_CHUNK,),
        in_specs=[pl.BlockSpec((GM_CHUNK, GM_INNER), lambda i: (i, ucol)),
                  pl.BlockSpec((GM_CHUNK, GM_INNER), lambda i: (i, ucol + 1)),
                  pl.BlockSpec((1, GM_INNER), lambda i: (0, 0)),
                  pl.BlockSpec((1, GM_INNER), lambda i: (0, 0)),
                  pl.BlockSpec((GM_GROUPS, GM_CHUNK, GM_CHUNK), lambda i: (0, 0, 0)),
                  pl.BlockSpec((GM_CHUNK, GM_INNER), lambda i: (0, 0))],
        out_specs=pl.BlockSpec((GM_CHUNK, GM_INNER), lambda i: (i, 0)),
        out_shape=jax.ShapeDtypeStruct((t, GM_INNER), BF16),
        compiler_params=_cparams("parallel"),
        name="gmlp",
    )(proj, proj, ln_g.reshape(1, GM_INNER), ln_b.reshape(1, GM_INNER), ws.astype(BF16), bst)


def _rope_norm(x, gain, cosf, sinf):
    ms = jnp.mean(x * x, axis=-1, keepdims=True)
    xn = x * lax.rsqrt(ms + NORM_EPS) * gain
    return xn * cosf + pltpu.roll(xn, ATT_HEAD_DIM // 2, axis=1) * sinf


def _qprep_body(x_ref, g_ref, cos_ref, sin_ref, o_ref):
    for r in range(ATT_REP):
        x = x_ref[:, r * ATT_HEAD_DIM:(r + 1) * ATT_HEAD_DIM].astype(F32)
        y = _rope_norm(x, g_ref[...], cos_ref[...], sin_ref[...]) * QK_SCALE_LOG2E
        o_ref[r] = y.T.astype(o_ref.dtype)


def _kprep_body(x_ref, v_ref, g_ref, cos_ref, sin_ref, o_ref, vt_ref, mean_ref):
    y = _rope_norm(x_ref[...].astype(F32), g_ref[...], cos_ref[...], sin_ref[...])
    tb = y.shape[0]
    lane = lax.broadcasted_iota(I32, (tb, LANES), 1)
    o_ref[:, :ATT_HEAD_DIM] = y.astype(o_ref.dtype)
    o_ref[:, ATT_HEAD_DIM:] = jnp.where(lane == pl.program_id(0), 1.0, 0.0).astype(o_ref.dtype)
    mean_ref[...] = jnp.mean(y, axis=0, keepdims=True)
    vt_ref[:ATT_HEAD_DIM, :] = v_ref[...].astype(F32).T.astype(vt_ref.dtype)
    vt_ref[ATT_HEAD_DIM:, :] = jnp.ones((ATT_VT_ROWS - ATT_HEAD_DIM, tb), vt_ref.dtype)


def qk_prep_call(qkv, q_norm, k_norm, cosf, sinf):
    t = qkv.shape[0]
    tb = MOBA_BLOCK
    gw = ATT_REP * ATT_HEAD_DIM
    common = [pl.BlockSpec((1, ATT_HEAD_DIM), lambda i, h: (0, 0)),
              pl.BlockSpec((tb, ATT_HEAD_DIM), lambda i, h: (i, 0)),
              pl.BlockSpec((tb, ATT_HEAD_DIM), lambda i, h: (i, 0))]
    qt = pl.pallas_call(
        _qprep_body,
        grid=(t // tb, ATT_KV_HEADS),
        in_specs=[pl.BlockSpec((tb, gw), lambda i, h: (i, h))] + common,
        out_specs=pl.BlockSpec((ATT_REP, ATT_HEAD_DIM, tb), lambda i, h: (h, 0, i)),
        out_shape=jax.ShapeDtypeStruct((ATT_HEADS, ATT_HEAD_DIM, t), BF16),
        compiler_params=_cparams("parallel", "parallel"),
        name="q_prep",
    )(qkv, q_norm.reshape(1, ATT_HEAD_DIM), cosf, sinf)
    vcol = (ATT_Q_W + ATT_KV_W) // ATT_HEAD_DIM
    k, vt, kmean = pl.pallas_call(
        _kprep_body,
        grid=(t // tb, ATT_KV_HEADS),
        in_specs=[pl.BlockSpec((tb, ATT_HEAD_DIM), lambda i, h: (i, ATT_HEADS + h)),
                  pl.BlockSpec((tb, ATT_HEAD_DIM), lambda i, h: (i, vcol + h))] + common,
        out_specs=[pl.BlockSpec((tb, ATT_KAUG_W), lambda i, h: (i, h)),
                   pl.BlockSpec((None, None, ATT_VT_ROWS, tb), lambda i, h: (h, i, 0, 0)),
                   pl.BlockSpec((None, None, 1, ATT_HEAD_DIM), lambda i, h: (i, h, 0, 0))],
        out_shape=[jax.ShapeDtypeStruct((t, ATT_KV_HEADS * ATT_KAUG_W), BF16),
                   jax.ShapeDtypeStruct((ATT_KV_HEADS, t // tb, ATT_VT_ROWS, tb), BF16),
                   jax.ShapeDtypeStruct((t // tb, ATT_KV_HEADS, 1, ATT_HEAD_DIM), F32)],
        compiler_params=_cparams("parallel", "parallel"),
        name="k_prep",
    )(qkv, qkv, k_norm.reshape(1, ATT_HEAD_DIM), cosf, sinf)
    return qt, k, vt, kmean


def _attn_body(qt_ref, k_ref, vt_ref, km_ref, o_ref, acc_ref, qa_ref, s_ref, *, nb):
    qblk = pl.program_id(1)
    blk = MOBA_BLOCK
    hd = ATT_HEAD_DIM
    n_blocks = k_ref.shape[0] // blk
    nbp = (n_blocks + 7) // 8 * 8
    own0 = pl.multiple_of(qblk * blk, blk)
    km = km_ref[...]
    km_hi = km.astype(BF16)
    km_lo = (km - km_hi.astype(F32)).astype(BF16)
    bid = lax.broadcasted_iota(I32, (nbp, blk), 0)
    causal = lax.broadcasted_iota(I32, (blk, blk), 0) <= lax.broadcasted_iota(I32, (blk, blk), 1)

    m0 = []
    for r in range(ATT_REP):
        qt = qt_ref[r]
        gate = (jnp.dot(km_hi, qt, preferred_element_type=F32)
                + jnp.dot(km_lo, qt, preferred_element_type=F32))[:nbp]
        gate = jnp.where(bid < qblk, gate, -jnp.inf)
        picked = bid < 0
        for kk in range(MOBA_TOPK):
            mx = jnp.max(gate, axis=0, keepdims=True)
            idx = jnp.min(jnp.where(gate == mx, bid, LANES), axis=0, keepdims=True)
            hit = bid == idx
            picked = picked | (hit & (kk < qblk))
            gate = jnp.where(hit, -jnp.inf, gate)
        bias = jnp.where(picked, 0.0, MASK_BIAS).astype(BF16)
        qa_ref[r, :hd, :] = qt
        qa_ref[r, hd:hd + nbp, :] = bias
        if nbp < LANES:
            qa_ref[r, hd + nbp:, :] = jnp.zeros((LANES - nbp, blk), BF16)
        s = jnp.dot(k_ref[pl.ds(own0, blk), :hd], qt, preferred_element_type=F32)
        s = jnp.where(causal, s, -jnp.inf)
        m = jnp.max(s, axis=0, keepdims=True)
        p = jnp.exp2(s - m)
        m0.append(m)
        acc_ref[r] = jnp.dot(vt_ref[qblk], p.astype(BF16), preferred_element_type=F32)

    def blocks_of(jj):
        return [jnp.minimum(jj * nb + u, n_blocks - 1) for u in range(nb)]

    def head_scores(r, blocks):
        return [jnp.dot(k_ref[pl.ds(pl.multiple_of(ja * blk, blk), blk), :], qa_ref[r],
                        preferred_element_type=F32) for ja in blocks]

    ahead = ATT_REP - 1
    for r in range(ahead):
        for u, su in enumerate(head_scores(r, blocks_of(0))):
            s_ref[r * nb + u] = su

    def past_blocks(jj, ms):
        blocks = blocks_of(jj)
        blocks_next = blocks_of(jj + 1)
        vts = [vt_ref[ja] for ja in blocks]
        m_out = []
        in_flight = {}
        for r in range(ATT_REP):
            scores = in_flight.pop(r) if r in in_flight else [s_ref[r * nb + u] for u in range(nb)]
            r_req = r + ahead
            if r_req < ATT_REP:
                in_flight[r_req] = head_scores(r_req, blocks)
            else:
                for u, su in enumerate(head_scores(r_req - ATT_REP, blocks_next)):
                    s_ref[(r_req - ATT_REP) * nb + u] = su
            m_new = ms[r]
            for su in scores:
                m_new = jnp.maximum(m_new, jnp.max(su, axis=0, keepdims=True))
            alpha = jnp.exp2(ms[r] - m_new)
            pv = None
            for u in range(nb):
                pu = jnp.exp2(scores[u] - m_new).astype(BF16)
                d = jnp.dot(vts[u], pu, preferred_element_type=F32)
                pv = d if pv is None else pv + d
            acc_ref[r] = alpha * acc_ref[r] + pv
            m_out.append(m_new)
        return tuple(m_out)

    lax.fori_loop(0, (qblk + nb - 1) // nb, past_blocks, tuple(m0))

    for r in range(ATT_REP):
        out = acc_ref[r, :hd, :] / acc_ref[r, hd:hd + 1, :]
        o_ref[:, r * hd:(r + 1) * hd] = out.T.astype(o_ref.dtype)


def attention_call(qt, k, vt, kmean):
    t = k.shape[0]
    tq = MOBA_BLOCK
    n_blocks = t // MOBA_BLOCK
    km = jnp.transpose(kmean.reshape(n_blocks, ATT_KV_HEADS, ATT_HEAD_DIM), (1, 0, 2))
    km = jnp.pad(km, ((0, 0), (0, LANES - n_blocks), (0, 0)))
    return pl.pallas_call(
        functools.partial(_attn_body, nb=ATT_NB),
        grid=(ATT_KV_HEADS, t // tq),
        in_specs=[pl.BlockSpec((ATT_REP, ATT_HEAD_DIM, tq), lambda g, i: (g, 0, i)),
                  pl.BlockSpec((t, ATT_KAUG_W), lambda g, i: (0, g)),
                  pl.BlockSpec((None, n_blocks, ATT_VT_ROWS, MOBA_BLOCK), lambda g, i: (g, 0, 0, 0)),
                  pl.BlockSpec((None, LANES, ATT_HEAD_DIM), lambda g, i: (g, 0, 0))],
        out_specs=pl.BlockSpec((tq, ATT_REP * ATT_HEAD_DIM), lambda g, i: (i, g)),
        out_shape=jax.ShapeDtypeStruct((t, ATT_Q_W), BF16),
        scratch_shapes=[pltpu.VMEM((ATT_REP, ATT_VT_ROWS, tq), F32),
                        pltpu.VMEM((ATT_REP, ATT_KAUG_W, tq), BF16),
                        pltpu.VMEM(((ATT_REP - 1) * ATT_NB, MOBA_BLOCK, tq), F32)],
        compiler_params=_cparams("parallel", "parallel"),
        name="moba_attention",
    )(qt, k, vt, km)


def _router_body(h_ref, g_ref, wr_ref, br_ref, xn_ref, re_ref, rw_ref, cnt_ref):
    h = h_ref[...]
    ms = jnp.mean(h * h, axis=-1, keepdims=True)
    xn = h * lax.rsqrt(ms + NORM_EPS) * g_ref[...]
    xn_ref[...] = xn
    logits = jnp.dot(xn, wr_ref[...], precision=HIGHEST, preferred_element_type=F32) + br_ref[...]
    lane = lax.broadcasted_iota(I32, logits.shape, 1)
    neg = -jnp.inf

    is_g = lane < MOE_GROUPS
    gl = jnp.where(is_g, logits, neg)
    ge = jnp.exp(gl - jnp.max(gl, axis=1, keepdims=True))
    gp = ge / jnp.sum(ge, axis=1, keepdims=True)
    g_w = jnp.max(gp, axis=1, keepdims=True)
    g_idx = jnp.min(jnp.where(is_g & (gp == g_w), lane, LANES), axis=1, keepdims=True)

    e_lo = MOE_GROUPS + MOE_EPG * g_idx
    is_e = (lane >= e_lo) & (lane < e_lo + MOE_EPG)
    el = jnp.where(is_e, logits, neg)
    ee = jnp.exp(el - jnp.max(el, axis=1, keepdims=True))
    ep = jnp.where(is_e, ee / jnp.sum(ee, axis=1, keepdims=True), -1.0)
    p1 = jnp.max(ep, axis=1, keepdims=True)
    i1 = jnp.min(jnp.where(ep == p1, lane, LANES), axis=1, keepdims=True)
    ep2 = jnp.where(lane == i1, -1.0, ep)
    p2 = jnp.max(ep2, axis=1, keepdims=True)
    i2 = jnp.min(jnp.where(ep2 == p2, lane, LANES), axis=1, keepdims=True)
    den = p1 + p2
    w1 = g_w * (p1 / den)
    w2 = g_w * (p2 / den)
    rw_ref[...] = jnp.where(lane == 0, w1, jnp.where(lane == 1, w2, 0.0))

    @pl.when(pl.program_id(0) == 0)
    def _():
        cnt_ref[...] = jnp.zeros_like(cnt_ref)

    e1, e2 = i1 - MOE_GROUPS, i2 - MOE_GROUPS
    oh1 = (lane == e1).astype(F32)
    oh2 = (lane == e2).astype(F32)
    tm = h.shape[0]
    earlier = (lax.broadcasted_iota(I32, (tm, tm), 1) < lax.broadcasted_iota(I32, (tm, tm), 0)).astype(BF16)
    before = jnp.dot(earlier, (oh1 + oh2).astype(BF16), preferred_element_type=F32) + cnt_ref[...]
    r1 = jnp.sum(oh1 * before, axis=1, keepdims=True).astype(I32)
    r2 = jnp.sum(oh2 * (before + oh1), axis=1, keepdims=True).astype(I32)
    cnt_ref[...] = cnt_ref[...] + jnp.sum(oh1 + oh2, axis=0, keepdims=True)
    re_ref[...] = jnp.where(lane == 0, e1, jnp.where(lane == 1, e2, jnp.where(lane == 2, r1,
                                                                               jnp.where(lane == 3, r2, 0))))


def router_call(h, gain, w_group, b_group, w_expert, b_expert):
    t, d = h.shape
    nr = MOE_GROUPS + MOE_EXPERTS
    wr = jnp.zeros((d, LANES), F32).at[:, :MOE_GROUPS].set(w_group).at[:, MOE_GROUPS:nr].set(w_expert)
    br = jnp.zeros((1, LANES), F32).at[0, :MOE_GROUPS].set(b_group).at[0, MOE_GROUPS:nr].set(b_expert)
    return pl.pallas_call(
        _router_body,
        grid=(t // NORM_TM,),
        in_specs=[pl.BlockSpec((NORM_TM, d), lambda i: (i, 0)),
                  pl.BlockSpec((1, d), lambda i: (0, 0)),
                  pl.BlockSpec((d, LANES), lambda i: (0, 0)),
                  pl.BlockSpec((1, LANES), lambda i: (0, 0))],
        out_specs=[pl.BlockSpec((NORM_TM, d), lambda i: (i, 0)),
                   pl.BlockSpec((NORM_TM, LANES), lambda i: (i, 0)),
                   pl.BlockSpec((NORM_TM, LANES), lambda i: (i, 0)),
                   pl.BlockSpec((1, LANES), lambda i: (0, 0))],
        out_shape=[jax.ShapeDtypeStruct((t, d), F32),
                   jax.ShapeDtypeStruct((t, LANES), I32),
                   jax.ShapeDtypeStruct((t, LANES), F32),
                   jax.ShapeDtypeStruct((1, LANES), F32)],
        compiler_params=_cparams("arbitrary"),
        name="norm_router",
    )(h, gain.reshape(1, d), wr, br)


def _row_copy(src_ref, dst_ref, sem, src_row, dst_row):
    return pltpu.make_async_copy(src_ref.at[pl.ds(src_row, 1), :], dst_ref.at[pl.ds(dst_row, 1), :], sem)


def _start_row_gather(idx_ref, src_ref, dst_ref, sem, n):
    for r in range(n):
        _row_copy(src_ref, dst_ref, sem, idx_ref[0, r], r).start(priority=r % 2)


def _wait_row_gather(src_ref, dst_ref, sem, n):
    for r in range(n):
        _row_copy(src_ref, dst_ref, sem, 0, r).wait()


def _ffn_body(be_ref, nv_ref, tok0_ref, tokn_ref, x_hbm, wg_ref, wu_ref, wd_ref, o_ref, xbuf, sems):
    b = pl.program_id(0)
    nv = nv_ref[0]
    last = pl.num_programs(0) - 1
    tm = o_ref.shape[0]
    slot = b % 2
    valid = b < nv

    @pl.when(b == 0)
    def _():
        _start_row_gather(tok0_ref, x_hbm, xbuf.at[0], sems.at[0], tm)

    @pl.when(b <= nv)
    def _():
        _wait_row_gather(x_hbm, xbuf.at[slot], sems.at[slot], tm)

    @pl.when(valid)
    def _():
        _start_row_gather(tokn_ref, x_hbm, xbuf.at[1 - slot], sems.at[1 - slot], tm)
        x = xbuf[slot].astype(BF16)
        gate = jnp.dot(x, wg_ref[...], preferred_element_type=F32)
        up = jnp.dot(x, wu_ref[...], preferred_element_type=F32)
        hid = (_silu(gate) * up).astype(BF16)
        o_ref[...] = jnp.dot(hid, wd_ref[...], preferred_element_type=F32).astype(o_ref.dtype)

    @pl.when(jnp.logical_not(valid))
    def _():
        o_ref[...] = jnp.zeros_like(o_ref)

    @pl.when(valid & (b == last))
    def _():
        _wait_row_gather(x_hbm, xbuf.at[1 - slot], sems.at[1 - slot], tm)


def ffn_call(xn, row_token, blk_expert, n_valid, w_gate, w_up, w_down, layer):
    t, d = xn.shape
    ff = w_gate.shape[3]
    tm = MOE_TM
    n_rows = row_token.shape[0]
    n_blk = n_rows // tm

    def w_map(b, be, nv):
        return (layer, be[jnp.minimum(b, nv[0] - 1)], 0, 0)

    tok_spec = functools.partial(pl.BlockSpec, (None, 1, tm), memory_space=pltpu.SMEM)
    grid_spec = pltpu.PrefetchScalarGridSpec(
        num_scalar_prefetch=2,
        grid=(n_blk,),
        in_specs=[tok_spec(index_map=lambda b, be, nv: (0, 0, 0)),
                  tok_spec(index_map=lambda b, be, nv: (jnp.minimum(b + 1, n_blk - 1), 0, 0)),
                  pl.BlockSpec(memory_space=pl.ANY),
                  pl.BlockSpec((None, None, d, ff), w_map),
                  pl.BlockSpec((None, None, d, ff), w_map),
                  pl.BlockSpec((None, None, ff, d), w_map)],
        out_specs=pl.BlockSpec((tm, d), lambda b, be, nv: (b, 0)),
        scratch_shapes=[pltpu.VMEM((2, tm, d), xn.dtype), pltpu.SemaphoreType.DMA((2,))],
    )
    tok = row_token.reshape(n_blk, 1, tm)
    return pl.pallas_call(
        _ffn_body,
        grid_spec=grid_spec,
        out_shape=jax.ShapeDtypeStruct((n_rows, d), F32),
        compiler_params=_cparams("arbitrary"),
        name="moe_ffn",
    )(blk_expert, n_valid, tok, tok, xn, w_gate, w_up, w_down)


def _combine_body(pos_ref, posn_ref, y_ref, h_ref, rw_ref, g_ref, *out_refs_and_scratch, with_norm):
    if with_norm:
        ho_ref, xn_ref, buf_ref, sems = out_refs_and_scratch
    else:
        ho_ref, buf_ref, sems = out_refs_and_scratch
    tb = ho_ref.shape[0]
    n = MOE_TOPK * tb
    i = pl.program_id(0)
    slot = i % 2

    @pl.when(i == 0)
    def _():
        _start_row_gather(pos_ref, y_ref, buf_ref.at[0], sems.at[0], n)

    _wait_row_gather(y_ref, buf_ref.at[slot], sems.at[slot], n)

    @pl.when(i + 1 < pl.num_programs(0))
    def _():
        _start_row_gather(posn_ref, y_ref, buf_ref.at[1 - slot], sems.at[1 - slot], n)

    rw = rw_ref[...]
    ffn = rw[:, 0:1] * buf_ref[slot, 0:tb, :] + rw[:, 1:2] * buf_ref[slot, tb:2 * tb, :]
    hn = h_ref[...] + ffn
    ho_ref[...] = hn
    if with_norm:
        ms = jnp.mean(hn * hn, axis=-1, keepdims=True)
        xn_ref[...] = (hn * lax.rsqrt(ms + NORM_EPS) * g_ref[...]).astype(xn_ref.dtype)


def combine_call(y, pos, h, route_w, next_gain):
    t, d = h.shape
    tb = NORM_TM // 2
    n_steps = t // tb
    with_norm = next_gain is not None
    gain = (next_gain if with_norm else jnp.ones((d,), F32)).reshape(1, d)
    out_specs = [pl.BlockSpec((tb, d), lambda i: (i, 0))]
    out_shape = [jax.ShapeDtypeStruct((t, d), F32)]
    if with_norm:
        out_specs.append(pl.BlockSpec((tb, d), lambda i: (i, 0)))
        out_shape.append(jax.ShapeDtypeStruct((t, d), BF16))
    pos_spec = functools.partial(pl.BlockSpec, (None, 1, MOE_TOPK * tb), memory_space=pltpu.SMEM)
    outs = pl.pallas_call(
        functools.partial(_combine_body, with_norm=with_norm),
        grid=(n_steps,),
        in_specs=[pos_spec(index_map=lambda i: (i, 0, 0)),
                  pos_spec(index_map=lambda i: (jnp.minimum(i + 1, n_steps - 1), 0, 0)),
                  pl.BlockSpec(memory_space=pl.ANY),
                  pl.BlockSpec((tb, d), lambda i: (i, 0)),
                  pl.BlockSpec((tb, LANES), lambda i: (i, 0)),
                  pl.BlockSpec((1, d), lambda i: (0, 0))],
        out_specs=out_specs,
        out_shape=out_shape,
        scratch_shapes=[pltpu.VMEM((2, MOE_TOPK * tb, d), F32), pltpu.SemaphoreType.DMA((2,))],
        compiler_params=_cparams("arbitrary"),
        name="moe_combine",
    )(pos, pos, y, h, route_w, gain)
    return (outs[0], outs[1]) if with_norm else (outs[0], None)


def _dispatch_tables(route_e, counts, t):
    tm = MOE_TM
    n_assign = MOE_TOPK * t
    expert = route_e[:, :MOE_TOPK].reshape(n_assign)
    rank = route_e[:, MOE_TOPK:2 * MOE_TOPK].reshape(n_assign)
    counts = counts[0, :MOE_EXPERTS].astype(I32)
    padded = (counts + tm - 1) // tm * tm
    pend = jnp.cumsum(padded)
    pstart = pend - padded
    onehot = expert[:, None] == jnp.arange(MOE_EXPERTS, dtype=I32)[None, :]
    dest = jnp.sum(jnp.where(onehot, pstart[None, :], 0), axis=1) + rank
    n_rows = n_assign + MOE_EXPERTS * tm
    token = jnp.arange(n_assign, dtype=I32) // MOE_TOPK
    row_token = jnp.zeros((n_rows,), I32).at[dest].set(token)
    n_blk = n_rows // tm
    blk_expert = jnp.minimum(jnp.searchsorted(pend, jnp.arange(n_blk, dtype=I32) * tm, side='right'),
                             MOE_EXPERTS - 1).astype(I32)
    n_valid = (pend[-1] // tm).astype(I32).reshape(1)
    tb = NORM_TM // 2
    pos = dest.reshape(t // tb, tb, MOE_TOPK).transpose(0, 2, 1).reshape(t // tb, 1, MOE_TOPK * tb)
    return row_token, blk_expert, n_valid, pos.astype(I32)


def moe_layer(h, gain, w_group, b_group, w_expert, b_expert, w_gate16, w_up16, w_down16, layer, next_gain):
    t = h.shape[0]
    xn, route_e, route_w, counts = router_call(h, gain, w_group, b_group, w_expert, b_expert)
    row_token, blk_expert, n_valid, pos = _dispatch_tables(route_e, counts, t)
    y = ffn_call(xn, row_token, blk_expert, n_valid, w_gate16, w_up16, w_down16, layer)
    return combine_call(y, pos, h, route_w, next_gain)


def hybrid_layer(h, xn, w_in16, j, conv_w, conv_b, dt_bias, a_log, d_skip, ssd_norm, ln_g, ln_b, ws, bs,
                 w_out16):
    hp = SSD_HEADS_PER_GROUP
    w_dt = jnp.zeros((D_MODEL, SSD_GROUPS, LANES), BF16).at[:, :, :hp].set(
        w_in16[j, :, OFF_DT:OFF_U].reshape(D_MODEL, SSD_GROUPS, hp)).reshape(D_MODEL, SSD_GROUPS * LANES)
    proj_a = matmul_call([xn], w_in16, out_dtype=BF16, n_cols=OFF_DT, layer=j, name="hyb_in_proj_ssd")
    proj_b = matmul_call([xn], w_in16[j, :, OFF_U:], out_dtype=BF16, name="hyb_in_proj_gmlp")
    dt_raw = matmul_call([xn], w_dt, out_dtype=F32, tn=SSD_GROUPS * LANES, name="hyb_dt_proj")
    dtt_raw = dt_raw.reshape(-1, SSD_GROUPS, LANES)[:, :, :hp].reshape(-1, SSD_HEADS).T
    y_a = ssd_call(proj_a, dt_raw, dtt_raw, conv_w, conv_b, dt_bias, a_log, d_skip, ssd_norm)
    y_b = gmlp_call(proj_b, ln_g, ln_b, ws, bs)
    return matmul_call([y_a, y_b], w_out16, res=h, layer=j, name="hyb_out_proj")


def moba_layer(h, xn, w_qkv16, j, q_norm, k_norm, w_out16, cosf, sinf):
    qkv = matmul_call([xn], w_qkv16, out_dtype=BF16, layer=j, name="att_qkv_proj")
    qt, k, vt, kmean = qk_prep_call(qkv, q_norm, k_norm, cosf, sinf)
    o = attention_call(qt, k, vt, kmean)
    return matmul_call([o], w_out16, res=h, layer=j, name="att_out_proj")


def _rope_tables(t):
    inv = 1.0 / (ROPE_THETA ** (jnp.arange(0, ATT_HEAD_DIM, 2, dtype=F32) / ATT_HEAD_DIM))
    ang = jnp.arange(t, dtype=F32)[:, None] * inv[None, :]
    cos, sin = jnp.cos(ang), jnp.sin(ang)
    return jnp.concatenate([cos, cos], axis=1), jnp.concatenate([-sin, sin], axis=1)


def kernel(x, norm_mix, norm_ffn, hyb_w_in, ssd_conv_w, ssd_conv_b, ssd_dt_bias, ssd_a_log, ssd_d, ssd_norm, gm_ln_g, gm_ln_b, gm_ws, gm_bs, hyb_w_out, att_w_qkv, att_q_norm, att_k_norm, att_w_out, moe_w_group, moe_b_group, moe_w_expert, moe_b_expert, moe_w_gate, moe_w_up, moe_w_down):
    bsz, t, d = x.shape
    depth = norm_mix.shape[0]
    cosf, sinf = _rope_tables(t)
    hyb_w_in16, hyb_w_out16 = hyb_w_in.astype(BF16), hyb_w_out.astype(BF16)
    att_w_qkv16, att_w_out16 = att_w_qkv.astype(BF16), att_w_out.astype(BF16)
    moe_w_gate16, moe_w_up16, moe_w_down16 = (w.astype(BF16) for w in (moe_w_gate, moe_w_up, moe_w_down))
    outs = []
    for b in range(bsz):
        h = x[b]
        xn = rmsnorm_call(h, norm_mix[0])
        for layer in range(depth):
            j = layer // 2
            if layer % 2 == 0:
                h = hybrid_layer(h, xn, hyb_w_in16, j, ssd_conv_w[j], ssd_conv_b[j], ssd_dt_bias[j],
                                 ssd_a_log[j], ssd_d[j], ssd_norm[j], gm_ln_g[j], gm_ln_b[j],
                                 gm_ws[j], gm_bs[j], hyb_w_out16)
            else:
                h = moba_layer(h, xn, att_w_qkv16, j, att_q_norm[j], att_k_norm[j], att_w_out16, cosf, sinf)
            next_gain = norm_mix[layer + 1] if layer + 1 < depth else None
            h, xn = moe_layer(h, norm_ffn[layer], moe_w_group[layer], moe_b_group[layer],
                              moe_w_expert[layer], moe_b_expert[layer], moe_w_gate16, moe_w_up16,
                              moe_w_down16, layer, next_gain)
        outs.append(h)
    return jnp.stack(outs, axis=0)
```

```python
import functools

import jax
import jax.numpy as jnp
from jax import lax
from jax.experimental import pallas as pl
from jax.experimental.pallas import tpu as pltpu

F32 = jnp.float32
BF16 = jnp.bfloat16
I32 = jnp.int32

D_MODEL = 4096
NORM_EPS = 1e-6

SSD_INNER = 2048
SSD_HEAD_DIM = 64
SSD_HEADS = 32
SSD_GROUPS = 4
SSD_STATE = 128
SSD_CONV = 4
SSD_CHUNK = 256
SSD_BC = SSD_GROUPS * SSD_STATE
SSD_CONV_CH = SSD_INNER + 2 * SSD_BC
SSD_GROUP_W = SSD_INNER // SSD_GROUPS
SSD_HEADS_PER_GROUP = SSD_HEADS // SSD_GROUPS

GM_INNER = 2048
GM_CHUNK = 128
GM_GROUPS = 16
GM_GROUP_DIM = 128

OFF_XBC = SSD_INNER
OFF_DT = OFF_XBC + SSD_CONV_CH
OFF_U = OFF_DT + SSD_HEADS
OFF_V = OFF_U + GM_INNER

ATT_HEAD_DIM = 128
ATT_HEADS = 32
ATT_KV_HEADS = 8
ATT_REP = ATT_HEADS // ATT_KV_HEADS
ATT_Q_W = ATT_HEADS * ATT_HEAD_DIM
ATT_KV_W = ATT_KV_HEADS * ATT_HEAD_DIM
ROPE_THETA = 10000.0
MOBA_BLOCK = 256
MOBA_TOPK = 3

MOE_GROUPS = 4
MOE_EPG = 4
MOE_EXPERTS = 16
MOE_TOPK = 2
MOE_FF = 768

LANES = 128
VMEM_LIMIT_BYTES = 56 * 1024 * 1024

MM_TM = 512
MM_TN = 1024
NORM_TM = 256
ATT_KAUG_W = ATT_HEAD_DIM + LANES
ATT_VT_ROWS = ATT_HEAD_DIM + 16
MASK_BIAS = -1e30
ATT_NB = 2
QK_SCALE_LOG2E = ATT_HEAD_DIM ** -0.5 * 1.4426950408889634
MOE_TM = 256
HIGHEST = lax.Precision.HIGHEST


def _cparams(*sem, flags=None):
    return pltpu.CompilerParams(dimension_semantics=sem, vmem_limit_bytes=VMEM_LIMIT_BYTES, flags=flags)


def _silu(x):
    return x / (1.0 + jnp.exp(-x))


def _softplus(x):
    return jnp.maximum(x, 0.0) + jnp.log1p(jnp.exp(-jnp.abs(x)))


def _gelu_erf(x):
    return 0.5 * x * (1.0 + lax.erf(x * (2.0 ** -0.5)))


def _dot_f32_by_01(x, m01, f32_on_right=False):
    hi = x.astype(BF16)
    r1 = x - hi.astype(F32)
    mid = r1.astype(BF16)
    lo = (r1 - mid.astype(F32)).astype(BF16)
    if f32_on_right:
        return sum(jnp.dot(m01, t, preferred_element_type=F32) for t in (hi, mid, lo))
    return sum(jnp.dot(t, m01, preferred_element_type=F32) for t in (hi, mid, lo))


def _rmsnorm_body(x_ref, g_ref, o_ref):
    x = x_ref[...]
    ms = jnp.mean(x * x, axis=-1, keepdims=True)
    o_ref[...] = (x * lax.rsqrt(ms + NORM_EPS) * g_ref[...]).astype(o_ref.dtype)


def rmsnorm_call(x, g):
    t, d = x.shape
    return pl.pallas_call(
        _rmsnorm_body,
        grid=(t // NORM_TM,),
        in_specs=[pl.BlockSpec((NORM_TM, d), lambda i: (i, 0)),
                  pl.BlockSpec((1, d), lambda i: (0, 0))],
        out_specs=pl.BlockSpec((NORM_TM, d), lambda i: (i, 0)),
        out_shape=jax.ShapeDtypeStruct((t, d), BF16),
        compiler_params=_cparams("parallel"),
        name="rmsnorm",
    )(x, g.reshape(1, d))


def _matmul_body(*refs, n_parts, has_res, n_side):
    a_refs = refs[:n_parts]
    w_refs = refs[n_parts:2 * n_parts]
    n_in = 2 * n_parts + int(has_res) + n_side
    o_ref = refs[n_in]
    acc = jnp.dot(a_refs[0][...], w_refs[0][...], preferred_element_type=F32)
    for p in range(1, n_parts):
        acc = acc + jnp.dot(a_refs[p][...], w_refs[p][...], preferred_element_type=F32)
    if has_res:
        acc = acc + refs[2 * n_parts][...]
    o_ref[...] = acc.astype(o_ref.dtype)
    for s in range(n_side):
        refs[n_in + 1 + s][...] = refs[n_in - n_side + s][...].astype(BF16)


def _side_blocks(rows, n_steps):
    for nsb in range(min(n_steps, rows // 16), 0, -1):
        if rows % nsb == 0 and (rows // nsb) % 16 == 0:
            return nsb
    return 1


def matmul_call(a_parts, w, res=None, out_dtype=F32, tn=MM_TN, n_cols=None, layer=None, side_casts=(),
                name="matmul"):
    n_parts = len(a_parts)
    m, kp = a_parts[0].shape
    n = w.shape[-1] if n_cols is None else n_cols
    tm = MM_TM
    steps_i = m // tm
    n_steps = (n // tn) * steps_i
    in_specs = [pl.BlockSpec((tm, kp), lambda j, i: (i, 0)) for _ in a_parts]
    if layer is None:
        in_specs += [pl.BlockSpec((kp, tn), functools.partial(lambda j, i, p: (p, j), p=p))
                     for p in range(n_parts)]
    else:
        in_specs += [pl.BlockSpec((None, kp, tn), functools.partial(lambda j, i, p: (layer, p, j), p=p))
                     for p in range(n_parts)]
    args = list(a_parts) + [w] * n_parts
    if res is not None:
        in_specs.append(pl.BlockSpec((tm, tn), lambda j, i: (i, j)))
        args.append(res)
    out_specs = [pl.BlockSpec((tm, tn), lambda j, i: (i, j))]
    out_shape = [jax.ShapeDtypeStruct((m, n), out_dtype)]
    for src, idx in side_casts:
        _, rows, cols = src.shape
        nsb = _side_blocks(rows, n_steps)
        slab = functools.partial(lambda j, i, nsb: jnp.minimum(j * steps_i + i, nsb - 1), nsb=nsb)
        in_specs.append(pl.BlockSpec((None, rows // nsb, cols),
                                     functools.partial(lambda j, i, slab, idx: (idx, slab(j, i), 0),
                                                       slab=slab, idx=idx)))
        out_specs.append(pl.BlockSpec((rows // nsb, cols),
                                      functools.partial(lambda j, i, slab: (slab(j, i), 0), slab=slab)))
        args.append(src)
        out_shape.append(jax.ShapeDtypeStruct((rows, cols), BF16))
    outs = pl.pallas_call(
        functools.partial(_matmul_body, n_parts=n_parts, has_res=res is not None, n_side=len(side_casts)),
        grid=(n // tn, m // tm),
        in_specs=in_specs,
        out_specs=out_specs,
        out_shape=out_shape,
        compiler_params=_cparams("arbitrary", "arbitrary"),
        name=name,
    )(*args)
    return tuple(outs) if side_casts else outs[0]


def _ssd_body(z_ref, xs_ref, b_ref, c_ref, dt_ref, dtt_ref,
              wx_ref, bx_ref, wb_ref, bb_ref, wc_ref, bc_ref,
              dtb_ref, dtbt_ref, alog_ref, alogt_ref, dskip_ref, ng_ref, expand_ref,
              o_ref, state_ref, tx_ref, tb_ref, tc_ref):
    c = pl.program_id(1)
    L = SSD_CHUNK
    HP = SSD_HEADS_PER_GROUP

    @pl.when(c == 0)
    def _():
        state_ref[...] = jnp.zeros_like(state_ref)
        tx_ref[...] = jnp.zeros_like(tx_ref)
        tb_ref[...] = jnp.zeros_like(tb_ref)
        tc_ref[...] = jnp.zeros_like(tc_ref)

    def conv_silu(cur_ref, tail_ref, w_ref, bias_ref):
        cur = cur_ref[...].astype(F32)
        ext = jnp.concatenate([tail_ref[...], cur], axis=0)
        w = w_ref[...]
        acc = bias_ref[...] + w[SSD_CONV - 1:SSD_CONV, :] * cur
        for j in range(1, SSD_CONV):
            acc = acc + w[SSD_CONV - 1 - j:SSD_CONV - j, :] * ext[8 - j:8 - j + L, :]
        tail_ref[...] = cur[L - 8:L, :]
        return _silu(acc)

    xs = conv_silu(xs_ref, tx_ref, wx_ref, bx_ref)
    bm = conv_silu(b_ref, tb_ref, wb_ref, bb_ref)
    cm = conv_silu(c_ref, tc_ref, wc_ref, bc_ref)

    dt = _softplus(dt_ref[...] + dtb_ref[...])
    da = dt * (-jnp.exp(alog_ref[...]))
    dtt = _softplus(dtt_ref[...] + dtbt_ref[...])
    dat = dtt * (-jnp.exp(alogt_ref[...]))

    row = lax.broadcasted_iota(I32, (L, L), 0)
    col = lax.broadcasted_iota(I32, (L, L), 1)
    causal = row >= col
    cs = _dot_f32_by_01(da, causal.astype(BF16), f32_on_right=True)
    cst = _dot_f32_by_01(dat, (row <= col).astype(BF16))
    cs_last = cs[L - 1:L, :]
    to_end = jnp.exp(cs_last - cs)
    ecs = jnp.exp(cs)

    expand = expand_ref[...]
    dt_e = _dot_f32_by_01(dt, expand)
    to_end_e = _dot_f32_by_01(to_end, expand)
    ecs_e = _dot_f32_by_01(ecs, expand)

    x = xs * dt_e
    bm16 = bm.astype(BF16)
    cm16 = cm.astype(BF16)
    cb = lax.dot_general(cm16, bm16, (((1,), (1,)), ((), ())), preferred_element_type=F32)

    prev = state_ref[...]
    y = jnp.dot(cm16, prev.astype(BF16), preferred_element_type=F32) * ecs_e

    x16 = x.astype(BF16)
    heads_per_tile = LANES // SSD_HEAD_DIM
    first_head = lax.broadcasted_iota(I32, (L, LANES), 1) < SSD_HEAD_DIM
    tiles = []
    for tt in range(SSD_GROUP_W // LANES):
        x_tile = x16[:, tt * LANES:(tt + 1) * LANES]
        y_tile = None
        for hh in range(heads_per_tile):
            h = tt * heads_per_tile + hh
            seg = cs[:, h:h + 1] - cst[h:h + 1, :]
            decay = jnp.exp(jnp.where(causal, seg, -jnp.inf))
            m16 = (cb * decay).astype(BF16)
            xh = jnp.where(first_head == (hh == 0), x_tile, jnp.zeros((), BF16))
            d = jnp.dot(m16, xh, preferred_element_type=F32)
            y_tile = d if y_tile is None else y_tile + d
        tiles.append(y_tile)
    y = y + jnp.concatenate(tiles, axis=1)

    xw = (x * to_end_e).astype(BF16)
    state_ref[...] = prev * ecs_e[L - 1:L, :] + jnp.dot(bm.T.astype(BF16), xw,
                                                        preferred_element_type=F32)

    y = y + xs * dskip_ref[...]
    y = y * _silu(z_ref[...].astype(F32))
    ms = jnp.mean(y * y, axis=-1, keepdims=True)
    o_ref[...] = (y * lax.rsqrt(ms + NORM_EPS) * ng_ref[...]).astype(o_ref.dtype)


def ssd_call(proj, dt_raw, dtt_raw, conv_w, conv_b, dt_bias, a_log, d_skip, norm_g):
    t = proj.shape[0]
    L = SSD_CHUNK
    gw = SSD_GROUP_W
    xs0 = SSD_INNER // gw
    b0 = 2 * SSD_INNER // SSD_STATE
    c0 = b0 + SSD_GROUPS
    hp = SSD_HEADS_PER_GROUP

    dtb = jnp.zeros((SSD_GROUPS, LANES), F32).at[:, :hp].set(dt_bias.reshape(SSD_GROUPS, hp))
    alog = jnp.zeros((SSD_GROUPS, LANES), F32).at[:, :hp].set(a_log.reshape(SSD_GROUPS, hp))
    dtb = dtb.reshape(1, SSD_GROUPS * LANES)
    alog = alog.reshape(1, SSD_GROUPS * LANES)
    dtbt = dt_bias.reshape(SSD_HEADS, 1)
    alogt = a_log.reshape(SSD_HEADS, 1)
    dskip = jnp.repeat(d_skip, SSD_HEAD_DIM).reshape(1, SSD_INNER)
    expand = (jnp.arange(LANES)[:, None] == (jnp.arange(gw)[None, :] // SSD_HEAD_DIM)).astype(BF16)
    cw = conv_w.astype(F32)
    cbias = conv_b.reshape(1, SSD_CONV_CH).astype(F32)
    xb0 = SSD_INNER // SSD_STATE
    cb0 = xb0 + SSD_GROUPS

    in_specs = [
        pl.BlockSpec((L, gw), lambda g, c: (c, g)),
        pl.BlockSpec((L, gw), lambda g, c: (c, xs0 + g)),
        pl.BlockSpec((L, SSD_STATE), lambda g, c: (c, b0 + g)),
        pl.BlockSpec((L, SSD_STATE), lambda g, c: (c, c0 + g)),
        pl.BlockSpec((L, LANES), lambda g, c: (c, g)),
        pl.BlockSpec((hp, L), lambda g, c: (g, c)),
        pl.BlockSpec((SSD_CONV, gw), lambda g, c: (0, g)),
        pl.BlockSpec((1, gw), lambda g, c: (0, g)),
        pl.BlockSpec((SSD_CONV, SSD_STATE), lambda g, c: (0, xb0 + g)),
        pl.BlockSpec((1, SSD_STATE), lambda g, c: (0, xb0 + g)),
        pl.BlockSpec((SSD_CONV, SSD_STATE), lambda g, c: (0, cb0 + g)),
        pl.BlockSpec((1, SSD_STATE), lambda g, c: (0, cb0 + g)),
        pl.BlockSpec((1, LANES), lambda g, c: (0, g)),
        pl.BlockSpec((hp, 1), lambda g, c: (g, 0)),
        pl.BlockSpec((1, LANES), lambda g, c: (0, g)),
        pl.BlockSpec((hp, 1), lambda g, c: (g, 0)),
        pl.BlockSpec((1, gw), lambda g, c: (0, g)),
        pl.BlockSpec((1, gw), lambda g, c: (0, g)),
        pl.BlockSpec((LANES, gw), lambda g, c: (0, 0)),
    ]
    return pl.pallas_call(
        _ssd_body,
        grid=(SSD_GROUPS, t // L),
        in_specs=in_specs,
        out_specs=pl.BlockSpec((L, gw), lambda g, c: (c, g)),
        out_shape=jax.ShapeDtypeStruct((t, SSD_INNER), BF16),
        scratch_shapes=[pltpu.VMEM((SSD_STATE, gw), F32),
                        pltpu.VMEM((8, gw), F32),
                        pltpu.VMEM((8, SSD_STATE), F32),
                        pltpu.VMEM((8, SSD_STATE), F32)],
        compiler_params=_cparams("parallel", "arbitrary"),
        name="ssd",
    )(proj, proj, proj, proj, dt_raw, dtt_raw, cw, cbias, cw, cbias, cw, cbias,
      dtb, dtbt, alog, alogt, dskip, norm_g.reshape(1, SSD_INNER), expand)


def _gmlp_body(u_ref, v_ref, g_ref, b_ref, ws_ref, bst_ref, o_ref):
    u = _gelu_erf(u_ref[...].astype(F32))
    v = _gelu_erf(v_ref[...].astype(F32))
    mu = jnp.mean(v, axis=-1, keepdims=True)
    vc = v - mu
    var = jnp.mean(vc * vc, axis=-1, keepdims=True)
    vn = (vc * lax.rsqrt(var + NORM_EPS) * g_ref[...] + b_ref[...]).astype(BF16)
    row = lax.broadcasted_iota(I32, (GM_CHUNK, GM_CHUNK), 0)
    col = lax.broadcasted_iota(I32, (GM_CHUNK, GM_CHUNK), 1)
    causal = row >= col
    for g in range(GM_GROUPS):
        sl = slice(g * GM_GROUP_DIM, (g + 1) * GM_GROUP_DIM)
        w = jnp.where(causal, ws_ref[g], jnp.zeros((), BF16))
        sv = jnp.dot(w, vn[:, sl], preferred_element_type=F32) + bst_ref[:, sl]
        o_ref[:, sl] = (u[:, sl] * sv).astype(o_ref.dtype)


def gmlp_call(proj, ln_g, ln_b, ws, bs):
    t = proj.shape[0]
    ucol = 0
    bst =jnp.repeat(bs.T, GM_GROUP_DIM, axis=1)
    return pl.pallas_call(
        _gmlp_body,
        grid=(t // GM_CHUNK,),
        in_specs=[pl.BlockSpec((GM_CHUNK, GM_INNER), lambda i: (i, ucol)),
                  pl.BlockSpec((GM_CHUNK, GM_INNER), lambda i: (i, ucol + 1)),
                  pl.BlockSpec((1, GM_INNER), lambda i: (0, 0)),
                  pl.BlockSpec((1, GM_INNER), lambda i: (0, 0)),
                  pl.BlockSpec((GM_GROUPS, GM_CHUNK, GM_CHUNK), lambda i: (0, 0, 0)),
                  pl.BlockSpec((GM_CHUNK, GM_INNER), lambda i: (0, 0))],
        out_specs=pl.BlockSpec((GM_CHUNK, GM_INNER), lambda i: (i, 0)),
        out_shape=jax.ShapeDtypeStruct((t, GM_INNER), BF16),
        compiler_params=_cparams("parallel"),
        name="gmlp",
    )(proj, proj, ln_g.reshape(1, GM_INNER), ln_b.reshape(1, GM_INNER), ws.astype(BF16), bst)


def _rope_norm(x, gain, cosf, sinf):
    ms = jnp.mean(x * x, axis=-1, keepdims=True)
    xn = x * lax.rsqrt(ms + NORM_EPS) * gain
    return xn * cosf + pltpu.roll(xn, ATT_HEAD_DIM // 2, axis=1) * sinf


def _qprep_body(x_ref, g_ref, cos_ref, sin_ref, o_ref):
    for r in range(ATT_REP):
        x = x_ref[:, r * ATT_HEAD_DIM:(r + 1) * ATT_HEAD_DIM].astype(F32)
        y = _rope_norm(x, g_ref[...], cos_ref[...], sin_ref[...]) * QK_SCALE_LOG2E
        o_ref[r] = y.T.astype(o_ref.dtype)


def _kprep_body(x_ref, v_ref, g_ref, cos_ref, sin_ref, o_ref, vt_ref, mean_ref):
    y = _rope_norm(x_ref[...].astype(F32), g_ref[...], cos_ref[...], sin_ref[...])
    tb = y.shape[0]
    lane = lax.broadcasted_iota(I32, (tb, LANES), 1)
    o_ref[:, :ATT_HEAD_DIM] = y.astype(o_ref.dtype)
    o_ref[:, ATT_HEAD_DIM:] = jnp.where(lane == pl.program_id(0), 1.0, 0.0).astype(o_ref.dtype)
    mean_ref[...] = jnp.mean(y, axis=0, keepdims=True)
    vt_ref[:ATT_HEAD_DIM, :] = v_ref[...].astype(F32).T.astype(vt_ref.dtype)
    vt_ref[ATT_HEAD_DIM:, :] = jnp.ones((ATT_VT_ROWS - ATT_HEAD_DIM, tb), vt_ref.dtype)


def qk_prep_call(qkv, q_norm, k_norm, cosf, sinf):
    t = qkv.shape[0]
    tb = MOBA_BLOCK
    gw = ATT_REP * ATT_HEAD_DIM
    common = [pl.BlockSpec((1, ATT_HEAD_DIM), lambda i, h: (0, 0)),
              pl.BlockSpec((tb, ATT_HEAD_DIM), lambda i, h: (i, 0)),
              pl.BlockSpec((tb, ATT_HEAD_DIM), lambda i, h: (i, 0))]
    qt = pl.pallas_call(
        _qprep_body,
        grid=(t // tb, ATT_KV_HEADS),
        in_specs=[pl.BlockSpec((tb, gw), lambda i, h: (i, h))] + common,
        out_specs=pl.BlockSpec((ATT_REP, ATT_HEAD_DIM, tb), lambda i, h: (h, 0, i)),
        out_shape=jax.ShapeDtypeStruct((ATT_HEADS, ATT_HEAD_DIM, t), BF16),
        compiler_params=_cparams("parallel", "parallel"),
        name="q_prep",
    )(qkv, q_norm.reshape(1, ATT_HEAD_DIM), cosf, sinf)
    vcol = (ATT_Q_W + ATT_KV_W) // ATT_HEAD_DIM
    k, vt, kmean = pl.pallas_call(
        _kprep_body,
        grid=(t // tb, ATT_KV_HEADS),
        in_specs=[pl.BlockSpec((tb, ATT_HEAD_DIM), lambda i, h: (i, ATT_HEADS + h)),
                  pl.BlockSpec((tb, ATT_HEAD_DIM), lambda i, h: (i, vcol + h))] + common,
        out_specs=[pl.BlockSpec((tb, ATT_KAUG_W), lambda i, h: (i, h)),
                   pl.BlockSpec((None, None, ATT_VT_ROWS, tb), lambda i, h: (h, i, 0, 0)),
                   pl.BlockSpec((None, None, 1, ATT_HEAD_DIM), lambda i, h: (i, h, 0, 0))],
        out_shape=[jax.ShapeDtypeStruct((t, ATT_KV_HEADS * ATT_KAUG_W), BF16),
                   jax.ShapeDtypeStruct((ATT_KV_HEADS, t // tb, ATT_VT_ROWS, tb), BF16),
                   jax.ShapeDtypeStruct((t // tb, ATT_KV_HEADS, 1, ATT_HEAD_DIM), F32)],
        compiler_params=_cparams("parallel", "parallel"),
        name="k_prep",
    )(qkv, qkv, k_norm.reshape(1, ATT_HEAD_DIM), cosf, sinf)
    return qt, k, vt, kmean


def _attn_body(qt_ref, k_ref, vt_ref, km_ref, o_ref, acc_ref, qa_ref, s_ref, *, nb):
    qblk = pl.program_id(1)
    blk = MOBA_BLOCK
    hd = ATT_HEAD_DIM
    n_blocks = k_ref.shape[0] // blk
    nbp = (n_blocks + 7) // 8 * 8
    own0 = pl.multiple_of(qblk * blk, blk)
    km = km_ref[...]
    km_hi = km.astype(BF16)
    km_lo = (km - km_hi.astype(F32)).astype(BF16)
    bid = lax.broadcasted_iota(I32, (nbp, blk), 0)
    causal = lax.broadcasted_iota(I32, (blk, blk), 0) <= lax.broadcasted_iota(I32, (blk, blk), 1)

    m0 = []
    for r in range(ATT_REP):
        qt = qt_ref[r]
        gate = (jnp.dot(km_hi, qt, preferred_element_type=F32)
                + jnp.dot(km_lo, qt, preferred_element_type=F32))[:nbp]
        gate = jnp.where(bid < qblk, gate, -jnp.inf)
        picked = bid < 0
        for kk in range(MOBA_TOPK):
            mx = jnp.max(gate, axis=0, keepdims=True)
            idx = jnp.min(jnp.where(gate == mx, bid, LANES), axis=0, keepdims=True)
            hit = bid == idx
            picked = picked | (hit & (kk < qblk))
            gate = jnp.where(hit, -jnp.inf, gate)
        bias = jnp.where(picked, 0.0, MASK_BIAS).astype(BF16)
        qa_ref[r, :hd, :] = qt
        qa_ref[r, hd:hd + nbp, :] = bias
        if nbp < LANES:
            qa_ref[r, hd + nbp:, :] = jnp.zeros((LANES - nbp, blk), BF16)
        s = jnp.dot(k_ref[pl.ds(own0, blk), :hd], qt, preferred_element_type=F32)
        s = jnp.where(causal, s, -jnp.inf)
        m = jnp.max(s, axis=0, keepdims=True)
        p = jnp.exp2(s - m)
        m0.append(m)
        acc_ref[r] = jnp.dot(vt_ref[qblk], p.astype(BF16), preferred_element_type=F32)

    def blocks_of(jj):
        return [jnp.minimum(jj * nb + u, n_blocks - 1) for u in range(nb)]

    def head_scores(r, blocks):
        return [jnp.dot(k_ref[pl.ds(pl.multiple_of(ja * blk, blk), blk), :], qa_ref[r],
                        preferred_element_type=F32) for ja in blocks]

    ahead = ATT_REP - 1
    for r in range(ahead):
        for u, su in enumerate(head_scores(r, blocks_of(0))):
            s_ref[r * nb + u] = su

    def past_blocks(jj, ms):
        blocks = blocks_of(jj)
        blocks_next = blocks_of(jj + 1)
        vts = [vt_ref[ja] for ja in blocks]
        m_out = []
        in_flight = {}
        for r in range(ATT_REP):
            scores = in_flight.pop(r) if r in in_flight else [s_ref[r * nb + u] for u in range(nb)]
            r_req = r + ahead
            if r_req < ATT_REP:
                in_flight[r_req] = head_scores(r_req, blocks)
            else:
                for u, su in enumerate(head_scores(r_req - ATT_REP, blocks_next)):
                    s_ref[(r_req - ATT_REP) * nb + u] = su
            m_new = ms[r]
            for su in scores:
                m_new = jnp.maximum(m_new, jnp.max(su, axis=0, keepdims=True))
            alpha = jnp.exp2(ms[r] - m_new)
            pv = None
            for u in range(nb):
                pu = jnp.exp2(scores[u] - m_new).astype(BF16)
                d = jnp.dot(vts[u], pu, preferred_element_type=F32)
                pv = d if pv is None else pv + d
            acc_ref[r] = alpha * acc_ref[r] + pv
            m_out.append(m_new)
        return tuple(m_out)

    lax.fori_loop(0, (qblk + nb - 1) // nb, past_blocks, tuple(m0))

    for r in range(ATT_REP):
        out = acc_ref[r, :hd, :] / acc_ref[r, hd:hd + 1, :]
        o_ref[:, r * hd:(r + 1) * hd] = out.T.astype(o_ref.dtype)


def attention_call(qt, k, vt, kmean):
    t = k.shape[0]
    tq = MOBA_BLOCK
    n_blocks = t // MOBA_BLOCK
    km = jnp.transpose(kmean.reshape(n_blocks, ATT_KV_HEADS, ATT_HEAD_DIM), (1, 0, 2))
    km = jnp.pad(km, ((0, 0), (0, LANES - n_blocks), (0, 0)))
    return pl.pallas_call(
        functools.partial(_attn_body, nb=ATT_NB),
        grid=(ATT_KV_HEADS, t // tq),
        in_specs=[pl.BlockSpec((ATT_REP, ATT_HEAD_DIM, tq), lambda g, i: (g, 0, i)),
                  pl.BlockSpec((t, ATT_KAUG_W), lambda g, i: (0, g)),
                  pl.BlockSpec((None, n_blocks, ATT_VT_ROWS, MOBA_BLOCK), lambda g, i: (g, 0, 0, 0)),
                  pl.BlockSpec((None, LANES, ATT_HEAD_DIM), lambda g, i: (g, 0, 0))],
        out_specs=pl.BlockSpec((tq, ATT_REP * ATT_HEAD_DIM), lambda g, i: (i, g)),
        out_shape=jax.ShapeDtypeStruct((t, ATT_Q_W), BF16),
        scratch_shapes=[pltpu.VMEM((ATT_REP, ATT_VT_ROWS, tq), F32),
                        pltpu.VMEM((ATT_REP, ATT_KAUG_W, tq), BF16),
                        pltpu.VMEM(((ATT_REP - 1) * ATT_NB, MOBA_BLOCK, tq), F32)],
        compiler_params=_cparams("parallel", "parallel"),
        name="moba_attention",
    )(qt, k, vt, km)


def _router_body(h_ref, g_ref, wr_ref, br_ref, xn_ref, re_ref, rw_ref, cnt_ref):
    h = h_ref[...]
    ms = jnp.mean(h * h, axis=-1, keepdims=True)
    xn = h * lax.rsqrt(ms + NORM_EPS) * g_ref[...]
    xn_ref[...] = xn
    logits = jnp.dot(xn, wr_ref[...], precision=HIGHEST, preferred_element_type=F32) + br_ref[...]
    lane = lax.broadcasted_iota(I32, logits.shape, 1)
    neg = -jnp.inf

    is_g = lane < MOE_GROUPS
    gl = jnp.where(is_g, logits, neg)
    ge = jnp.exp(gl - jnp.max(gl, axis=1, keepdims=True))
    gp = ge / jnp.sum(ge, axis=1, keepdims=True)
    g_w = jnp.max(gp, axis=1, keepdims=True)
    g_idx = jnp.min(jnp.where(is_g & (gp == g_w), lane, LANES), axis=1, keepdims=True)

    e_lo = MOE_GROUPS + MOE_EPG * g_idx
    is_e = (lane >= e_lo) & (lane < e_lo + MOE_EPG)
    el = jnp.where(is_e, logits, neg)
    ee = jnp.exp(el - jnp.max(el, axis=1, keepdims=True))
    ep = jnp.where(is_e, ee / jnp.sum(ee, axis=1, keepdims=True), -1.0)
    p1 = jnp.max(ep, axis=1, keepdims=True)
    i1 = jnp.min(jnp.where(ep == p1, lane, LANES), axis=1, keepdims=True)
    ep2 = jnp.where(lane == i1, -1.0, ep)
    p2 = jnp.max(ep2, axis=1, keepdims=True)
    i2 = jnp.min(jnp.where(ep2 == p2, lane, LANES), axis=1, keepdims=True)
    den = p1 + p2
    w1 = g_w * (p1 / den)
    w2 = g_w * (p2 / den)
    rw_ref[...] = jnp.where(lane == 0, w1, jnp.where(lane == 1, w2, 0.0))

    @pl.when(pl.program_id(0) == 0)
    def _():
        cnt_ref[...] = jnp.zeros_like(cnt_ref)

    e1, e2 = i1 - MOE_GROUPS, i2 - MOE_GROUPS
    oh1 = (lane == e1).astype(F32)
    oh2 = (lane == e2).astype(F32)
    tm = h.shape[0]
    earlier = (lax.broadcasted_iota(I32, (tm, tm), 1) < lax.broadcasted_iota(I32, (tm, tm), 0)).astype(BF16)
    before = jnp.dot(earlier, (oh1 + oh2).astype(BF16), preferred_element_type=F32) + cnt_ref[...]
    r1 = jnp.sum(oh1 * before, axis=1, keepdims=True).astype(I32)
    r2 = jnp.sum(oh2 * (before + oh1), axis=1, keepdims=True).astype(I32)
    cnt_ref[...] = cnt_ref[...] + jnp.sum(oh1 + oh2, axis=0, keepdims=True)
    re_ref[...] = jnp.where(lane == 0, e1, jnp.where(lane == 1, e2, jnp.where(lane == 2, r1,
                                                                               jnp.where(lane == 3, r2, 0))))


def router_call(h, gain, w_group, b_group, w_expert, b_expert):
    t, d = h.shape
    nr = MOE_GROUPS + MOE_EXPERTS
    wr = jnp.zeros((d, LANES), F32).at[:, :MOE_GROUPS].set(w_group).at[:, MOE_GROUPS:nr].set(w_expert)
    br = jnp.zeros((1, LANES), F32).at[0, :MOE_GROUPS].set(b_group).at[0, MOE_GROUPS:nr].set(b_expert)
    return pl.pallas_call(
        _router_body,
        grid=(t // NORM_TM,),
        in_specs=[pl.BlockSpec((NORM_TM, d), lambda i: (i, 0)),
                  pl.BlockSpec((1, d), lambda i: (0, 0)),
                  pl.BlockSpec((d, LANES), lambda i: (0, 0)),
                  pl.BlockSpec((1, LANES), lambda i: (0, 0))],
        out_specs=[pl.BlockSpec((NORM_TM, d), lambda i: (i, 0)),
                   pl.BlockSpec((NORM_TM, LANES), lambda i: (i, 0)),
                   pl.BlockSpec((NORM_TM, LANES), lambda i: (i, 0)),
                   pl.BlockSpec((1, LANES), lambda i: (0, 0))],
        out_shape=[jax.ShapeDtypeStruct((t, d), F32),
                   jax.ShapeDtypeStruct((t, LANES), I32),
                   jax.ShapeDtypeStruct((t, LANES), F32),
                   jax.ShapeDtypeStruct((1, LANES), F32)],
        compiler_params=_cparams("arbitrary"),
        name="norm_router",
    )(h, gain.reshape(1, d), wr, br)


def _row_copy(src_ref, dst_ref, sem, src_row, dst_row):
    return pltpu.make_async_copy(src_ref.at[pl.ds(src_row, 1), :], dst_ref.at[pl.ds(dst_row, 1), :], sem)


def _start_row_gather(idx_ref, src_ref, dst_ref, sem, n):
    for r in range(n):
        _row_copy(src_ref, dst_ref, sem, idx_ref[0, r], r).start(priority=r % 2)


def _wait_row_gather(src_ref, dst_ref, sem, n):
    for r in range(n):
        _row_copy(src_ref, dst_ref, sem, 0, r).wait()


def _ffn_body(be_ref, nv_ref, tok0_ref, tokn_ref, x_hbm, wg_ref, wu_ref, wd_ref, o_ref, xbuf, sems):
    b = pl.program_id(0)
    nv = nv_ref[0]
    last = pl.num_programs(0) - 1
    tm = o_ref.shape[0]
    slot = b % 2
    valid = b < nv

    @pl.when(b == 0)
    def _():
        _start_row_gather(tok0_ref, x_hbm, xbuf.at[0], sems.at[0], tm)

    @pl.when(b <= nv)
    def _():
        _wait_row_gather(x_hbm, xbuf.at[slot], sems.at[slot], tm)

    @pl.when(valid)
    def _():
        _start_row_gather(tokn_ref, x_hbm, xbuf.at[1 - slot], sems.at[1 - slot], tm)
        x = xbuf[slot].astype(BF16)
        gate = jnp.dot(x, wg_ref[...], preferred_element_type=F32)
        up = jnp.dot(x, wu_ref[...], preferred_element_type=F32)
        hid = (_silu(gate) * up).astype(BF16)
        o_ref[...] = jnp.dot(hid, wd_ref[...], preferred_element_type=F32).astype(o_ref.dtype)

    @pl.when(jnp.logical_not(valid))
    def _():
        o_ref[...] = jnp.zeros_like(o_ref)

    @pl.when(valid & (b == last))
    def _():
        _wait_row_gather(x_hbm, xbuf.at[1 - slot], sems.at[1 - slot], tm)


def ffn_call(xn, row_token, blk_expert, n_valid, w_gate, w_up, w_down, layer):
    t, d = xn.shape
    ff = w_gate.shape[3]
    tm = MOE_TM
    n_rows = row_token.shape[0]
    n_blk = n_rows // tm

    def w_map(b, be, nv):
        return (layer, be[jnp.minimum(b, nv[0] - 1)], 0, 0)

    tok_spec = functools.partial(pl.BlockSpec, (None, 1, tm), memory_space=pltpu.SMEM)
    grid_spec = pltpu.PrefetchScalarGridSpec(
        num_scalar_prefetch=2,
        grid=(n_blk,),
        in_specs=[tok_spec(index_map=lambda b, be, nv: (0, 0, 0)),
                  tok_spec(index_map=lambda b, be, nv: (jnp.minimum(b + 1, n_blk - 1), 0, 0)),
                  pl.BlockSpec(memory_space=pl.ANY),
                  pl.BlockSpec((None, None, d, ff), w_map),
                  pl.BlockSpec((None, None, d, ff), w_map),
                  pl.BlockSpec((None, None, ff, d), w_map)],
        out_specs=pl.BlockSpec((tm, d), lambda b, be, nv: (b, 0)),
        scratch_shapes=[pltpu.VMEM((2, tm, d), xn.dtype), pltpu.SemaphoreType.DMA((2,))],
    )
    tok = row_token.reshape(n_blk, 1, tm)
    return pl.pallas_call(
        _ffn_body,
        grid_spec=grid_spec,
        out_shape=jax.ShapeDtypeStruct((n_rows, d), F32),
        compiler_params=_cparams("arbitrary"),
        name="moe_ffn",
    )(blk_expert, n_valid, tok, tok, xn, w_gate, w_up, w_down)


def _combine_body(pos_ref, posn_ref, y_ref, h_ref, rw_ref, g_ref, *out_refs_and_scratch, with_norm):
    if with_norm:
        ho_ref, xn_ref, buf_ref, sems = out_refs_and_scratch
    else:
        ho_ref, buf_ref, sems = out_refs_and_scratch
    tb = ho_ref.shape[0]
    n = MOE_TOPK * tb
    i = pl.program_id(0)
    slot = i % 2

    @pl.when(i == 0)
    def _():
        _start_row_gather(pos_ref, y_ref, buf_ref.at[0], sems.at[0], n)

    _wait_row_gather(y_ref, buf_ref.at[slot], sems.at[slot], n)

    @pl.when(i + 1 < pl.num_programs(0))
    def _():
        _start_row_gather(posn_ref, y_ref, buf_ref.at[1 - slot], sems.at[1 - slot], n)

    rw = rw_ref[...]
    ffn = rw[:, 0:1] * buf_ref[slot, 0:tb, :] + rw[:, 1:2] * buf_ref[slot, tb:2 * tb, :]
    hn = h_ref[...] + ffn
    ho_ref[...] = hn
    if with_norm:
        ms = jnp.mean(hn * hn, axis=-1, keepdims=True)
        xn_ref[...] = (hn * lax.rsqrt(ms + NORM_EPS) * g_ref[...]).astype(xn_ref.dtype)


def combine_call(y, pos, h, route_w, next_gain):
    t, d = h.shape
    tb = NORM_TM // 2
    n_steps = t // tb
    with_norm = next_gain is not None
    gain = (next_gain if with_norm else jnp.ones((d,), F32)).reshape(1, d)
    out_specs = [pl.BlockSpec((tb, d), lambda i: (i, 0))]
    out_shape = [jax.ShapeDtypeStruct((t, d), F32)]
    if with_norm:
        out_specs.append(pl.BlockSpec((tb, d), lambda i: (i, 0)))
        out_shape.append(jax.ShapeDtypeStruct((t, d), BF16))
    pos_spec = functools.partial(pl.BlockSpec, (None, 1, MOE_TOPK * tb), memory_space=pltpu.SMEM)
    outs = pl.pallas_call(
        functools.partial(_combine_body, with_norm=with_norm),
        grid=(n_steps,),
        in_specs=[pos_spec(index_map=lambda i: (i, 0, 0)),
                  pos_spec(index_map=lambda i: (jnp.minimum(i + 1, n_steps - 1), 0, 0)),
                  pl.BlockSpec(memory_space=pl.ANY),
                  pl.BlockSpec((tb, d), lambda i: (i, 0)),
                  pl.BlockSpec((tb, LANES), lambda i: (i, 0)),
                  pl.BlockSpec((1, d), lambda i: (0, 0))],
        out_specs=out_specs,
        out_shape=out_shape,
        scratch_shapes=[pltpu.VMEM((2, MOE_TOPK * tb, d), F32), pltpu.SemaphoreType.DMA((2,))],
        compiler_params=_cparams("arbitrary"),
        name="moe_combine",
    )(pos, pos, y, h, route_w, gain)
    return (outs[0], outs[1]) if with_norm else (outs[0], None)


def _dispatch_tables(route_e, counts, t):
    tm = MOE_TM
    n_assign = MOE_TOPK * t
    expert = route_e[:, :MOE_TOPK].reshape(n_assign)
    rank = route_e[:, MOE_TOPK:2 * MOE_TOPK].reshape(n_assign)
    counts = counts[0, :MOE_EXPERTS].astype(I32)
    padded = (counts + tm - 1) // tm * tm
    pend = jnp.cumsum(padded)
    pstart = pend - padded
    onehot = expert[:, None] == jnp.arange(MOE_EXPERTS, dtype=I32)[None, :]
    dest = jnp.sum(jnp.where(onehot, pstart[None, :], 0), axis=1) + rank
    n_rows = n_assign + MOE_EXPERTS * tm
    token = jnp.arange(n_assign, dtype=I32) // MOE_TOPK
    row_token = jnp.zeros((n_rows,), I32).at[dest].set(token)
    n_blk = n_rows // tm
    blk_expert = jnp.minimum(jnp.searchsorted(pend, jnp.arange(n_blk, dtype=I32) * tm, side='right'),
                             MOE_EXPERTS - 1).astype(I32)
    n_valid = (pend[-1] // tm).astype(I32).reshape(1)
    tb = NORM_TM // 2
    pos = dest.reshape(t // tb, tb, MOE_TOPK).transpose(0, 2, 1).reshape(t // tb, 1, MOE_TOPK * tb)
    return row_token, blk_expert, n_valid, pos.astype(I32)


def moe_layer(h, gain, w_group, b_group, w_expert, b_expert, expert_w16, next_gain):
    t, d = h.shape
    xn, route_e, route_w, counts = router_call(h, gain, w_group, b_group, w_expert, b_expert)
    row_token, blk_expert, n_valid, pos = _dispatch_tables(route_e, counts, t)
    g16, u16, d16 = expert_w16
    y = ffn_call(xn, row_token, blk_expert, n_valid, g16.reshape(1, MOE_EXPERTS, d, -1),
                 u16.reshape(1, MOE_EXPERTS, d, -1), d16.reshape(1, MOE_EXPERTS, -1, d), 0)
    return combine_call(y, pos, h, route_w, next_gain)


def hybrid_layer(h, xn, w_in16, j, conv_w, conv_b, dt_bias, a_log, d_skip, ssd_norm, ln_g, ln_b, ws, bs,
                 w_out16, expert_w, layer):
    wg, wu, wd = expert_w
    hp = SSD_HEADS_PER_GROUP
    w_dt = jnp.zeros((D_MODEL, SSD_GROUPS, LANES), BF16).at[:, :, :hp].set(
        w_in16[j, :, OFF_DT:OFF_U].reshape(D_MODEL, SSD_GROUPS, hp)).reshape(D_MODEL, SSD_GROUPS * LANES)
    proj_a, g16 = matmul_call([xn], w_in16, out_dtype=BF16, n_cols=OFF_DT, layer=j,
                              side_casts=[(wg, layer)], name="hyb_in_proj_ssd")
    proj_b, u16 = matmul_call([xn], w_in16[j, :, OFF_U:], out_dtype=BF16, side_casts=[(wu, layer)],
                              name="hyb_in_proj_gmlp")
    dt_raw = matmul_call([xn], w_dt, out_dtype=F32, tn=SSD_GROUPS * LANES, name="hyb_dt_proj")
    dtt_raw = dt_raw.reshape(-1, SSD_GROUPS, LANES)[:, :, :hp].reshape(-1, SSD_HEADS).T
    y_a = ssd_call(proj_a, dt_raw, dtt_raw, conv_w, conv_b, dt_bias, a_log, d_skip, ssd_norm)
    y_b = gmlp_call(proj_b, ln_g, ln_b, ws, bs)
    h, d16 = matmul_call([y_a, y_b], w_out16, res=h, layer=j, side_casts=[(wd, layer)], name="hyb_out_proj")
    return h, (g16, u16, d16)


def moba_layer(h, xn, w_qkv16, j, q_norm, k_norm, w_out16, cosf, sinf, expert_w, layer):
    wg, wu, wd = expert_w
    qkv, g16, u16 = matmul_call([xn], w_qkv16, out_dtype=BF16, layer=j,
                                side_casts=[(wg, layer), (wu, layer)], name="att_qkv_proj")
    qt, k, vt, kmean = qk_prep_call(qkv, q_norm, k_norm, cosf, sinf)
    o = attention_call(qt, k, vt, kmean)
    h, d16 = matmul_call([o], w_out16, res=h, layer=j, side_casts=[(wd, layer)], name="att_out_proj")
    return h, (g16, u16, d16)


def _rope_tables(t):
    inv = 1.0 / (ROPE_THETA ** (jnp.arange(0, ATT_HEAD_DIM, 2, dtype=F32) / ATT_HEAD_DIM))
    ang = jnp.arange(t, dtype=F32)[:, None] * inv[None, :]
    cos, sin = jnp.cos(ang), jnp.sin(ang)
    return jnp.concatenate([cos, cos], axis=1), jnp.concatenate([-sin, sin], axis=1)


def kernel(x, norm_mix, norm_ffn, hyb_w_in, ssd_conv_w, ssd_conv_b, ssd_dt_bias, ssd_a_log, ssd_d, ssd_norm, gm_ln_g, gm_ln_b, gm_ws, gm_bs, hyb_w_out, att_w_qkv, att_q_norm, att_k_norm, att_w_out, moe_w_group, moe_b_group, moe_w_expert, moe_b_expert, moe_w_gate, moe_w_up, moe_w_down):
    bsz, t, d = x.shape
    depth = norm_mix.shape[0]
    cosf, sinf = _rope_tables(t)
    hyb_w_in16, hyb_w_out16 = hyb_w_in.astype(BF16), hyb_w_out.astype(BF16)
    att_w_qkv16, att_w_out16 = att_w_qkv.astype(BF16), att_w_out.astype(BF16)
    ff = moe_w_gate.shape[-1]
    expert_w = (moe_w_gate.reshape(depth, MOE_EXPERTS * d, ff), moe_w_up.reshape(depth, MOE_EXPERTS * d, ff),
                moe_w_down.reshape(depth, MOE_EXPERTS * ff, d))
    outs = []
    for b in range(bsz):
        h = x[b]
        xn = rmsnorm_call(h, norm_mix[0])
        for layer in range(depth):
            j = layer // 2
            if layer % 2 == 0:
                h, expert_w16 = hybrid_layer(h, xn, hyb_w_in16, j, ssd_conv_w[j], ssd_conv_b[j],
                                             ssd_dt_bias[j], ssd_a_log[j], ssd_d[j], ssd_norm[j], gm_ln_g[j],
                                             gm_ln_b[j], gm_ws[j], gm_bs[j], hyb_w_out16, expert_w, layer)
            else:
                h, expert_w16 = moba_layer(h, xn, att_w_qkv16, j, att_q_norm[j], att_k_norm[j], att_w_out16,
                                           cosf, sinf, expert_w, layer)
            next_gain = norm_mix[layer + 1] if layer + 1 < depth else None
            h, xn = moe_layer(h, norm_ffn[layer], moe_w_group[layer], moe_b_group[layer],
                              moe_w_expert[layer], moe_b_expert[layer], expert_w16, next_gain)
        outs.append(h)
    return jnp.stack(outs, axis=0)
```

```python
import functools

import jax
import jax.numpy as jnp
from jax import lax
from jax.experimental import pallas as pl
from jax.experimental.pallas import tpu as pltpu

F32 = jnp.float32
BF16 = jnp.bfloat16
I32 = jnp.int32

D_MODEL = 4096
NORM_EPS = 1e-6

SSD_INNER = 2048
SSD_HEAD_DIM = 64
SSD_HEADS = 32
SSD_GROUPS = 4
SSD_STATE = 128
SSD_CONV = 4
SSD_CHUNK = 256
SSD_BC = SSD_GROUPS * SSD_STATE
SSD_CONV_CH = SSD_INNER + 2 * SSD_BC
SSD_GROUP_W = SSD_INNER // SSD_GROUPS
SSD_HEADS_PER_GROUP = SSD_HEADS // SSD_GROUPS

GM_INNER = 2048
GM_CHUNK = 128
GM_GROUPS = 16
GM_GROUP_DIM = 128

OFF_XBC = SSD_INNER
OFF_DT = OFF_XBC + SSD_CONV_CH
OFF_U = OFF_DT + SSD_HEADS
OFF_V = OFF_U + GM_INNER

ATT_HEAD_DIM = 128
ATT_HEADS = 32
ATT_KV_HEADS = 8
ATT_REP = ATT_HEADS // ATT_KV_HEADS
ATT_Q_W = ATT_HEADS * ATT_HEAD_DIM
ATT_KV_W = ATT_KV_HEADS * ATT_HEAD_DIM
ROPE_THETA = 10000.0
MOBA_BLOCK = 256
MOBA_TOPK = 3

MOE_GROUPS = 4
MOE_EPG = 4
MOE_EXPERTS = 16
MOE_TOPK = 2
MOE_FF = 768

LANES = 128
VMEM_LIMIT_BYTES = 56 * 1024 * 1024

MM_TM = 512
MM_TN = 1024
NORM_TM = 256
ATT_KAUG_W = ATT_HEAD_DIM + LANES
ATT_VT_ROWS = ATT_HEAD_DIM + 16
MASK_BIAS = -1e30
ATT_NB = 2
QK_SCALE_LOG2E = ATT_HEAD_DIM ** -0.5 * 1.4426950408889634
MOE_TM = 256


def _cparams(*sem, flags=None):
    return pltpu.CompilerParams(dimension_semantics=sem, vmem_limit_bytes=VMEM_LIMIT_BYTES, flags=flags)


def _silu(x):
    return x / (1.0 + jnp.exp(-x))


def _softplus(x):
    return jnp.maximum(x, 0.0) + jnp.log1p(jnp.exp(-jnp.abs(x)))


def _gelu_erf(x):
    return 0.5 * x * (1.0 + lax.erf(x * (2.0 ** -0.5)))


def _dot_f32_by_01(x, m01, f32_on_right=False):
    hi = x.astype(BF16)
    r1 = x - hi.astype(F32)
    mid = r1.astype(BF16)
    lo = (r1 - mid.astype(F32)).astype(BF16)
    if f32_on_right:
        return sum(jnp.dot(m01, t, preferred_element_type=F32) for t in (hi, mid, lo))
    return sum(jnp.dot(t, m01, preferred_element_type=F32) for t in (hi, mid, lo))


def _rmsnorm_body(x_ref, g_ref, o_ref):
    x = x_ref[...]
    ms = jnp.mean(x * x, axis=-1, keepdims=True)
    o_ref[...] = (x * lax.rsqrt(ms + NORM_EPS) * g_ref[...]).astype(o_ref.dtype)


def rmsnorm_call(x, g):
    t, d = x.shape
    return pl.pallas_call(
        _rmsnorm_body,
        grid=(t // NORM_TM,),
        in_specs=[pl.BlockSpec((NORM_TM, d), lambda i: (i, 0)),
                  pl.BlockSpec((1, d), lambda i: (0, 0))],
        out_specs=pl.BlockSpec((NORM_TM, d), lambda i: (i, 0)),
        out_shape=jax.ShapeDtypeStruct((t, d), BF16),
        compiler_params=_cparams("parallel"),
        name="rmsnorm",
    )(x, g.reshape(1, d))


def _matmul_body(*refs, n_parts, has_res, n_side):
    a_refs = refs[:n_parts]
    w_refs = refs[n_parts:2 * n_parts]
    n_in = 2 * n_parts + int(has_res) + n_side
    o_ref = refs[n_in]
    acc = jnp.dot(a_refs[0][...], w_refs[0][...], preferred_element_type=F32)
    for p in range(1, n_parts):
        acc = acc + jnp.dot(a_refs[p][...], w_refs[p][...], preferred_element_type=F32)
    if has_res:
        acc = acc + refs[2 * n_parts][...]
    o_ref[...] = acc.astype(o_ref.dtype)
    for s in range(n_side):
        refs[n_in + 1 + s][...] = refs[n_in - n_side + s][...].astype(BF16)


def _side_blocks(rows, n_steps):
    for nsb in range(min(n_steps, rows // 16), 0, -1):
        if rows % nsb == 0 and (rows // nsb) % 16 == 0:
            return nsb
    return 1


def matmul_call(a_parts, w, res=None, out_dtype=F32, tn=MM_TN, n_cols=None, layer=None, side_casts=(),
                name="matmul"):
    n_parts = len(a_parts)
    m, kp = a_parts[0].shape
    n = w.shape[-1] if n_cols is None else n_cols
    tm = MM_TM
    steps_i = m // tm
    n_steps = (n // tn) * steps_i
    in_specs = [pl.BlockSpec((tm, kp), lambda j, i: (i, 0)) for _ in a_parts]
    if layer is None:
        in_specs += [pl.BlockSpec((kp, tn), functools.partial(lambda j, i, p: (p, j), p=p))
                     for p in range(n_parts)]
    else:
        in_specs += [pl.BlockSpec((None, kp, tn), functools.partial(lambda j, i, p: (layer, p, j), p=p))
                     for p in range(n_parts)]
    args = list(a_parts) + [w] * n_parts
    if res is not None:
        in_specs.append(pl.BlockSpec((tm, tn), lambda j, i: (i, j)))
        args.append(res)
    out_specs = [pl.BlockSpec((tm, tn), lambda j, i: (i, j))]
    out_shape = [jax.ShapeDtypeStruct((m, n), out_dtype)]
    for src, idx in side_casts:
        _, rows, cols = src.shape
        nsb = _side_blocks(rows, n_steps)
        slab = functools.partial(lambda j, i, nsb: jnp.minimum(j * steps_i + i, nsb - 1), nsb=nsb)
        in_specs.append(pl.BlockSpec((None, rows // nsb, cols),
                                     functools.partial(lambda j, i, slab, idx: (idx, slab(j, i), 0),
                                                       slab=slab, idx=idx)))
        out_specs.append(pl.BlockSpec((rows // nsb, cols),
                                      functools.partial(lambda j, i, slab: (slab(j, i), 0), slab=slab)))
        args.append(src)
        out_shape.append(jax.ShapeDtypeStruct((rows, cols), BF16))
    outs = pl.pallas_call(
        functools.partial(_matmul_body, n_parts=n_parts, has_res=res is not None, n_side=len(side_casts)),
        grid=(n // tn, m // tm),
        in_specs=in_specs,
        out_specs=out_specs,
        out_shape=out_shape,
        compiler_params=_cparams("arbitrary", "arbitrary"),
        name=name,
    )(*args)
    return tuple(outs) if side_casts else outs[0]


def _ssd_body(z_ref, xs_ref, b_ref, c_ref, dt_ref, dtt_ref,
              wx_ref, bx_ref, wb_ref, bb_ref, wc_ref, bc_ref,
              dtb_ref, dtbt_ref, alog_ref, alogt_ref, dskip_ref, ng_ref, expand_ref,
              o_ref, state_ref, tx_ref, tb_ref, tc_ref):
    c = pl.program_id(1)
    L = SSD_CHUNK
    HP = SSD_HEADS_PER_GROUP

    @pl.when(c == 0)
    def _():
        state_ref[...] = jnp.zeros_like(state_ref)
        tx_ref[...] = jnp.zeros_like(tx_ref)
        tb_ref[...] = jnp.zeros_like(tb_ref)
        tc_ref[...] = jnp.zeros_like(tc_ref)

    def conv_silu(cur_ref, tail_ref, w_ref, bias_ref):
        cur = cur_ref[...].astype(F32)
        ext = jnp.concatenate([tail_ref[...], cur], axis=0)
        w = w_ref[...]
        acc = bias_ref[...] + w[SSD_CONV - 1:SSD_CONV, :] * cur
        for j in range(1, SSD_CONV):
            acc = acc + w[SSD_CONV - 1 - j:SSD_CONV - j, :] * ext[8 - j:8 - j + L, :]
        tail_ref[...] = cur[L - 8:L, :]
        return _silu(acc)

    xs = conv_silu(xs_ref, tx_ref, wx_ref, bx_ref)
    bm = conv_silu(b_ref, tb_ref, wb_ref, bb_ref)
    cm = conv_silu(c_ref, tc_ref, wc_ref, bc_ref)

    dt = _softplus(dt_ref[...] + dtb_ref[...])
    da = dt * (-jnp.exp(alog_ref[...]))
    dtt = _softplus(dtt_ref[...] + dtbt_ref[...])
    dat = dtt * (-jnp.exp(alogt_ref[...]))

    row = lax.broadcasted_iota(I32, (L, L), 0)
    col = lax.broadcasted_iota(I32, (L, L), 1)
    causal = row >= col
    cs = _dot_f32_by_01(da, causal.astype(BF16), f32_on_right=True)
    cst = _dot_f32_by_01(dat, (row <= col).astype(BF16))
    cs_last = cs[L - 1:L, :]
    to_end = jnp.exp(cs_last - cs)
    ecs = jnp.exp(cs)

    expand = expand_ref[...]
    dt_e = _dot_f32_by_01(dt, expand)
    to_end_e = _dot_f32_by_01(to_end, expand)
    ecs_e = _dot_f32_by_01(ecs, expand)

    x = xs * dt_e
    bm16 = bm.astype(BF16)
    cm16 = cm.astype(BF16)
    cb = lax.dot_general(cm16, bm16, (((1,), (1,)), ((), ())), preferred_element_type=F32)

    prev = state_ref[...]
    y = jnp.dot(cm16, prev.astype(BF16), preferred_element_type=F32) * ecs_e

    x16 = x.astype(BF16)
    heads_per_tile = LANES // SSD_HEAD_DIM
    first_head = lax.broadcasted_iota(I32, (L, LANES), 1) < SSD_HEAD_DIM
    tiles = []
    for tt in range(SSD_GROUP_W // LANES):
        x_tile = x16[:, tt * LANES:(tt + 1) * LANES]
        y_tile = None
        for hh in range(heads_per_tile):
            h = tt * heads_per_tile + hh
            seg = cs[:, h:h + 1] - cst[h:h + 1, :]
            decay = jnp.exp(jnp.where(causal, seg, -jnp.inf))
            m16 = (cb * decay).astype(BF16)
            xh = jnp.where(first_head == (hh == 0), x_tile, jnp.zeros((), BF16))
            d = jnp.dot(m16, xh, preferred_element_type=F32)
            y_tile = d if y_tile is None else y_tile + d
        tiles.append(y_tile)
    y = y + jnp.concatenate(tiles, axis=1)

    xw = (x * to_end_e).astype(BF16)
    state_ref[...] = prev * ecs_e[L - 1:L, :] + jnp.dot(bm.T.astype(BF16), xw,
                                                        preferred_element_type=F32)

    y = y + xs * dskip_ref[...]
    y = y * _silu(z_ref[...].astype(F32))
    ms = jnp.mean(y * y, axis=-1, keepdims=True)
    o_ref[...] = (y * lax.rsqrt(ms + NORM_EPS) * ng_ref[...]).astype(o_ref.dtype)


def ssd_call(proj, dt_raw, dtt_raw, conv_w, conv_b, dt_bias, a_log, d_skip, norm_g):
    t = proj.shape[0]
    L = SSD_CHUNK
    gw = SSD_GROUP_W
    xs0 = SSD_INNER // gw
    b0 = 2 * SSD_INNER // SSD_STATE
    c0 = b0 + SSD_GROUPS
    hp = SSD_HEADS_PER_GROUP

    dtb = jnp.zeros((SSD_GROUPS, LANES), F32).at[:, :hp].set(dt_bias.reshape(SSD_GROUPS, hp))
    alog = jnp.zeros((SSD_GROUPS, LANES), F32).at[:, :hp].set(a_log.reshape(SSD_GROUPS, hp))
    dtb = dtb.reshape(1, SSD_GROUPS * LANES)
    alog = alog.reshape(1, SSD_GROUPS * LANES)
    dtbt = dt_bias.reshape(SSD_HEADS, 1)
    alogt = a_log.reshape(SSD_HEADS, 1)
    dskip = jnp.repeat(d_skip, SSD_HEAD_DIM).reshape(1, SSD_INNER)
    expand = (jnp.arange(LANES)[:, None] == (jnp.arange(gw)[None, :] // SSD_HEAD_DIM)).astype(BF16)
    cw = conv_w.astype(F32)
    cbias = conv_b.reshape(1, SSD_CONV_CH).astype(F32)
    xb0 = SSD_INNER // SSD_STATE
    cb0 = xb0 + SSD_GROUPS

    in_specs = [
        pl.BlockSpec((L, gw), lambda g, c: (c, g)),
        pl.BlockSpec((L, gw), lambda g, c: (c, xs0 + g)),
        pl.BlockSpec((L, SSD_STATE), lambda g, c: (c, b0 + g)),
        pl.BlockSpec((L, SSD_STATE), lambda g, c: (c, c0 + g)),
        pl.BlockSpec((L, LANES), lambda g, c: (c, g)),
        pl.BlockSpec((hp, L), lambda g, c: (g, c)),
        pl.BlockSpec((SSD_CONV, gw), lambda g, c: (0, g)),
        pl.BlockSpec((1, gw), lambda g, c: (0, g)),
        pl.BlockSpec((SSD_CONV, SSD_STATE), lambda g, c: (0, xb0 + g)),
        pl.BlockSpec((1, SSD_STATE), lambda g, c: (0, xb0 + g)),
        pl.BlockSpec((SSD_CONV, SSD_STATE), lambda g, c: (0, cb0 + g)),
        pl.BlockSpec((1, SSD_STATE), lambda g, c: (0, cb0 + g)),
        pl.BlockSpec((1, LANES), lambda g, c: (0, g)),
        pl.BlockSpec((hp, 1), lambda g, c: (g, 0)),
        pl.BlockSpec((1, LANES), lambda g, c: (0, g)),
        pl.BlockSpec((hp, 1), lambda g, c: (g, 0)),
        pl.BlockSpec((1, gw), lambda g, c: (0, g)),
        pl.BlockSpec((1, gw), lambda g, c: (0, g)),
        pl.BlockSpec((LANES, gw), lambda g, c: (0, 0)),
    ]
    return pl.pallas_call(
        _ssd_body,
        grid=(SSD_GROUPS, t // L),
        in_specs=in_specs,
        out_specs=pl.BlockSpec((L, gw), lambda g, c: (c, g)),
        out_shape=jax.ShapeDtypeStruct((t, SSD_INNER), BF16),
        scratch_shapes=[pltpu.VMEM((SSD_STATE, gw), F32),
                        pltpu.VMEM((8, gw), F32),
                        pltpu.VMEM((8, SSD_STATE), F32),
                        pltpu.VMEM((8, SSD_STATE), F32)],
        compiler_params=_cparams("parallel", "arbitrary"),
        name="ssd",
    )(proj, proj, proj, proj, dt_raw, dtt_raw, cw, cbias, cw, cbias, cw, cbias,
      dtb, dtbt, alog, alogt, dskip, norm_g.reshape(1, SSD_INNER), expand)


def _gmlp_body(u_ref, v_ref, g_ref, b_ref, ws_ref, bst_ref, o_ref):
    u = _gelu_erf(u_ref[...].astype(F32))
    v = _gelu_erf(v_ref[...].astype(F32))
    mu = jnp.mean(v, axis=-1, keepdims=True)
    vc = v - mu
    var = jnp.mean(vc * vc, axis=-1, keepdims=True)
    vn = (vc * lax.rsqrt(var + NORM_EPS) * g_ref[...] + b_ref[...]).astype(BF16)
    row = lax.broadcasted_iota(I32, (GM_CHUNK, GM_CHUNK), 0)
    col = lax.broadcasted_iota(I32, (GM_CHUNK, GM_CHUNK), 1)
    causal = row >= col
    for g in range(GM_GROUPS):
        sl = slice(g * GM_GROUP_DIM, (g + 1) * GM_GROUP_DIM)
        w = jnp.where(causal, ws_ref[g], jnp.zeros((), BF16))
        sv = jnp.dot(w, vn[:, sl], preferred_element_type=F32) + bst_ref[:, sl]
        o_ref[:, sl] = (u[:, sl] * sv).astype(o_ref.dtype)


def gmlp_call(proj, ln_g, ln_b, ws, bs):
    t = proj.shape[0]
    ucol = 0
    bst =jnp.repeat(bs.T, GM_GROUP_DIM, axis=1)
    return pl.pallas_call(
        _gmlp_body,
        grid=(t // GM_CHUNK,),
        in_specs=[pl.BlockSpec((GM_CHUNK, GM_INNER), lambda i: (i, ucol)),
                  pl.BlockSpec((GM_CHUNK, GM_INNER), lambda i: (i, ucol + 1)),
                  pl.BlockSpec((1, GM_INNER), lambda i: (0, 0)),
                  pl.BlockSpec((1, GM_INNER), lambda i: (0, 0)),
                  pl.BlockSpec((GM_GROUPS, GM_CHUNK, GM_CHUNK), lambda i: (0, 0, 0)),
                  pl.BlockSpec((GM_CHUNK, GM_INNER), lambda i: (0, 0))],
        out_specs=pl.BlockSpec((GM_CHUNK, GM_INNER), lambda i: (i, 0)),
        out_shape=jax.ShapeDtypeStruct((t, GM_INNER), BF16),
        compiler_params=_cparams("parallel"),
        name="gmlp",
    )(proj, proj, ln_g.reshape(1, GM_INNER), ln_b.reshape(1, GM_INNER), ws.astype(BF16), bst)


def _rope_norm(x, gain, cosf, sinf):
    ms = jnp.mean(x * x, axis=-1, keepdims=True)
    xn = x * lax.rsqrt(ms + NORM_EPS) * gain
    return xn * cosf + pltpu.roll(xn, ATT_HEAD_DIM // 2, axis=1) * sinf


def _qprep_body(x_ref, g_ref, cos_ref, sin_ref, o_ref):
    for r in range(ATT_REP):
        x = x_ref[:, r * ATT_HEAD_DIM:(r + 1) * ATT_HEAD_DIM].astype(F32)
        y = _rope_norm(x, g_ref[...], cos_ref[...], sin_ref[...]) * QK_SCALE_LOG2E
        o_ref[r] = y.T.astype(o_ref.dtype)


def _kprep_body(x_ref, v_ref, g_ref, cos_ref, sin_ref, o_ref, vt_ref, mean_ref):
    y = _rope_norm(x_ref[...].astype(F32), g_ref[...], cos_ref[...], sin_ref[...])
    tb = y.shape[0]
    lane = lax.broadcasted_iota(I32, (tb, LANES), 1)
    o_ref[:, :ATT_HEAD_DIM] = y.astype(o_ref.dtype)
    o_ref[:, ATT_HEAD_DIM:] = jnp.where(lane == pl.program_id(0), 1.0, 0.0).astype(o_ref.dtype)
    mean_ref[...] = jnp.mean(y, axis=0, keepdims=True)
    vt_ref[:ATT_HEAD_DIM, :] = v_ref[...].astype(F32).T.astype(vt_ref.dtype)
    vt_ref[ATT_HEAD_DIM:, :] = jnp.ones((ATT_VT_ROWS - ATT_HEAD_DIM, tb), vt_ref.dtype)


def qk_prep_call(qkv, q_norm, k_norm, cosf, sinf):
    t = qkv.shape[0]
    tb = MOBA_BLOCK
    gw = ATT_REP * ATT_HEAD_DIM
    common = [pl.BlockSpec((1, ATT_HEAD_DIM), lambda i, h: (0, 0)),
              pl.BlockSpec((tb, ATT_HEAD_DIM), lambda i, h: (i, 0)),
              pl.BlockSpec((tb, ATT_HEAD_DIM), lambda i, h: (i, 0))]
    qt = pl.pallas_call(
        _qprep_body,
        grid=(t // tb, ATT_KV_HEADS),
        in_specs=[pl.BlockSpec((tb, gw), lambda i, h: (i, h))] + common,
        out_specs=pl.BlockSpec((ATT_REP, ATT_HEAD_DIM, tb), lambda i, h: (h, 0, i)),
        out_shape=jax.ShapeDtypeStruct((ATT_HEADS, ATT_HEAD_DIM, t), BF16),
        compiler_params=_cparams("parallel", "parallel"),
        name="q_prep",
    )(qkv, q_norm.reshape(1, ATT_HEAD_DIM), cosf, sinf)
    vcol = (ATT_Q_W + ATT_KV_W) // ATT_HEAD_DIM
    k, vt, kmean = pl.pallas_call(
        _kprep_body,
        grid=(t // tb, ATT_KV_HEADS),
        in_specs=[pl.BlockSpec((tb, ATT_HEAD_DIM), lambda i, h: (i, ATT_HEADS + h)),
                  pl.BlockSpec((tb, ATT_HEAD_DIM), lambda i, h: (i, vcol + h))] + common,
        out_specs=[pl.BlockSpec((tb, ATT_KAUG_W), lambda i, h: (i, h)),
                   pl.BlockSpec((None, None, ATT_VT_ROWS, tb), lambda i, h: (h, i, 0, 0)),
                   pl.BlockSpec((None, None, 1, ATT_HEAD_DIM), lambda i, h: (i, h, 0, 0))],
        out_shape=[jax.ShapeDtypeStruct((t, ATT_KV_HEADS * ATT_KAUG_W), BF16),
                   jax.ShapeDtypeStruct((ATT_KV_HEADS, t // tb, ATT_VT_ROWS, tb), BF16),
                   jax.ShapeDtypeStruct((t // tb, ATT_KV_HEADS, 1, ATT_HEAD_DIM), F32)],
        compiler_params=_cparams("parallel", "parallel"),
        name="k_prep",
    )(qkv, qkv, k_norm.reshape(1, ATT_HEAD_DIM), cosf, sinf)
    return qt, k, vt, kmean


def _attn_body(qt_ref, k_ref, vt_ref, km_ref, o_ref, acc_ref, qa_ref, s_ref, *, nb):
    qblk = pl.program_id(1)
    blk = MOBA_BLOCK
    hd = ATT_HEAD_DIM
    n_blocks = k_ref.shape[0] // blk
    nbp = (n_blocks + 7) // 8 * 8
    own0 = pl.multiple_of(qblk * blk, blk)
    km = km_ref[...]
    km_hi = km.astype(BF16)
    km_lo = (km - km_hi.astype(F32)).astype(BF16)
    bid = lax.broadcasted_iota(I32, (nbp, blk), 0)
    causal = lax.broadcasted_iota(I32, (blk, blk), 0) <= lax.broadcasted_iota(I32, (blk, blk), 1)

    gates, own_scores = [], []
    for r in range(ATT_REP):
        qt = qt_ref[r]
        gates.append((jnp.dot(km_hi, qt, preferred_element_type=F32)
                      + jnp.dot(km_lo, qt, preferred_element_type=F32))[:nbp])
    for r in range(ATT_REP):
        own_scores.append(jnp.dot(k_ref[pl.ds(own0, blk), :hd], qt_ref[r],
                                  preferred_element_type=F32))
    for r in range(ATT_REP):
        gate = jnp.where(bid < qblk, gates[r], -jnp.inf)
        picked = bid < 0
        for kk in range(MOBA_TOPK):
            mx = jnp.max(gate, axis=0, keepdims=True)
            idx = jnp.min(jnp.where(gate == mx, bid, LANES), axis=0, keepdims=True)
            hit = bid == idx
            picked = picked | (hit & (kk < qblk))
            gate = jnp.where(hit, -jnp.inf, gate)
        bias = jnp.where(picked, 0.0, MASK_BIAS).astype(BF16)
        qa_ref[r, :hd, :] = qt_ref[r]
        qa_ref[r, hd:hd + nbp, :] = bias
        if nbp < LANES:
            qa_ref[r, hd + nbp:, :] = jnp.zeros((LANES - nbp, blk), BF16)

    def blocks_of(jj):
        return [jnp.minimum(jj * nb + u, n_blocks - 1) for u in range(nb)]

    def head_scores(r, blocks):
        return [jnp.dot(k_ref[pl.ds(pl.multiple_of(ja * blk, blk), blk), :], qa_ref[r],
                        preferred_element_type=F32) for ja in blocks]

    ahead = ATT_REP - 1
    for r in range(ahead):
        for u, su in enumerate(head_scores(r, blocks_of(0))):
            s_ref[r * nb + u] = su

    m0 = []
    for r in range(ATT_REP):
        s = jnp.where(causal, own_scores[r], -jnp.inf)
        m = jnp.max(s, axis=0, keepdims=True)
        p = jnp.exp2(s - m)
        m0.append(m)
        acc_ref[r] = jnp.dot(vt_ref[qblk], p.astype(BF16), preferred_element_type=F32)

    def past_blocks(jj, ms):
        blocks = blocks_of(jj)
        blocks_next = blocks_of(jj + 1)
        vts = [vt_ref[ja] for ja in blocks]
        m_out = []
        in_flight = {}
        for r in range(ATT_REP):
            scores = in_flight.pop(r) if r in in_flight else [s_ref[r * nb + u] for u in range(nb)]
            r_req = r + ahead
            if r_req < ATT_REP:
                in_flight[r_req] = head_scores(r_req, blocks)
            else:
                for u, su in enumerate(head_scores(r_req - ATT_REP, blocks_next)):
                    s_ref[(r_req - ATT_REP) * nb + u] = su
            m_new = ms[r]
            for su in scores:
                m_new = jnp.maximum(m_new, jnp.max(su, axis=0, keepdims=True))
            alpha = jnp.exp2(ms[r] - m_new)
            pv = None
            for u in range(nb):
                pu = jnp.exp2(scores[u] - m_new).astype(BF16)
                d = jnp.dot(vts[u], pu, preferred_element_type=F32)
                pv = d if pv is None else pv + d
            acc_ref[r] = alpha * acc_ref[r] + pv
            m_out.append(m_new)
        return tuple(m_out)

    lax.fori_loop(0, (qblk + nb - 1) // nb, past_blocks, tuple(m0))

    for r in range(ATT_REP):
        out = acc_ref[r, :hd, :] / acc_ref[r, hd:hd + 1, :]
        o_ref[:, r * hd:(r + 1) * hd] = out.T.astype(o_ref.dtype)


def attention_call(qt, k, vt, kmean):
    t = k.shape[0]
    tq = MOBA_BLOCK
    n_blocks = t // MOBA_BLOCK
    km = jnp.transpose(kmean.reshape(n_blocks, ATT_KV_HEADS, ATT_HEAD_DIM), (1, 0, 2))
    km = jnp.pad(km, ((0, 0), (0, LANES - n_blocks), (0, 0)))
    return pl.pallas_call(
        functools.partial(_attn_body, nb=ATT_NB),
        grid=(ATT_KV_HEADS, t // tq),
        in_specs=[pl.BlockSpec((ATT_REP, ATT_HEAD_DIM, tq), lambda g, i: (g, 0, i)),
                  pl.BlockSpec((t, ATT_KAUG_W), lambda g, i: (0, g)),
                  pl.BlockSpec((None, n_blocks, ATT_VT_ROWS, MOBA_BLOCK), lambda g, i: (g, 0, 0, 0)),
                  pl.BlockSpec((None, LANES, ATT_HEAD_DIM), lambda g, i: (g, 0, 0))],
        out_specs=pl.BlockSpec((tq, ATT_REP * ATT_HEAD_DIM), lambda g, i: (i, g)),
        out_shape=jax.ShapeDtypeStruct((t, ATT_Q_W), BF16),
        scratch_shapes=[pltpu.VMEM((ATT_REP, ATT_VT_ROWS, tq), F32),
                        pltpu.VMEM((ATT_REP, ATT_KAUG_W, tq), BF16),
                        pltpu.VMEM(((ATT_REP - 1) * ATT_NB, MOBA_BLOCK, tq), F32)],
        compiler_params=_cparams("parallel", "parallel"),
        name="moba_attention",
    )(qt, k, vt, km)


def _router_body(h_ref, g_ref, wr_ref, wrl_ref, br_ref, xn_ref, re_ref, rw_ref, cnt_ref):
    h = h_ref[...]
    ms = jnp.mean(h * h, axis=-1, keepdims=True)
    xn = h * lax.rsqrt(ms + NORM_EPS) * g_ref[...]
    xn_ref[...] = xn
    xh = xn.astype(BF16)
    xl = (xn - xh.astype(F32)).astype(BF16)
    logits = (jnp.dot(xh, wr_ref[...], preferred_element_type=F32)
              + jnp.dot(xh, wrl_ref[...], preferred_element_type=F32)
              + jnp.dot(xl, wr_ref[...], preferred_element_type=F32)) + br_ref[...]
    lane = lax.broadcasted_iota(I32, logits.shape, 1)
    neg = -jnp.inf

    is_g = lane < MOE_GROUPS
    gl = jnp.where(is_g, logits, neg)
    ge = jnp.exp(gl - jnp.max(gl, axis=1, keepdims=True))
    gp = ge / jnp.sum(ge, axis=1, keepdims=True)
    g_w = jnp.max(gp, axis=1, keepdims=True)
    g_idx = jnp.min(jnp.where(is_g & (gp == g_w), lane, LANES), axis=1, keepdims=True)

    e_lo = MOE_GROUPS + MOE_EPG * g_idx
    is_e = (lane >= e_lo) & (lane < e_lo + MOE_EPG)
    el = jnp.where(is_e, logits, neg)
    ee = jnp.exp(el - jnp.max(el, axis=1, keepdims=True))
    ep = jnp.where(is_e, ee / jnp.sum(ee, axis=1, keepdims=True), -1.0)
    p1 = jnp.max(ep, axis=1, keepdims=True)
    i1 = jnp.min(jnp.where(ep == p1, lane, LANES), axis=1, keepdims=True)
    ep2 = jnp.where(lane == i1, -1.0, ep)
    p2 = jnp.max(ep2, axis=1, keepdims=True)
    i2 = jnp.min(jnp.where(ep2 == p2, lane, LANES), axis=1, keepdims=True)
    den = p1 + p2
    w1 = g_w * (p1 / den)
    w2 = g_w * (p2 / den)
    rw_ref[...] = jnp.where(lane == 0, w1, jnp.where(lane == 1, w2, 0.0))

    @pl.when(pl.program_id(0) == 0)
    def _():
        cnt_ref[...] = jnp.zeros_like(cnt_ref)

    e1, e2 = i1 - MOE_GROUPS, i2 - MOE_GROUPS
    oh1 = (lane == e1).astype(F32)
    oh2 = (lane == e2).astype(F32)
    tm = h.shape[0]
    earlier = (lax.broadcasted_iota(I32, (tm, tm), 1) < lax.broadcasted_iota(I32, (tm, tm), 0)).astype(BF16)
    before = jnp.dot(earlier, (oh1 + oh2).astype(BF16), preferred_element_type=F32) + cnt_ref[...]
    r1 = jnp.sum(oh1 * before, axis=1, keepdims=True).astype(I32)
    r2 = jnp.sum(oh2 * (before + oh1), axis=1, keepdims=True).astype(I32)
    cnt_ref[...] = cnt_ref[...] + jnp.sum(oh1 + oh2, axis=0, keepdims=True)
    re_ref[...] = jnp.where(lane == 0, e1, jnp.where(lane == 1, e2, jnp.where(lane == 2, r1,
                                                                               jnp.where(lane == 3, r2, 0))))


def router_call(h, gain, w_group, b_group, w_expert, b_expert):
    t, d = h.shape
    nr = MOE_GROUPS + MOE_EXPERTS
    wr = jnp.zeros((d, LANES), F32).at[:, :MOE_GROUPS].set(w_group).at[:, MOE_GROUPS:nr].set(w_expert)
    br = jnp.zeros((1, LANES), F32).at[0, :MOE_GROUPS].set(b_group).at[0, MOE_GROUPS:nr].set(b_expert)
    wr_hi = wr.astype(BF16)
    wr_lo = (wr - wr_hi.astype(F32)).astype(BF16)
    return pl.pallas_call(
        _router_body,
        grid=(t // NORM_TM,),
        in_specs=[pl.BlockSpec((NORM_TM, d), lambda i: (i, 0)),
                  pl.BlockSpec((1, d), lambda i: (0, 0)),
                  pl.BlockSpec((d, LANES), lambda i: (0, 0)),
                  pl.BlockSpec((d, LANES), lambda i: (0, 0)),
                  pl.BlockSpec((1, LANES), lambda i: (0, 0))],
        out_specs=[pl.BlockSpec((NORM_TM, d), lambda i: (i, 0)),
                   pl.BlockSpec((NORM_TM, LANES), lambda i: (i, 0)),
                   pl.BlockSpec((NORM_TM, LANES), lambda i: (i, 0)),
                   pl.BlockSpec((1, LANES), lambda i: (0, 0))],
        out_shape=[jax.ShapeDtypeStruct((t, d), F32),
                   jax.ShapeDtypeStruct((t, LANES), I32),
                   jax.ShapeDtypeStruct((t, LANES), F32),
                   jax.ShapeDtypeStruct((1, LANES), F32)],
        compiler_params=_cparams("arbitrary"),
        name="norm_router",
    )(h, gain.reshape(1, d), wr_hi, wr_lo, br)


def _row_copy(src_ref, dst_ref, sem, src_row, dst_row):
    return pltpu.make_async_copy(src_ref.at[pl.ds(src_row, 1), :], dst_ref.at[pl.ds(dst_row, 1), :], sem)


def _start_row_gather(idx_ref, src_ref, dst_ref, sem, n):
    for r in range(n):
        _row_copy(src_ref, dst_ref, sem, idx_ref[0, r], r).start(priority=r % 2)


def _wait_row_gather(src_ref, dst_ref, sem, n):
    for r in range(n):
        _row_copy(src_ref, dst_ref, sem, 0, r).wait()


def _ffn_body(be_ref, nv_ref, tok0_ref, tokn_ref, x_hbm, wg_ref, wu_ref, wd_ref, o_ref, xbuf, sems):
    b = pl.program_id(0)
    nv = nv_ref[0]
    last = pl.num_programs(0) - 1
    tm = o_ref.shape[0]
    slot = b % 2
    valid = b < nv

    @pl.when(b == 0)
    def _():
        _start_row_gather(tok0_ref, x_hbm, xbuf.at[0], sems.at[0], tm)

    @pl.when(b <= nv)
    def _():
        _wait_row_gather(x_hbm, xbuf.at[slot], sems.at[slot], tm)

    @pl.when(valid)
    def _():
        _start_row_gather(tokn_ref, x_hbm, xbuf.at[1 - slot], sems.at[1 - slot], tm)
        x = xbuf[slot].astype(BF16)
        gate = jnp.dot(x, wg_ref[...], preferred_element_type=F32)
        up = jnp.dot(x, wu_ref[...], preferred_element_type=F32)
        hid = (_silu(gate) * up).astype(BF16)
        o_ref[...] = jnp.dot(hid, wd_ref[...], preferred_element_type=F32).astype(o_ref.dtype)

    @pl.when(jnp.logical_not(valid))
    def _():
        o_ref[...] = jnp.zeros_like(o_ref)

    @pl.when(valid & (b == last))
    def _():
        _wait_row_gather(x_hbm, xbuf.at[1 - slot], sems.at[1 - slot], tm)


def ffn_call(xn, row_token, blk_expert, n_valid, w_gate, w_up, w_down, layer):
    t, d = xn.shape
    ff = w_gate.shape[3]
    tm = MOE_TM
    n_rows = row_token.shape[0]
    n_blk = n_rows // tm

    def w_map(b, be, nv):
        return (layer, be[jnp.minimum(b, nv[0] - 1)], 0, 0)

    tok_spec = functools.partial(pl.BlockSpec, (None, 1, tm), memory_space=pltpu.SMEM)
    grid_spec = pltpu.PrefetchScalarGridSpec(
        num_scalar_prefetch=2,
        grid=(n_blk,),
        in_specs=[tok_spec(index_map=lambda b, be, nv: (0, 0, 0)),
                  tok_spec(index_map=lambda b, be, nv: (jnp.minimum(b + 1, n_blk - 1), 0, 0)),
                  pl.BlockSpec(memory_space=pl.ANY),
                  pl.BlockSpec((None, None, d, ff), w_map),
                  pl.BlockSpec((None, None, d, ff), w_map),
                  pl.BlockSpec((None, None, ff, d), w_map)],
        out_specs=pl.BlockSpec((tm, d), lambda b, be, nv: (b, 0)),
        scratch_shapes=[pltpu.VMEM((2, tm, d), xn.dtype), pltpu.SemaphoreType.DMA((2,))],
    )
    tok = row_token.reshape(n_blk, 1, tm)
    return pl.pallas_call(
        _ffn_body,
        grid_spec=grid_spec,
        out_shape=jax.ShapeDtypeStruct((n_rows, d), F32),
        compiler_params=_cparams("arbitrary"),
        name="moe_ffn",
    )(blk_expert, n_valid, tok, tok, xn, w_gate, w_up, w_down)


def _combine_body(pos_ref, posn_ref, y_ref, h_ref, rw_ref, g_ref, *out_refs_and_scratch, with_norm):
    if with_norm:
        ho_ref, xn_ref, buf_ref, sems = out_refs_and_scratch
    else:
        ho_ref, buf_ref, sems = out_refs_and_scratch
    tb = ho_ref.shape[0]
    n = MOE_TOPK * tb
    i = pl.program_id(0)
    slot = i % 2

    @pl.when(i == 0)
    def _():
        _start_row_gather(pos_ref, y_ref, buf_ref.at[0], sems.at[0], n)

    _wait_row_gather(y_ref, buf_ref.at[slot], sems.at[slot], n)

    @pl.when(i + 1 < pl.num_programs(0))
    def _():
        _start_row_gather(posn_ref, y_ref, buf_ref.at[1 - slot], sems.at[1 - slot], n)

    rw = rw_ref[...]
    ffn = rw[:, 0:1] * buf_ref[slot, 0:tb, :] + rw[:, 1:2] * buf_ref[slot, tb:2 * tb, :]
    hn = h_ref[...] + ffn
    ho_ref[...] = hn
    if with_norm:
        ms = jnp.mean(hn * hn, axis=-1, keepdims=True)
        xn_ref[...] = (hn * lax.rsqrt(ms + NORM_EPS) * g_ref[...]).astype(xn_ref.dtype)


def combine_call(y, pos, h, route_w, next_gain):
    t, d = h.shape
    tb = NORM_TM // 2
    n_steps = t // tb
    with_norm = next_gain is not None
    gain = (next_gain if with_norm else jnp.ones((d,), F32)).reshape(1, d)
    out_specs = [pl.BlockSpec((tb, d), lambda i: (i, 0))]
    out_shape = [jax.ShapeDtypeStruct((t, d), F32)]
    if with_norm:
        out_specs.append(pl.BlockSpec((tb, d), lambda i: (i, 0)))
        out_shape.append(jax.ShapeDtypeStruct((t, d), BF16))
    pos_spec = functools.partial(pl.BlockSpec, (None, 1, MOE_TOPK * tb), memory_space=pltpu.SMEM)
    outs = pl.pallas_call(
        functools.partial(_combine_body, with_norm=with_norm),
        grid=(n_steps,),
        in_specs=[pos_spec(index_map=lambda i: (i, 0, 0)),
                  pos_spec(index_map=lambda i: (jnp.minimum(i + 1, n_steps - 1), 0, 0)),
                  pl.BlockSpec(memory_space=pl.ANY),
                  pl.BlockSpec((tb, d), lambda i: (i, 0)),
                  pl.BlockSpec((tb, LANES), lambda i: (i, 0)),
                  pl.BlockSpec((1, d), lambda i: (0, 0))],
        out_specs=out_specs,
        out_shape=out_shape,
        scratch_shapes=[pltpu.VMEM((2, MOE_TOPK * tb, d), F32), pltpu.SemaphoreType.DMA((2,))],
        compiler_params=_cparams("arbitrary"),
        name="moe_combine",
    )(pos, pos, y, h, route_w, gain)
    return (outs[0], outs[1]) if with_norm else (outs[0], None)


def _dispatch_tables(route_e, counts, t):
    tm = MOE_TM
    n_assign = MOE_TOPK * t
    expert = route_e[:, :MOE_TOPK].reshape(n_assign)
    rank = route_e[:, MOE_TOPK:2 * MOE_TOPK].reshape(n_assign)
    counts = counts[0, :MOE_EXPERTS].astype(I32)
    padded = (counts + tm - 1) // tm * tm
    pend = jnp.cumsum(padded)
    pstart = pend - padded
    onehot = expert[:, None] == jnp.arange(MOE_EXPERTS, dtype=I32)[None, :]
    dest = jnp.sum(jnp.where(onehot, pstart[None, :], 0), axis=1) + rank
    n_rows = n_assign + MOE_EXPERTS * tm
    token = jnp.arange(n_assign, dtype=I32) // MOE_TOPK
    row_token = jnp.zeros((n_rows,), I32).at[dest].set(token)
    n_blk = n_rows // tm
    blk_start = jnp.arange(n_blk, dtype=I32) * tm
    blk_expert = jnp.minimum(jnp.sum((pend[None, :] <= blk_start[:, None]).astype(I32), axis=1),
                             MOE_EXPERTS - 1)
    n_valid = (pend[-1] // tm).astype(I32).reshape(1)
    tb = NORM_TM // 2
    pos = dest.reshape(t // tb, tb, MOE_TOPK).transpose(0, 2, 1).reshape(t // tb, 1, MOE_TOPK * tb)
    return row_token, blk_expert, n_valid, pos.astype(I32)


def moe_layer(h, gain, w_group, b_group, w_expert, b_expert, expert_w16, next_gain):
    t, d = h.shape
    xn, route_e, route_w, counts = router_call(h, gain, w_group, b_group, w_expert, b_expert)
    row_token, blk_expert, n_valid, pos = _dispatch_tables(route_e, counts, t)
    g16, u16, d16 = expert_w16
    y = ffn_call(xn, row_token, blk_expert, n_valid, g16.reshape(1, MOE_EXPERTS, d, -1),
                 u16.reshape(1, MOE_EXPERTS, d, -1), d16.reshape(1, MOE_EXPERTS, -1, d), 0)
    return combine_call(y, pos, h, route_w, next_gain)


def hybrid_layer(h, xn, w_in16, j, conv_w, conv_b, dt_bias, a_log, d_skip, ssd_norm, ln_g, ln_b, ws, bs,
                 w_out16, expert_w, layer):
    wg, wu, wd = expert_w
    hp = SSD_HEADS_PER_GROUP
    w_dt = jnp.zeros((D_MODEL, SSD_GROUPS, LANES), BF16).at[:, :, :hp].set(
        w_in16[j, :, OFF_DT:OFF_U].reshape(D_MODEL, SSD_GROUPS, hp)).reshape(D_MODEL, SSD_GROUPS * LANES)
    proj_a, g16 = matmul_call([xn], w_in16, out_dtype=BF16, n_cols=OFF_DT, layer=j,
                              side_casts=[(wg, layer)], name="hyb_in_proj_ssd")
    proj_b, u16 = matmul_call([xn], w_in16[j, :, OFF_U:], out_dtype=BF16, side_casts=[(wu, layer)],
                              name="hyb_in_proj_gmlp")
    dt_raw = matmul_call([xn], w_dt, out_dtype=F32, tn=SSD_GROUPS * LANES, name="hyb_dt_proj")
    dtt_raw = dt_raw.reshape(-1, SSD_GROUPS, LANES)[:, :, :hp].reshape(-1, SSD_HEADS).T
    y_a = ssd_call(proj_a, dt_raw, dtt_raw, conv_w, conv_b, dt_bias, a_log, d_skip, ssd_norm)
    y_b = gmlp_call(proj_b, ln_g, ln_b, ws, bs)
    h, d16 = matmul_call([y_a, y_b], w_out16, res=h, layer=j, side_casts=[(wd, layer)], name="hyb_out_proj")
    return h, (g16, u16, d16)


def moba_layer(h, xn, w_qkv16, j, q_norm, k_norm, w_out16, cosf, sinf, expert_w, layer):
    wg, wu, wd = expert_w
    qkv, g16, u16 = matmul_call([xn], w_qkv16, out_dtype=BF16, layer=j,
                                side_casts=[(wg, layer), (wu, layer)], name="att_qkv_proj")
    qt, k, vt, kmean = qk_prep_call(qkv, q_norm, k_norm, cosf, sinf)
    o = attention_call(qt, k, vt, kmean)
    h, d16 = matmul_call([o], w_out16, res=h, layer=j, side_casts=[(wd, layer)], name="att_out_proj")
    return h, (g16, u16, d16)


def _rope_tables(t):
    inv = 1.0 / (ROPE_THETA ** (jnp.arange(0, ATT_HEAD_DIM, 2, dtype=F32) / ATT_HEAD_DIM))
    ang = jnp.arange(t, dtype=F32)[:, None] * inv[None, :]
    cos, sin = jnp.cos(ang), jnp.sin(ang)
    return jnp.concatenate([cos, cos], axis=1), jnp.concatenate([-sin, sin], axis=1)


def kernel(x, norm_mix, norm_ffn, hyb_w_in, ssd_conv_w, ssd_conv_b, ssd_dt_bias, ssd_a_log, ssd_d, ssd_norm, gm_ln_g, gm_ln_b, gm_ws, gm_bs, hyb_w_out, att_w_qkv, att_q_norm, att_k_norm, att_w_out, moe_w_group, moe_b_group, moe_w_expert, moe_b_expert, moe_w_gate, moe_w_up, moe_w_down):
    bsz, t, d = x.shape
    depth = norm_mix.shape[0]
    cosf, sinf = _rope_tables(t)
    hyb_w_in16, hyb_w_out16 = hyb_w_in.astype(BF16), hyb_w_out.astype(BF16)
    att_w_qkv16, att_w_out16 = att_w_qkv.astype(BF16), att_w_out.astype(BF16)
    ff = moe_w_gate.shape[-1]
    expert_w = (moe_w_gate.reshape(depth, MOE_EXPERTS * d, ff), moe_w_up.reshape(depth, MOE_EXPERTS * d, ff),
                moe_w_down.reshape(depth, MOE_EXPERTS * ff, d))
    outs = []
    for b in range(bsz):
        h = x[b]
        xn = rmsnorm_call(h, norm_mix[0])
        for layer in range(depth):
            j = layer // 2
            if layer % 2 == 0:
                h, expert_w16 = hybrid_layer(h, xn, hyb_w_in16, j, ssd_conv_w[j], ssd_conv_b[j],
                                             ssd_dt_bias[j], ssd_a_log[j], ssd_d[j], ssd_norm[j], gm_ln_g[j],
                                             gm_ln_b[j], gm_ws[j], gm_bs[j], hyb_w_out16, expert_w, layer)
            else:
                h, expert_w16 = moba_layer(h, xn, att_w_qkv16, j, att_q_norm[j], att_k_norm[j], att_w_out16,
                                           cosf, sinf, expert_w, layer)
            next_gain = norm_mix[layer + 1] if layer + 1 < depth else None
            h, xn = moe_layer(h, norm_ffn[layer], moe_w_group[layer], moe_b_group[layer],
                              moe_w_expert[layer], moe_b_expert[layer], expert_w16, next_gain)
        outs.append(h)
    return jnp.stack(outs, axis=0)
```

```python
import functools

import jax
import jax.numpy as jnp
from jax import lax
from jax.experimental import pallas as pl
from jax.experimental.pallas import tpu as pltpu

F32 = jnp.float32
BF16 = jnp.bfloat16
I32 = jnp.int32

D_MODEL = 4096
NORM_EPS = 1e-6

SSD_INNER = 2048
SSD_HEAD_DIM = 64
SSD_HEADS = 32
SSD_GROUPS = 4
SSD_STATE = 128
SSD_CONV = 4
SSD_CHUNK = 256
SSD_BC = SSD_GROUPS * SSD_STATE
SSD_CONV_CH = SSD_INNER + 2 * SSD_BC
SSD_GROUP_W = SSD_INNER // SSD_GROUPS
SSD_HEADS_PER_GROUP = SSD_HEADS // SSD_GROUPS

GM_INNER = 2048
GM_CHUNK = 128
GM_GROUPS = 16
GM_GROUP_DIM = 128

OFF_XBC = SSD_INNER
OFF_DT = OFF_XBC + SSD_CONV_CH
OFF_U = OFF_DT + SSD_HEADS
OFF_V = OFF_U + GM_INNER

ATT_HEAD_DIM = 128
ATT_HEADS = 32
ATT_KV_HEADS = 8
ATT_REP = ATT_HEADS // ATT_KV_HEADS
ATT_Q_W = ATT_HEADS * ATT_HEAD_DIM
ATT_KV_W = ATT_KV_HEADS * ATT_HEAD_DIM
ROPE_THETA = 10000.0
MOBA_BLOCK = 256
MOBA_TOPK = 3

MOE_GROUPS = 4
MOE_EPG = 4
MOE_EXPERTS = 16
MOE_TOPK = 2
MOE_FF = 768

LANES = 128
VMEM_LIMIT_BYTES = 56 * 1024 * 1024

MM_TM = 512
MM_TN = 1024
NORM_TM = 256
ATT_KAUG_W = ATT_HEAD_DIM + LANES
ATT_VT_ROWS = ATT_HEAD_DIM + 16
MASK_BIAS = -1e30
ATT_NB = 2
QK_SCALE_LOG2E = ATT_HEAD_DIM ** -0.5 * 1.4426950408889634
MOE_TM = 256


def _cparams(*sem, flags=None):
    return pltpu.CompilerParams(dimension_semantics=sem, vmem_limit_bytes=VMEM_LIMIT_BYTES, flags=flags)


def _silu(x):
    return x / (1.0 + jnp.exp(-x))


def _softplus(x):
    return jnp.maximum(x, 0.0) + jnp.log1p(jnp.exp(-jnp.abs(x)))


def _gelu_erf(x):
    return 0.5 * x * (1.0 + lax.erf(x * (2.0 ** -0.5)))


def _pack_bf16_pairs(x):
    c = x.shape[1] // 2
    bits = pltpu.bitcast(x.astype(BF16).astype(F32), jnp.uint32)
    return (bits[:, :c] >> 16) | bits[:, c:]


def _unpack_bf16_pairs(w):
    lo = pltpu.bitcast(w << 16, F32)
    hi = pltpu.bitcast(w & jnp.uint32(0xFFFF0000), F32)
    return jnp.concatenate([lo, hi], axis=1).astype(BF16)


def _dot_f32_by_01(x, m01, f32_on_right=False):
    hi = x.astype(BF16)
    r1 = x - hi.astype(F32)
    mid = r1.astype(BF16)
    lo = (r1 - mid.astype(F32)).astype(BF16)
    if f32_on_right:
        return sum(jnp.dot(m01, t, preferred_element_type=F32) for t in (hi, mid, lo))
    return sum(jnp.dot(t, m01, preferred_element_type=F32) for t in (hi, mid, lo))


def _rmsnorm_body(x_ref, g_ref, o_ref):
    x = x_ref[...]
    ms = jnp.mean(x * x, axis=-1, keepdims=True)
    o_ref[...] = (x * lax.rsqrt(ms + NORM_EPS) * g_ref[...]).astype(o_ref.dtype)


def rmsnorm_call(x, g):
    t, d = x.shape
    return pl.pallas_call(
        _rmsnorm_body,
        grid=(t // NORM_TM,),
        in_specs=[pl.BlockSpec((NORM_TM, d), lambda i: (i, 0)),
                  pl.BlockSpec((1, d), lambda i: (0, 0))],
        out_specs=pl.BlockSpec((NORM_TM, d), lambda i: (i, 0)),
        out_shape=jax.ShapeDtypeStruct((t, d), BF16),
        compiler_params=_cparams("parallel"),
        name="rmsnorm",
    )(x, g.reshape(1, d))


def _matmul_body(*refs, n_parts, has_res, n_side):
    a_refs = refs[:n_parts]
    w_refs = refs[n_parts:2 * n_parts]
    n_in = 2 * n_parts + int(has_res) + n_side
    o_ref = refs[n_in]
    acc = jnp.dot(a_refs[0][...], w_refs[0][...], preferred_element_type=F32)
    for p in range(1, n_parts):
        acc = acc + jnp.dot(a_refs[p][...], w_refs[p][...], preferred_element_type=F32)
    if has_res:
        acc = acc + refs[2 * n_parts][...]
    o_ref[...] = acc.astype(o_ref.dtype)
    _side_cast(refs[n_in - n_side:n_in], refs[n_in + 1:n_in + 1 + n_side])


def _side_blocks(rows, n_steps):
    for nsb in range(min(n_steps, rows // 16), 0, -1):
        if rows % nsb == 0 and (rows // nsb) % 16 == 0:
            return nsb
    return 1


def _side_cast_plumbing(side_casts, n_steps, step_of):
    in_specs, out_specs, out_shapes, args = [], [], [], []
    for src, idx in side_casts:
        _, rows, cols = src.shape
        nsb = _side_blocks(rows, n_steps)

        def slab(*g, nsb=nsb):
            return jnp.minimum(step_of(*g), nsb - 1)

        in_specs.append(pl.BlockSpec((None, rows // nsb, cols),
                                     lambda *g, slab=slab, idx=idx: (idx, slab(*g), 0)))
        out_specs.append(pl.BlockSpec((rows // nsb, cols), lambda *g, slab=slab: (slab(*g), 0)))
        out_shapes.append(jax.ShapeDtypeStruct((rows, cols), BF16))
        args.append(src)
    return in_specs, out_specs, out_shapes, args


def _side_cast(in_refs, out_refs):
    for src, dst in zip(in_refs, out_refs):
        dst[...] = src[...].astype(BF16)


def matmul_call(a_parts, w, res=None, out_dtype=F32, tn=MM_TN, n_cols=None, layer=None, side_casts=(),
                name="matmul"):
    n_parts = len(a_parts)
    m, kp = a_parts[0].shape
    n = w.shape[-1] if n_cols is None else n_cols
    tm = MM_TM
    steps_i = m // tm
    n_steps = (n // tn) * steps_i
    in_specs = [pl.BlockSpec((tm, kp), lambda j, i: (i, 0)) for _ in a_parts]
    if layer is None:
        in_specs += [pl.BlockSpec((kp, tn), functools.partial(lambda j, i, p: (p, j), p=p))
                     for p in range(n_parts)]
    else:
        in_specs += [pl.BlockSpec((None, kp, tn), functools.partial(lambda j, i, p: (layer, p, j), p=p))
                     for p in range(n_parts)]
    args = list(a_parts) + [w] * n_parts
    if res is not None:
        in_specs.append(pl.BlockSpec((tm, tn), lambda j, i: (i, j)))
        args.append(res)
    out_specs = [pl.BlockSpec((tm, tn), lambda j, i: (i, j))]
    out_shape = [jax.ShapeDtypeStruct((m, n), out_dtype)]
    s_in, s_out, s_shapes, s_args = _side_cast_plumbing(side_casts, n_steps, lambda j, i: j * steps_i + i)
    in_specs += s_in
    out_specs += s_out
    out_shape += s_shapes
    args += s_args
    outs = pl.pallas_call(
        functools.partial(_matmul_body, n_parts=n_parts, has_res=res is not None, n_side=len(side_casts)),
        grid=(n // tn, m // tm),
        in_specs=in_specs,
        out_specs=out_specs,
        out_shape=out_shape,
        compiler_params=_cparams("arbitrary", "arbitrary"),
        name=name,
    )(*args)
    return tuple(outs) if side_casts else outs[0]


def _ssd_body(z_ref, xs_ref, b_ref, c_ref, dt_ref, dtt_ref,
              wx_ref, bx_ref, wb_ref, bb_ref, wc_ref, bc_ref,
              dtb_ref, dtbt_ref, alog_ref, alogt_ref, dskip_ref, ng_ref, expand_ref, *rest, n_side):
    o_ref = rest[n_side]
    state_ref, tx_ref, tb_ref, tc_ref = rest[2 * n_side + 1:]
    _side_cast(rest[:n_side], rest[n_side + 1:2 * n_side + 1])
    c = pl.program_id(1)
    L = SSD_CHUNK
    HP = SSD_HEADS_PER_GROUP

    @pl.when(c == 0)
    def _():
        state_ref[...] = jnp.zeros_like(state_ref)
        tx_ref[...] = jnp.zeros_like(tx_ref)
        tb_ref[...] = jnp.zeros_like(tb_ref)
        tc_ref[...] = jnp.zeros_like(tc_ref)

    def conv_silu(cur_ref, tail_ref, w_ref, bias_ref):
        cur = cur_ref[...].astype(F32)
        ext = jnp.concatenate([tail_ref[...], cur], axis=0)
        w = w_ref[...]
        acc = bias_ref[...] + w[SSD_CONV - 1:SSD_CONV, :] * cur
        for j in range(1, SSD_CONV):
            acc = acc + w[SSD_CONV - 1 - j:SSD_CONV - j, :] * ext[8 - j:8 - j + L, :]
        tail_ref[...] = cur[L - 8:L, :]
        return _silu(acc)

    xs = conv_silu(xs_ref, tx_ref, wx_ref, bx_ref)
    bm = conv_silu(b_ref, tb_ref, wb_ref, bb_ref)
    cm = conv_silu(c_ref, tc_ref, wc_ref, bc_ref)

    dt = _softplus(dt_ref[...] + dtb_ref[...])
    da = dt * (-jnp.exp(alog_ref[...]))
    dtt = _softplus(dtt_ref[...] + dtbt_ref[...])
    dat = dtt * (-jnp.exp(alogt_ref[...]))

    row = lax.broadcasted_iota(I32, (L, L), 0)
    col = lax.broadcasted_iota(I32, (L, L), 1)
    causal = row >= col
    cs = _dot_f32_by_01(da, causal.astype(BF16), f32_on_right=True)
    cst = _dot_f32_by_01(dat, (row <= col).astype(BF16))
    cs_last = cs[L - 1:L, :]
    to_end = jnp.exp(cs_last - cs)
    ecs = jnp.exp(cs)

    expand = expand_ref[...]
    dt_e = _dot_f32_by_01(dt, expand)
    to_end_e = _dot_f32_by_01(to_end, expand)
    ecs_e = _dot_f32_by_01(ecs, expand)

    x = xs * dt_e
    bm16 = bm.astype(BF16)
    cm16 = cm.astype(BF16)
    cb = lax.dot_general(cm16, bm16, (((1,), (1,)), ((), ())), preferred_element_type=F32)

    prev = state_ref[...]
    y = jnp.dot(cm16, prev.astype(BF16), preferred_element_type=F32) * ecs_e

    x16 = x.astype(BF16)
    heads_per_tile = LANES // SSD_HEAD_DIM
    first_head = lax.broadcasted_iota(I32, (L, LANES), 1) < SSD_HEAD_DIM
    tiles = []
    for tt in range(SSD_GROUP_W // LANES):
        x_tile = x16[:, tt * LANES:(tt + 1) * LANES]
        y_tile = None
        for hh in range(heads_per_tile):
            h = tt * heads_per_tile + hh
            seg = cs[:, h:h + 1] - cst[h:h + 1, :]
            decay = jnp.exp(jnp.where(causal, seg, -jnp.inf))
            m16 = (cb * decay).astype(BF16)
            xh = jnp.where(first_head == (hh == 0), x_tile, jnp.zeros((), BF16))
            d = jnp.dot(m16, xh, preferred_element_type=F32)
            y_tile = d if y_tile is None else y_tile + d
        tiles.append(y_tile)
    y = y + jnp.concatenate(tiles, axis=1)

    xw = (x * to_end_e).astype(BF16)
    state_ref[...] = prev * ecs_e[L - 1:L, :] + jnp.dot(bm.T.astype(BF16), xw,
                                                        preferred_element_type=F32)

    y = y + xs * dskip_ref[...]
    y = y * _silu(z_ref[...].astype(F32))
    ms = jnp.mean(y * y, axis=-1, keepdims=True)
    o_ref[...] = (y * lax.rsqrt(ms + NORM_EPS) * ng_ref[...]).astype(o_ref.dtype)


def ssd_call(proj, dt_raw, dtt_raw, conv_w, conv_b, dt_bias, a_log, d_skip, norm_g, side_casts=()):
    t = proj.shape[0]
    L = SSD_CHUNK
    gw = SSD_GROUP_W
    xs0 = SSD_INNER // gw
    b0 = 2 * SSD_INNER // SSD_STATE
    c0 = b0 + SSD_GROUPS
    hp = SSD_HEADS_PER_GROUP

    dtb = jnp.zeros((SSD_GROUPS, LANES), F32).at[:, :hp].set(dt_bias.reshape(SSD_GROUPS, hp))
    alog = jnp.zeros((SSD_GROUPS, LANES), F32).at[:, :hp].set(a_log.reshape(SSD_GROUPS, hp))
    dtb = dtb.reshape(1, SSD_GROUPS * LANES)
    alog = alog.reshape(1, SSD_GROUPS * LANES)
    dtbt = dt_bias.reshape(SSD_HEADS, 1)
    alogt = a_log.reshape(SSD_HEADS, 1)
    dskip = jnp.repeat(d_skip, SSD_HEAD_DIM).reshape(1, SSD_INNER)
    expand = (jnp.arange(LANES)[:, None] == (jnp.arange(gw)[None, :] // SSD_HEAD_DIM)).astype(BF16)
    cw = conv_w.astype(F32)
    cbias = conv_b.reshape(1, SSD_CONV_CH).astype(F32)
    xb0 = SSD_INNER // SSD_STATE
    cb0 = xb0 + SSD_GROUPS

    in_specs = [
        pl.BlockSpec((L, gw), lambda g, c: (c, g)),
        pl.BlockSpec((L, gw), lambda g, c: (c, xs0 + g)),
        pl.BlockSpec((L, SSD_STATE), lambda g, c: (c, b0 + g)),
        pl.BlockSpec((L, SSD_STATE), lambda g, c: (c, c0 + g)),
        pl.BlockSpec((L, LANES), lambda g, c: (c, g)),
        pl.BlockSpec((hp, L), lambda g, c: (g, c)),
        pl.BlockSpec((SSD_CONV, gw), lambda g, c: (0, g)),
        pl.BlockSpec((1, gw), lambda g, c: (0, g)),
        pl.BlockSpec((SSD_CONV, SSD_STATE), lambda g, c: (0, xb0 + g)),
        pl.BlockSpec((1, SSD_STATE), lambda g, c: (0, xb0 + g)),
        pl.BlockSpec((SSD_CONV, SSD_STATE), lambda g, c: (0, cb0 + g)),
        pl.BlockSpec((1, SSD_STATE), lambda g, c: (0, cb0 + g)),
        pl.BlockSpec((1, LANES), lambda g, c: (0, g)),
        pl.BlockSpec((hp, 1), lambda g, c: (g, 0)),
        pl.BlockSpec((1, LANES), lambda g, c: (0, g)),
        pl.BlockSpec((hp, 1), lambda g, c: (g, 0)),
        pl.BlockSpec((1, gw), lambda g, c: (0, g)),
        pl.BlockSpec((1, gw), lambda g, c: (0, g)),
        pl.BlockSpec((LANES, gw), lambda g, c: (0, 0)),
    ]
    n_chunks = t // L
    s_in, s_out, s_shapes, s_args = _side_cast_plumbing(side_casts, SSD_GROUPS * n_chunks,
                                                        lambda g, c: g * n_chunks + c)
    outs = pl.pallas_call(
        functools.partial(_ssd_body, n_side=len(side_casts)),
        grid=(SSD_GROUPS, n_chunks),
        in_specs=in_specs + s_in,
        out_specs=[pl.BlockSpec((L, gw), lambda g, c: (c, g))] + s_out,
        out_shape=[jax.ShapeDtypeStruct((t, SSD_INNER), BF16)] + s_shapes,
        scratch_shapes=[pltpu.VMEM((SSD_STATE, gw), F32),
                        pltpu.VMEM((8, gw), F32),
                        pltpu.VMEM((8, SSD_STATE), F32),
                        pltpu.VMEM((8, SSD_STATE), F32)],
        compiler_params=_cparams("arbitrary", "arbitrary"),
        name="ssd",
    )(proj, proj, proj, proj, dt_raw, dtt_raw, cw, cbias, cw, cbias, cw, cbias,
      dtb, dtbt, alog, alogt, dskip, norm_g.reshape(1, SSD_INNER), expand, *s_args)
    return tuple(outs) if side_casts else outs[0]


def _gmlp_body(u_ref, v_ref, g_ref, b_ref, ws_ref, bst_ref, o_ref):
    u = _gelu_erf(u_ref[...].astype(F32))
    v = _gelu_erf(v_ref[...].astype(F32))
    mu = jnp.mean(v, axis=-1, keepdims=True)
    vc = v - mu
    var = jnp.mean(vc * vc, axis=-1, keepdims=True)
    vn = (vc * lax.rsqrt(var + NORM_EPS) * g_ref[...] + b_ref[...]).astype(BF16)
    row = lax.broadcasted_iota(I32, (GM_CHUNK, GM_CHUNK), 0)
    col = lax.broadcasted_iota(I32, (GM_CHUNK, GM_CHUNK), 1)
    causal = row >= col
    for g in range(GM_GROUPS):
        sl = slice(g * GM_GROUP_DIM, (g + 1) * GM_GROUP_DIM)
        w = jnp.where(causal, ws_ref[g], jnp.zeros((), BF16))
        sv = jnp.dot(w, vn[:, sl], preferred_element_type=F32) + bst_ref[:, sl]
        o_ref[:, sl] = (u[:, sl] * sv).astype(o_ref.dtype)


def gmlp_call(proj, ln_g, ln_b, ws, bs):
    t = proj.shape[0]
    ucol = 0
    bst =jnp.repeat(bs.T, GM_GROUP_DIM, axis=1)
    return pl.pallas_call(
        _gmlp_body,
        grid=(t // GM_CHUNK,),
        in_specs=[pl.BlockSpec((GM_CHUNK, GM_INNER), lambda i: (i, ucol)),
                  pl.BlockSpec((GM_CHUNK, GM_INNER), lambda i: (i, ucol + 1)),
                  pl.BlockSpec((1, GM_INNER), lambda i: (0, 0)),
                  pl.BlockSpec((1, GM_INNER), lambda i: (0, 0)),
                  pl.BlockSpec((GM_GROUPS, GM_CHUNK, GM_CHUNK), lambda i: (0, 0, 0)),
                  pl.BlockSpec((GM_CHUNK, GM_INNER), lambda i: (0, 0))],
        out_specs=pl.BlockSpec((GM_CHUNK, GM_INNER), lambda i: (i, 0)),
        out_shape=jax.ShapeDtypeStruct((t, GM_INNER), BF16),
        compiler_params=_cparams("parallel"),
        name="gmlp",
    )(proj, proj, ln_g.reshape(1, GM_INNER), ln_b.reshape(1, GM_INNER), ws.astype(BF16), bst)


def _rope_norm(x, gain, cosf, sinf):
    ms = jnp.mean(x * x, axis=-1, keepdims=True)
    xn = x * lax.rsqrt(ms + NORM_EPS) * gain
    return xn * cosf + pltpu.roll(xn, ATT_HEAD_DIM // 2, axis=1) * sinf


def _qprep_body(x_ref, g_ref, cos_ref, sin_ref, o_ref):
    for r in range(ATT_REP):
        x = x_ref[:, r * ATT_HEAD_DIM:(r + 1) * ATT_HEAD_DIM].astype(F32)
        y = _rope_norm(x, g_ref[...], cos_ref[...], sin_ref[...]) * QK_SCALE_LOG2E
        o_ref[r] = y.T.astype(o_ref.dtype)


def _kprep_body(x_ref, v_ref, g_ref, cos_ref, sin_ref, o_ref, vt_ref, mean_ref):
    y = _rope_norm(x_ref[...].astype(F32), g_ref[...], cos_ref[...], sin_ref[...])
    tb = y.shape[0]
    lane = lax.broadcasted_iota(I32, (tb, LANES), 1)
    o_ref[:, :ATT_HEAD_DIM] = y.astype(o_ref.dtype)
    o_ref[:, ATT_HEAD_DIM:] = jnp.where(lane == pl.program_id(0), 1.0, 0.0).astype(o_ref.dtype)
    mean_ref[...] = jnp.mean(y, axis=0, keepdims=True)
    vt_ref[:ATT_HEAD_DIM, :] = v_ref[...].astype(F32).T.astype(vt_ref.dtype)
    vt_ref[ATT_HEAD_DIM:, :] = jnp.ones((ATT_VT_ROWS - ATT_HEAD_DIM, tb), vt_ref.dtype)


def qk_prep_call(qkv, q_norm, k_norm, cosf, sinf):
    t = qkv.shape[0]
    tb = MOBA_BLOCK
    gw = ATT_REP * ATT_HEAD_DIM
    common = [pl.BlockSpec((1, ATT_HEAD_DIM), lambda i, h: (0, 0)),
              pl.BlockSpec((tb, ATT_HEAD_DIM), lambda i, h: (i, 0)),
              pl.BlockSpec((tb, ATT_HEAD_DIM), lambda i, h: (i, 0))]
    qt = pl.pallas_call(
        _qprep_body,
        grid=(t // tb, ATT_KV_HEADS),
        in_specs=[pl.BlockSpec((tb, gw), lambda i, h: (i, h))] + common,
        out_specs=pl.BlockSpec((ATT_REP, ATT_HEAD_DIM, tb), lambda i, h: (h, 0, i)),
        out_shape=jax.ShapeDtypeStruct((ATT_HEADS, ATT_HEAD_DIM, t), BF16),
        compiler_params=_cparams("parallel", "parallel"),
        name="q_prep",
    )(qkv, q_norm.reshape(1, ATT_HEAD_DIM), cosf, sinf)
    vcol = (ATT_Q_W + ATT_KV_W) // ATT_HEAD_DIM
    k, vt, kmean = pl.pallas_call(
        _kprep_body,
        grid=(t // tb, ATT_KV_HEADS),
        in_specs=[pl.BlockSpec((tb, ATT_HEAD_DIM), lambda i, h: (i, ATT_HEADS + h)),
                  pl.BlockSpec((tb, ATT_HEAD_DIM), lambda i, h: (i, vcol + h))] + common,
        out_specs=[pl.BlockSpec((tb, ATT_KAUG_W), lambda i, h: (i, h)),
                   pl.BlockSpec((None, None, ATT_VT_ROWS, tb), lambda i, h: (h, i, 0, 0)),
                   pl.BlockSpec((None, None, 1, ATT_HEAD_DIM), lambda i, h: (i, h, 0, 0))],
        out_shape=[jax.ShapeDtypeStruct((t, ATT_KV_HEADS * ATT_KAUG_W), BF16),
                   jax.ShapeDtypeStruct((ATT_KV_HEADS, t // tb, ATT_VT_ROWS, tb), BF16),
                   jax.ShapeDtypeStruct((t // tb, ATT_KV_HEADS, 1, ATT_HEAD_DIM), F32)],
        compiler_params=_cparams("parallel", "parallel"),
        name="k_prep",
    )(qkv, qkv, k_norm.reshape(1, ATT_HEAD_DIM), cosf, sinf)
    return qt, k, vt, kmean


def _attn_body(qt_ref, k_ref, vt_ref, km_ref, *rest, nb, n_side):
    o_ref = rest[n_side]
    acc_ref, qa_ref, s_ref = rest[2 * n_side + 1:]
    _side_cast(rest[:n_side], rest[n_side + 1:2 * n_side + 1])
    qblk = pl.program_id(1)
    blk = MOBA_BLOCK
    hd = ATT_HEAD_DIM
    n_blocks = k_ref.shape[0] // blk
    nbp = (n_blocks + 7) // 8 * 8
    own0 = pl.multiple_of(qblk * blk, blk)
    km = km_ref[...]
    km_hi = km.astype(BF16)
    km_lo = (km - km_hi.astype(F32)).astype(BF16)
    bid = lax.broadcasted_iota(I32, (nbp, blk), 0)
    causal = lax.broadcasted_iota(I32, (blk, blk), 0) <= lax.broadcasted_iota(I32, (blk, blk), 1)

    gates, own_scores = [], []
    for r in range(ATT_REP):
        qt = qt_ref[r]
        gates.append((jnp.dot(km_hi, qt, preferred_element_type=F32)
                      + jnp.dot(km_lo, qt, preferred_element_type=F32))[:nbp])
    for r in range(ATT_REP):
        own_scores.append(jnp.dot(k_ref[pl.ds(own0, blk), :hd], qt_ref[r],
                                  preferred_element_type=F32))
    for r in range(ATT_REP):
        gate = jnp.where(bid < qblk, gates[r], -jnp.inf)
        picked = bid < 0
        for kk in range(MOBA_TOPK):
            mx = jnp.max(gate, axis=0, keepdims=True)
            idx = jnp.min(jnp.where(gate == mx, bid, LANES), axis=0, keepdims=True)
            hit = bid == idx
            picked = picked | (hit & (kk < qblk))
            gate = jnp.where(hit, -jnp.inf, gate)
        bias = jnp.where(picked, 0.0, MASK_BIAS).astype(BF16)
        qa_ref[r, :hd, :] = qt_ref[r]
        qa_ref[r, hd:hd + nbp, :] = bias
        if nbp < LANES:
            qa_ref[r, hd + nbp:, :] = jnp.zeros((LANES - nbp, blk), BF16)

    def blocks_of(jj):
        return [jnp.minimum(jj * nb + u, n_blocks - 1) for u in range(nb)]

    def head_scores(r, blocks):
        return [jnp.dot(k_ref[pl.ds(pl.multiple_of(ja * blk, blk), blk), :], qa_ref[r],
                        preferred_element_type=F32) for ja in blocks]

    ahead = ATT_REP - 1
    for r in range(ahead):
        for u, su in enumerate(head_scores(r, blocks_of(0))):
            s_ref[r * nb + u] = su

    m0 = []
    for r in range(ATT_REP):
        s = jnp.where(causal, own_scores[r], -jnp.inf)
        m = jnp.max(s, axis=0, keepdims=True)
        p = jnp.exp2(s - m)
        m0.append(m)
        acc_ref[r] = jnp.dot(vt_ref[qblk], p.astype(BF16), preferred_element_type=F32)

    def past_blocks(jj, ms):
        blocks = blocks_of(jj)
        blocks_next = blocks_of(jj + 1)
        vts = [vt_ref[ja] for ja in blocks]
        m_out = []
        in_flight = {}
        for r in range(ATT_REP):
            scores = in_flight.pop(r) if r in in_flight else [s_ref[r * nb + u] for u in range(nb)]
            r_req = r + ahead
            if r_req < ATT_REP:
                in_flight[r_req] = head_scores(r_req, blocks)
            else:
                for u, su in enumerate(head_scores(r_req - ATT_REP, blocks_next)):
                    s_ref[(r_req - ATT_REP) * nb + u] = su
            m_new = ms[r]
            for su in scores:
                m_new = jnp.maximum(m_new, jnp.max(su, axis=0, keepdims=True))
            alpha = jnp.exp2(ms[r] - m_new)
            pv = None
            for u in range(nb):
                pu = jnp.exp2(scores[u] - m_new).astype(BF16)
                d = jnp.dot(vts[u], pu, preferred_element_type=F32)
                pv = d if pv is None else pv + d
            acc_ref[r] = alpha * acc_ref[r] + pv
            m_out.append(m_new)
        return tuple(m_out)

    lax.fori_loop(0, (qblk + nb - 1) // nb, past_blocks, tuple(m0))

    for r in range(ATT_REP):
        out = acc_ref[r, :hd, :] / acc_ref[r, hd:hd + 1, :]
        o_ref[:, r * hd:(r + 1) * hd] = out.T.astype(o_ref.dtype)


def attention_call(qt, k, vt, kmean, side_casts=()):
    t = k.shape[0]
    tq = MOBA_BLOCK
    n_blocks = t // MOBA_BLOCK
    n_tiles = t // tq
    km = jnp.transpose(kmean.reshape(n_blocks, ATT_KV_HEADS, ATT_HEAD_DIM), (1, 0, 2))
    km = jnp.pad(km, ((0, 0), (0, LANES - n_blocks), (0, 0)))
    s_in, s_out, s_shapes, s_args = _side_cast_plumbing(side_casts, ATT_KV_HEADS * n_tiles,
                                                        lambda g, i: g * n_tiles + i)
    outs = pl.pallas_call(
        functools.partial(_attn_body, nb=ATT_NB, n_side=len(side_casts)),
        grid=(ATT_KV_HEADS, n_tiles),
        in_specs=[pl.BlockSpec((ATT_REP, ATT_HEAD_DIM, tq), lambda g, i: (g, 0, i)),
                  pl.BlockSpec((t, ATT_KAUG_W), lambda g, i: (0, g)),
                  pl.BlockSpec((None, n_blocks, ATT_VT_ROWS, MOBA_BLOCK), lambda g, i: (g, 0, 0, 0)),
                  pl.BlockSpec((None, LANES, ATT_HEAD_DIM), lambda g, i: (g, 0, 0))] + s_in,
        out_specs=[pl.BlockSpec((tq, ATT_REP * ATT_HEAD_DIM), lambda g, i: (i, g))] + s_out,
        out_shape=[jax.ShapeDtypeStruct((t, ATT_Q_W), BF16)] + s_shapes,
        scratch_shapes=[pltpu.VMEM((ATT_REP, ATT_VT_ROWS, tq), F32),
                        pltpu.VMEM((ATT_REP, ATT_KAUG_W, tq), BF16),
                        pltpu.VMEM(((ATT_REP - 1) * ATT_NB, MOBA_BLOCK, tq), F32)],
        compiler_params=_cparams("arbitrary", "arbitrary"),
        name="moba_attention",
    )(qt, k, vt, km, *s_args)
    return tuple(outs) if side_casts else outs[0]


def _router_body(h_ref, g_ref, wr_ref, wrl_ref, br_ref, xn_ref, re_ref, rw_ref, cnt_ref):
    h = h_ref[...]
    ms = jnp.mean(h * h, axis=-1, keepdims=True)
    xn = h * lax.rsqrt(ms + NORM_EPS) * g_ref[...]
    xn_ref[...] = _pack_bf16_pairs(xn)
    xh = xn.astype(BF16)
    xl = (xn - xh.astype(F32)).astype(BF16)
    logits = (jnp.dot(xh, wr_ref[...], preferred_element_type=F32)
              + jnp.dot(xh, wrl_ref[...], preferred_element_type=F32)
              + jnp.dot(xl, wr_ref[...], preferred_element_type=F32)) + br_ref[...]
    lane = lax.broadcasted_iota(I32, logits.shape, 1)
    neg = -jnp.inf

    is_g = lane < MOE_GROUPS
    gl = jnp.where(is_g, logits, neg)
    ge = jnp.exp(gl - jnp.max(gl, axis=1, keepdims=True))
    gp = ge / jnp.sum(ge, axis=1, keepdims=True)
    g_w = jnp.max(gp, axis=1, keepdims=True)
    g_idx = jnp.min(jnp.where(is_g & (gp == g_w), lane, LANES), axis=1, keepdims=True)

    e_lo = MOE_GROUPS + MOE_EPG * g_idx
    is_e = (lane >= e_lo) & (lane < e_lo + MOE_EPG)
    el = jnp.where(is_e, logits, neg)
    ee = jnp.exp(el - jnp.max(el, axis=1, keepdims=True))
    ep = jnp.where(is_e, ee / jnp.sum(ee, axis=1, keepdims=True), -1.0)
    p1 = jnp.max(ep, axis=1, keepdims=True)
    i1 = jnp.min(jnp.where(ep == p1, lane, LANES), axis=1, keepdims=True)
    ep2 = jnp.where(lane == i1, -1.0, ep)
    p2 = jnp.max(ep2, axis=1, keepdims=True)
    i2 = jnp.min(jnp.where(ep2 == p2, lane, LANES), axis=1, keepdims=True)
    den = p1 + p2
    w1 = g_w * (p1 / den)
    w2 = g_w * (p2 / den)
    rw_ref[...] = jnp.where(lane == 0, w1, jnp.where(lane == 1, w2, 0.0))

    @pl.when(pl.program_id(0) == 0)
    def _():
        cnt_ref[...] = jnp.zeros_like(cnt_ref)

    e1, e2 = i1 - MOE_GROUPS, i2 - MOE_GROUPS
    oh1 = (lane == e1).astype(F32)
    oh2 = (lane == e2).astype(F32)
    tm = h.shape[0]
    earlier = (lax.broadcasted_iota(I32, (tm, tm), 1) < lax.broadcasted_iota(I32, (tm, tm), 0)).astype(BF16)
    before = jnp.dot(earlier, (oh1 + oh2).astype(BF16), preferred_element_type=F32) + cnt_ref[...]
    r1 = jnp.sum(oh1 * before, axis=1, keepdims=True).astype(I32)
    r2 = jnp.sum(oh2 * (before + oh1), axis=1, keepdims=True).astype(I32)
    cnt_ref[...] = cnt_ref[...] + jnp.sum(oh1 + oh2, axis=0, keepdims=True)
    re_ref[...] = jnp.where(lane == 0, e1, jnp.where(lane == 1, e2, jnp.where(lane == 2, r1,
                                                                               jnp.where(lane == 3, r2, 0))))


def router_call(h, gain, w_group, b_group, w_expert, b_expert):
    t, d = h.shape
    nr = MOE_GROUPS + MOE_EXPERTS
    wr = jnp.zeros((d, LANES), F32).at[:, :MOE_GROUPS].set(w_group).at[:, MOE_GROUPS:nr].set(w_expert)
    br = jnp.zeros((1, LANES), F32).at[0, :MOE_GROUPS].set(b_group).at[0, MOE_GROUPS:nr].set(b_expert)
    wr_hi = wr.astype(BF16)
    wr_lo = (wr - wr_hi.astype(F32)).astype(BF16)
    return pl.pallas_call(
        _router_body,
        grid=(t // NORM_TM,),
        in_specs=[pl.BlockSpec((NORM_TM, d), lambda i: (i, 0)),
                  pl.BlockSpec((1, d), lambda i: (0, 0)),
                  pl.BlockSpec((d, LANES), lambda i: (0, 0)),
                  pl.BlockSpec((d, LANES), lambda i: (0, 0)),
                  pl.BlockSpec((1, LANES), lambda i: (0, 0))],
        out_specs=[pl.BlockSpec((NORM_TM, d // 2), lambda i: (i, 0)),
                   pl.BlockSpec((NORM_TM, LANES), lambda i: (i, 0)),
                   pl.BlockSpec((NORM_TM, LANES), lambda i: (i, 0)),
                   pl.BlockSpec((1, LANES), lambda i: (0, 0))],
        out_shape=[jax.ShapeDtypeStruct((t, d // 2), jnp.uint32),
                   jax.ShapeDtypeStruct((t, LANES), I32),
                   jax.ShapeDtypeStruct((t, LANES), F32),
                   jax.ShapeDtypeStruct((1, LANES), F32)],
        compiler_params=_cparams("arbitrary"),
        name="norm_router",
    )(h, gain.reshape(1, d), wr_hi, wr_lo, br)


def _row_copy(src_ref, dst_ref, sem, src_row, dst_row):
    return pltpu.make_async_copy(src_ref.at[pl.ds(src_row, 1), :], dst_ref.at[pl.ds(dst_row, 1), :], sem)


def _start_row_gather(idx_ref, src_ref, dst_ref, sem, n):
    for r in range(n):
        _row_copy(src_ref, dst_ref, sem, idx_ref[0, r], r).start(priority=r % 2)


def _wait_row_gather(src_ref, dst_ref, sem, n):
    for r in range(n):
        _row_copy(src_ref, dst_ref, sem, 0, r).wait()


def _ffn_body(be_ref, nv_ref, tok0_ref, tokn_ref, x_hbm, wg_ref, wu_ref, wd_ref, o_ref, xbuf, sems):
    b = pl.program_id(0)
    nv = nv_ref[0]
    last = pl.num_programs(0) - 1
    tm = o_ref.shape[0]
    slot = b % 2
    valid = b < nv

    @pl.when(b == 0)
    def _():
        _start_row_gather(tok0_ref, x_hbm, xbuf.at[0], sems.at[0], tm)

    @pl.when(b <= nv)
    def _():
        _wait_row_gather(x_hbm, xbuf.at[slot], sems.at[slot], tm)

    @pl.when(valid)
    def _():
        _start_row_gather(tokn_ref, x_hbm, xbuf.at[1 - slot], sems.at[1 - slot], tm)
        x = _unpack_bf16_pairs(xbuf[slot])
        gate = jnp.dot(x, wg_ref[...], preferred_element_type=F32)
        up = jnp.dot(x, wu_ref[...], preferred_element_type=F32)
        hid = (_silu(gate) * up).astype(BF16)
        o_ref[...] = jnp.dot(hid, wd_ref[...], preferred_element_type=F32).astype(o_ref.dtype)

    @pl.when(jnp.logical_not(valid))
    def _():
        o_ref[...] = jnp.zeros_like(o_ref)

    @pl.when(valid & (b == last))
    def _():
        _wait_row_gather(x_hbm, xbuf.at[1 - slot], sems.at[1 - slot], tm)


def ffn_call(xn, row_token, blk_expert, n_valid, w_gate, w_up, w_down, layer):
    d, ff = w_gate.shape[2], w_gate.shape[3]
    tm = MOE_TM
    n_rows = row_token.shape[0]
    n_blk = n_rows // tm

    def w_map(b, be, nv):
        return (layer, be[jnp.minimum(b, nv[0] - 1)], 0, 0)

    tok_spec = functools.partial(pl.BlockSpec, (None, 1, tm), memory_space=pltpu.SMEM)
    grid_spec = pltpu.PrefetchScalarGridSpec(
        num_scalar_prefetch=2,
        grid=(n_blk,),
        in_specs=[tok_spec(index_map=lambda b, be, nv: (0, 0, 0)),
                  tok_spec(index_map=lambda b, be, nv: (jnp.minimum(b + 1, n_blk - 1), 0, 0)),
                  pl.BlockSpec(memory_space=pl.ANY),
                  pl.BlockSpec((None, None, d, ff), w_map),
                  pl.BlockSpec((None, None, d, ff), w_map),
                  pl.BlockSpec((None, None, ff, d), w_map)],
        out_specs=pl.BlockSpec((tm, d), lambda b, be, nv: (b, 0)),
        scratch_shapes=[pltpu.VMEM((2, tm, xn.shape[1]), xn.dtype), pltpu.SemaphoreType.DMA((2,))],
    )
    tok = row_token.reshape(n_blk, 1, tm)
    return pl.pallas_call(
        _ffn_body,
        grid_spec=grid_spec,
        out_shape=jax.ShapeDtypeStruct((n_rows, d), F32),
        compiler_params=_cparams("arbitrary"),
        name="moe_ffn",
    )(blk_expert, n_valid, tok, tok, xn, w_gate, w_up, w_down)


def _combine_body(pos_ref, posn_ref, y_ref, h_ref, rw_ref, g_ref, *out_refs_and_scratch, with_norm):
    if with_norm:
        ho_ref, xn_ref, buf_ref, sems = out_refs_and_scratch
    else:
        ho_ref, buf_ref, sems = out_refs_and_scratch
    tb = ho_ref.shape[0]
    n = MOE_TOPK * tb
    i = pl.program_id(0)
    slot = i % 2

    @pl.when(i == 0)
    def _():
        _start_row_gather(pos_ref, y_ref, buf_ref.at[0], sems.at[0], n)

    _wait_row_gather(y_ref, buf_ref.at[slot], sems.at[slot], n)

    @pl.when(i + 1 < pl.num_programs(0))
    def _():
        _start_row_gather(posn_ref, y_ref, buf_ref.at[1 - slot], sems.at[1 - slot], n)

    rw = rw_ref[...]
    ffn = rw[:, 0:1] * buf_ref[slot, 0:tb, :] + rw[:, 1:2] * buf_ref[slot, tb:2 * tb, :]
    hn = h_ref[...] + ffn
    ho_ref[...] = hn
    if with_norm:
        ms = jnp.mean(hn * hn, axis=-1, keepdims=True)
        xn_ref[...] = (hn * lax.rsqrt(ms + NORM_EPS) * g_ref[...]).astype(xn_ref.dtype)


def combine_call(y, pos, h, route_w, next_gain):
    t, d = h.shape
    tb = NORM_TM // 2
    n_steps = t // tb
    with_norm = next_gain is not None
    gain = (next_gain if with_norm else jnp.ones((d,), F32)).reshape(1, d)
    out_specs = [pl.BlockSpec((tb, d), lambda i: (i, 0))]
    out_shape = [jax.ShapeDtypeStruct((t, d), F32)]
    if with_norm:
        out_specs.append(pl.BlockSpec((tb, d), lambda i: (i, 0)))
        out_shape.append(jax.ShapeDtypeStruct((t, d), BF16))
    pos_spec = functools.partial(pl.BlockSpec, (None, 1, MOE_TOPK * tb), memory_space=pltpu.SMEM)
    outs = pl.pallas_call(
        functools.partial(_combine_body, with_norm=with_norm),
        grid=(n_steps,),
        in_specs=[pos_spec(index_map=lambda i: (i, 0, 0)),
                  pos_spec(index_map=lambda i: (jnp.minimum(i + 1, n_steps - 1), 0, 0)),
                  pl.BlockSpec(memory_space=pl.ANY),
                  pl.BlockSpec((tb, d), lambda i: (i, 0)),
                  pl.BlockSpec((tb, LANES), lambda i: (i, 0)),
                  pl.BlockSpec((1, d), lambda i: (0, 0))],
        out_specs=out_specs,
        out_shape=out_shape,
        scratch_shapes=[pltpu.VMEM((2, MOE_TOPK * tb, d), F32), pltpu.SemaphoreType.DMA((2,))],
        compiler_params=_cparams("arbitrary"),
        name="moe_combine",
    )(pos, pos, y, h, route_w, gain)
    return (outs[0], outs[1]) if with_norm else (outs[0], None)


def _dispatch_tables(route_e, counts, t):
    tm = MOE_TM
    n_assign = MOE_TOPK * t
    expert = route_e[:, :MOE_TOPK].reshape(n_assign)
    rank = route_e[:, MOE_TOPK:2 * MOE_TOPK].reshape(n_assign)
    counts = counts[0, :MOE_EXPERTS].astype(I32)
    padded = (counts + tm - 1) // tm * tm
    pend = jnp.cumsum(padded)
    pstart = pend - padded
    onehot = expert[:, None] == jnp.arange(MOE_EXPERTS, dtype=I32)[None, :]
    dest = jnp.sum(jnp.where(onehot, pstart[None, :], 0), axis=1) + rank
    n_rows = n_assign + MOE_EXPERTS * tm
    token = jnp.arange(n_assign, dtype=I32) // MOE_TOPK
    row_token = jnp.zeros((n_rows,), I32).at[dest].set(token)
    n_blk = n_rows // tm
    blk_start = jnp.arange(n_blk, dtype=I32) * tm
    blk_expert = jnp.minimum(jnp.sum((pend[None, :] <= blk_start[:, None]).astype(I32), axis=1),
                             MOE_EXPERTS - 1)
    n_valid = (pend[-1] // tm).astype(I32).reshape(1)
    tb = NORM_TM // 2
    pos = dest.reshape(t // tb, tb, MOE_TOPK).transpose(0, 2, 1).reshape(t // tb, 1, MOE_TOPK * tb)
    return row_token, blk_expert, n_valid, pos.astype(I32)


def moe_layer(h, gain, w_group, b_group, w_expert, b_expert, expert_w16, next_gain):
    t, d = h.shape
    xn, route_e, route_w, counts = router_call(h, gain, w_group, b_group, w_expert, b_expert)
    row_token, blk_expert, n_valid, pos = _dispatch_tables(route_e, counts, t)
    g16, u16, d16 = expert_w16
    y = ffn_call(xn, row_token, blk_expert, n_valid, g16.reshape(1, MOE_EXPERTS, d, -1),
                 u16.reshape(1, MOE_EXPERTS, d, -1), d16.reshape(1, MOE_EXPERTS, -1, d), 0)
    return combine_call(y, pos, h, route_w, next_gain)


def hybrid_layer(h, xn, w_in16, w_out16, conv_w, conv_b, dt_bias, a_log, d_skip, ssd_norm, ln_g, ln_b, ws, bs,
                 expert_w, layer, next_mixer_w):
    wg, wu, wd = expert_w
    hp = SSD_HEADS_PER_GROUP
    w_dt = jnp.zeros((D_MODEL, SSD_GROUPS, LANES), BF16).at[:, :, :hp].set(
        w_in16[:, OFF_DT:OFF_U].reshape(D_MODEL, SSD_GROUPS, hp)).reshape(D_MODEL, SSD_GROUPS * LANES)
    proj_a, g16 = matmul_call([xn], w_in16, out_dtype=BF16, n_cols=OFF_DT, side_casts=[(wg, layer)],
                              name="hyb_in_proj_ssd")
    proj_b, u16 = matmul_call([xn], w_in16[:, OFF_U:], out_dtype=BF16, side_casts=[(wu, layer)],
                              name="hyb_in_proj_gmlp")
    dt_raw = matmul_call([xn], w_dt, out_dtype=F32, tn=SSD_GROUPS * LANES, name="hyb_dt_proj")
    dtt_raw = dt_raw.reshape(-1, SSD_GROUPS, LANES)[:, :, :hp].reshape(-1, SSD_HEADS).T
    next16 = None
    if next_mixer_w is None:
        y_a = ssd_call(proj_a, dt_raw, dtt_raw, conv_w, conv_b, dt_bias, a_log, d_skip, ssd_norm)
    else:
        stacks, idx = next_mixer_w
        y_a, *next16 = ssd_call(proj_a, dt_raw, dtt_raw, conv_w, conv_b, dt_bias, a_log, d_skip, ssd_norm,
                                side_casts=[(s, idx) for s in stacks])
    y_b = gmlp_call(proj_b, ln_g, ln_b, ws, bs)
    h, d16 = matmul_call([y_a, y_b], w_out16, res=h, side_casts=[(wd, layer)], name="hyb_out_proj")
    return h, (g16, u16, d16), next16


def moba_layer(h, xn, w_qkv16, w_out16, q_norm, k_norm, cosf, sinf, expert_w, layer, next_mixer_w):
    wg, wu, wd = expert_w
    qkv, g16, u16 = matmul_call([xn], w_qkv16, out_dtype=BF16, side_casts=[(wg, layer), (wu, layer)],
                                name="att_qkv_proj")
    qt, k, vt, kmean = qk_prep_call(qkv, q_norm, k_norm, cosf, sinf)
    next16 = None
    if next_mixer_w is None:
        o = attention_call(qt, k, vt, kmean)
    else:
        stacks, idx = next_mixer_w
        o, *next16 = attention_call(qt, k, vt, kmean, side_casts=[(s, idx) for s in stacks])
    h, d16 = matmul_call([o], w_out16, res=h, side_casts=[(wd, layer)], name="att_out_proj")
    return h, (g16, u16, d16), next16


def _rope_tables(t):
    inv = 1.0 / (ROPE_THETA ** (jnp.arange(0, ATT_HEAD_DIM, 2, dtype=F32) / ATT_HEAD_DIM))
    ang = jnp.arange(t, dtype=F32)[:, None] * inv[None, :]
    cos, sin = jnp.cos(ang), jnp.sin(ang)
    return jnp.concatenate([cos, cos], axis=1), jnp.concatenate([-sin, sin], axis=1)


def kernel(x, norm_mix, norm_ffn, hyb_w_in, ssd_conv_w, ssd_conv_b, ssd_dt_bias, ssd_a_log, ssd_d, ssd_norm, gm_ln_g, gm_ln_b, gm_ws, gm_bs, hyb_w_out, att_w_qkv, att_q_norm, att_k_norm, att_w_out, moe_w_group, moe_b_group, moe_w_expert, moe_b_expert, moe_w_gate, moe_w_up, moe_w_down):
    bsz, t, d = x.shape
    depth = norm_mix.shape[0]
    cosf, sinf = _rope_tables(t)
    ff = moe_w_gate.shape[-1]
    expert_w = (moe_w_gate.reshape(depth, MOE_EXPERTS * d, ff), moe_w_up.reshape(depth, MOE_EXPERTS * d, ff),
                moe_w_down.reshape(depth, MOE_EXPERTS * ff, d))
    outs = []
    for b in range(bsz):
        h = x[b]
        xn = rmsnorm_call(h, norm_mix[0])
        mixer16 = (hyb_w_in[0].astype(BF16), hyb_w_out[0].astype(BF16))
        for layer in range(depth):
            j = layer // 2
            if layer % 2 == 0:
                next_w = ((att_w_qkv, att_w_out), j) if layer + 1 < depth else None
                h, expert_w16, mixer16 = hybrid_layer(
                    h, xn, mixer16[0], mixer16[1], ssd_conv_w[j], ssd_conv_b[j], ssd_dt_bias[j], ssd_a_log[j],
                    ssd_d[j], ssd_norm[j], gm_ln_g[j], gm_ln_b[j], gm_ws[j], gm_bs[j], expert_w, layer, next_w)
            else:
                next_w = ((hyb_w_in, hyb_w_out), j + 1) if layer + 1 < depth else None
                h, expert_w16, mixer16 = moba_layer(h, xn, mixer16[0], mixer16[1], att_q_norm[j], att_k_norm[j],
                                                    cosf, sinf, expert_w, layer, next_w)
            next_gain = norm_mix[layer + 1] if layer + 1 < depth else None
            h, xn = moe_layer(h, norm_ffn[layer], moe_w_group[layer], moe_b_group[layer],
                              moe_w_expert[layer], moe_b_expert[layer], expert_w16, next_gain)
        outs.append(h)
    return jnp.stack(outs, axis=0)
```

```python
import functools

import jax
import jax.numpy as jnp
from jax import lax
from jax.experimental import pallas as pl
from jax.experimental.pallas import tpu as pltpu

F32 = jnp.float32
BF16 = jnp.bfloat16
I32 = jnp.int32

D_MODEL = 4096
NORM_EPS = 1e-6

SSD_INNER = 2048
SSD_HEAD_DIM = 64
SSD_HEADS = 32
SSD_GROUPS = 4
SSD_STATE = 128
SSD_CONV = 4
SSD_CHUNK = 256
SSD_BC = SSD_GROUPS * SSD_STATE
SSD_CONV_CH = SSD_INNER + 2 * SSD_BC
SSD_GROUP_W = SSD_INNER // SSD_GROUPS
SSD_HEADS_PER_GROUP = SSD_HEADS // SSD_GROUPS

GM_INNER = 2048
GM_CHUNK = 128
GM_GROUPS = 16
GM_GROUP_DIM = 128

OFF_XBC = SSD_INNER
OFF_DT = OFF_XBC + SSD_CONV_CH
OFF_U = OFF_DT + SSD_HEADS
OFF_V = OFF_U + GM_INNER

ATT_HEAD_DIM = 128
ATT_HEADS = 32
ATT_KV_HEADS = 8
ATT_REP = ATT_HEADS // ATT_KV_HEADS
ATT_Q_W = ATT_HEADS * ATT_HEAD_DIM
ATT_KV_W = ATT_KV_HEADS * ATT_HEAD_DIM
ROPE_THETA = 10000.0
MOBA_BLOCK = 256
MOBA_TOPK = 3

MOE_GROUPS = 4
MOE_EPG = 4
MOE_EXPERTS = 16
MOE_TOPK = 2
MOE_FF = 768

LANES = 128
VMEM_LIMIT_BYTES = 56 * 1024 * 1024

MM_TM = 512
MM_TN = 1024
NORM_TM = 256
ATT_KAUG_W = ATT_HEAD_DIM + LANES
ATT_VT_ROWS = ATT_HEAD_DIM + 16
MASK_BIAS = -1e30
ATT_NB = 2
QK_SCALE_LOG2E = ATT_HEAD_DIM ** -0.5 * 1.4426950408889634
MOE_TM = 256


def _cparams(*sem, flags=None):
    return pltpu.CompilerParams(dimension_semantics=sem, vmem_limit_bytes=VMEM_LIMIT_BYTES, flags=flags)


def _silu(x):
    return x / (1.0 + jnp.exp(-x))


def _softplus(x):
    return jnp.maximum(x, 0.0) + jnp.log1p(jnp.exp(-jnp.abs(x)))


def _gelu_erf(x):
    return 0.5 * x * (1.0 + lax.erf(x * (2.0 ** -0.5)))


def _pack_bf16_pairs(x):
    c = x.shape[1] // 2
    bits = pltpu.bitcast(x.astype(BF16).astype(F32), jnp.uint32)
    return (bits[:, :c] >> 16) | bits[:, c:]


def _unpack_bf16_pairs(w, dtype=BF16):
    lo = pltpu.bitcast(w << 16, F32)
    hi = pltpu.bitcast(w & jnp.uint32(0xFFFF0000), F32)
    return jnp.concatenate([lo, hi], axis=1).astype(dtype)


def _dot_f32_by_01(x, m01, f32_on_right=False):
    hi = x.astype(BF16)
    r1 = x - hi.astype(F32)
    mid = r1.astype(BF16)
    lo = (r1 - mid.astype(F32)).astype(BF16)
    if f32_on_right:
        return sum(jnp.dot(m01, t, preferred_element_type=F32) for t in (hi, mid, lo))
    return sum(jnp.dot(t, m01, preferred_element_type=F32) for t in (hi, mid, lo))


def _rmsnorm_body(x_ref, g_ref, o_ref):
    x = x_ref[...]
    ms = jnp.mean(x * x, axis=-1, keepdims=True)
    o_ref[...] = (x * lax.rsqrt(ms + NORM_EPS) * g_ref[...]).astype(o_ref.dtype)


def rmsnorm_call(x, g):
    t, d = x.shape
    return pl.pallas_call(
        _rmsnorm_body,
        grid=(t // NORM_TM,),
        in_specs=[pl.BlockSpec((NORM_TM, d), lambda i: (i, 0)),
                  pl.BlockSpec((1, d), lambda i: (0, 0))],
        out_specs=pl.BlockSpec((NORM_TM, d), lambda i: (i, 0)),
        out_shape=jax.ShapeDtypeStruct((t, d), BF16),
        compiler_params=_cparams("parallel"),
        name="rmsnorm",
    )(x, g.reshape(1, d))


def _matmul_body(*refs, n_parts, has_res, n_side):
    a_refs = refs[:n_parts]
    w_refs = refs[n_parts:2 * n_parts]
    n_in = 2 * n_parts + int(has_res) + n_side
    o_ref = refs[n_in]
    acc = jnp.dot(a_refs[0][...], w_refs[0][...], preferred_element_type=F32)
    for p in range(1, n_parts):
        acc = acc + jnp.dot(a_refs[p][...], w_refs[p][...], preferred_element_type=F32)
    if has_res:
        acc = acc + refs[2 * n_parts][...]
    o_ref[...] = acc.astype(o_ref.dtype)
    _side_cast(refs[n_in - n_side:n_in], refs[n_in + 1:n_in + 1 + n_side])


def _side_blocks(rows, n_steps):
    for nsb in range(min(n_steps, rows // 16), 0, -1):
        if rows % nsb == 0 and (rows // nsb) % 16 == 0:
            return nsb
    return 1


def _side_cast_plumbing(side_casts, n_steps, step_of):
    in_specs, out_specs, out_shapes, args = [], [], [], []
    for src, idx in side_casts:
        _, rows, cols = src.shape
        nsb = _side_blocks(rows, n_steps)

        def slab(*g, nsb=nsb):
            return jnp.minimum(step_of(*g), nsb - 1)

        in_specs.append(pl.BlockSpec((None, rows // nsb, cols),
                                     lambda *g, slab=slab, idx=idx: (idx, slab(*g), 0)))
        out_specs.append(pl.BlockSpec((rows // nsb, cols), lambda *g, slab=slab: (slab(*g), 0)))
        out_shapes.append(jax.ShapeDtypeStruct((rows, cols), BF16))
        args.append(src)
    return in_specs, out_specs, out_shapes, args


def _side_cast(in_refs, out_refs):
    for src, dst in zip(in_refs, out_refs):
        dst[...] = src[...].astype(BF16)


def matmul_call(a_parts, w, res=None, out_dtype=F32, tn=MM_TN, n_cols=None, layer=None, side_casts=(),
                name="matmul"):
    n_parts = len(a_parts)
    m, kp = a_parts[0].shape
    n = w.shape[-1] if n_cols is None else n_cols
    tm = MM_TM
    steps_i = m // tm
    n_steps = (n // tn) * steps_i
    in_specs = [pl.BlockSpec((tm, kp), lambda j, i: (i, 0)) for _ in a_parts]
    if layer is None:
        in_specs += [pl.BlockSpec((kp, tn), functools.partial(lambda j, i, p: (p, j), p=p))
                     for p in range(n_parts)]
    else:
        in_specs += [pl.BlockSpec((None, kp, tn), functools.partial(lambda j, i, p: (layer, p, j), p=p))
                     for p in range(n_parts)]
    args = list(a_parts) + [w] * n_parts
    if res is not None:
        in_specs.append(pl.BlockSpec((tm, tn), lambda j, i: (i, j)))
        args.append(res)
    out_specs = [pl.BlockSpec((tm, tn), lambda j, i: (i, j))]
    out_shape = [jax.ShapeDtypeStruct((m, n), out_dtype)]
    s_in, s_out, s_shapes, s_args = _side_cast_plumbing(side_casts, n_steps, lambda j, i: j * steps_i + i)
    in_specs += s_in
    out_specs += s_out
    out_shape += s_shapes
    args += s_args
    outs = pl.pallas_call(
        functools.partial(_matmul_body, n_parts=n_parts, has_res=res is not None, n_side=len(side_casts)),
        grid=(n // tn, m // tm),
        in_specs=in_specs,
        out_specs=out_specs,
        out_shape=out_shape,
        compiler_params=_cparams("arbitrary", "arbitrary"),
        name=name,
    )(*args)
    return tuple(outs) if side_casts else outs[0]


def _ssd_body(z_ref, xs_ref, b_ref, c_ref, dt_ref, dtt_ref,
              wx_ref, bx_ref, wb_ref, bb_ref, wc_ref, bc_ref,
              dtb_ref, dtbt_ref, alog_ref, alogt_ref, dskip_ref, ng_ref, expand_ref, *rest, n_side):
    o_ref = rest[n_side]
    state_ref, tx_ref, tb_ref, tc_ref = rest[2 * n_side + 1:]
    _side_cast(rest[:n_side], rest[n_side + 1:2 * n_side + 1])
    c = pl.program_id(1)
    L = SSD_CHUNK
    HP = SSD_HEADS_PER_GROUP

    @pl.when(c == 0)
    def _():
        state_ref[...] = jnp.zeros_like(state_ref)
        tx_ref[...] = jnp.zeros_like(tx_ref)
        tb_ref[...] = jnp.zeros_like(tb_ref)
        tc_ref[...] = jnp.zeros_like(tc_ref)

    def conv_silu(cur_ref, tail_ref, w_ref, bias_ref):
        cur = cur_ref[...].astype(F32)
        ext = jnp.concatenate([tail_ref[...], cur], axis=0)
        w = w_ref[...]
        acc = bias_ref[...] + w[SSD_CONV - 1:SSD_CONV, :] * cur
        for j in range(1, SSD_CONV):
            acc = acc + w[SSD_CONV - 1 - j:SSD_CONV - j, :] * ext[8 - j:8 - j + L, :]
        tail_ref[...] = cur[L - 8:L, :]
        return _silu(acc)

    xs = conv_silu(xs_ref, tx_ref, wx_ref, bx_ref)
    bm = conv_silu(b_ref, tb_ref, wb_ref, bb_ref)
    cm = conv_silu(c_ref, tc_ref, wc_ref, bc_ref)

    dt = _softplus(dt_ref[...] + dtb_ref[...])
    da = dt * (-jnp.exp(alog_ref[...]))
    dtt = _softplus(dtt_ref[...] + dtbt_ref[...])
    dat = dtt * (-jnp.exp(alogt_ref[...]))

    row = lax.broadcasted_iota(I32, (L, L), 0)
    col = lax.broadcasted_iota(I32, (L, L), 1)
    causal = row >= col
    cs = _dot_f32_by_01(da, causal.astype(BF16), f32_on_right=True)
    cst = _dot_f32_by_01(dat, (row <= col).astype(BF16))
    cs_last = cs[L - 1:L, :]
    to_end = jnp.exp(cs_last - cs)
    ecs = jnp.exp(cs)

    expand = expand_ref[...]
    dt_e = _dot_f32_by_01(dt, expand)
    to_end_e = _dot_f32_by_01(to_end, expand)
    ecs_e = _dot_f32_by_01(ecs, expand)

    x = xs * dt_e
    bm16 = bm.astype(BF16)
    cm16 = cm.astype(BF16)
    cb = lax.dot_general(cm16, bm16, (((1,), (1,)), ((), ())), preferred_element_type=F32)

    prev = state_ref[...]
    y = jnp.dot(cm16, prev.astype(BF16), preferred_element_type=F32) * ecs_e

    x16 = x.astype(BF16)
    heads_per_tile = LANES // SSD_HEAD_DIM
    first_head = lax.broadcasted_iota(I32, (L, LANES), 1) < SSD_HEAD_DIM
    tiles = []
    for tt in range(SSD_GROUP_W // LANES):
        x_tile = x16[:, tt * LANES:(tt + 1) * LANES]
        y_tile = None
        for hh in range(heads_per_tile):
            h = tt * heads_per_tile + hh
            seg = cs[:, h:h + 1] - cst[h:h + 1, :]
            decay = jnp.exp(jnp.where(causal, seg, -jnp.inf))
            m16 = (cb * decay).astype(BF16)
            xh = jnp.where(first_head == (hh == 0), x_tile, jnp.zeros((), BF16))
            d = jnp.dot(m16, xh, preferred_element_type=F32)
            y_tile = d if y_tile is None else y_tile + d
        tiles.append(y_tile)
    y = y + jnp.concatenate(tiles, axis=1)

    xw = (x * to_end_e).astype(BF16)
    state_ref[...] = prev * ecs_e[L - 1:L, :] + jnp.dot(bm.T.astype(BF16), xw,
                                                        preferred_element_type=F32)

    y = y + xs * dskip_ref[...]
    y = y * _silu(z_ref[...].astype(F32))
    ms = jnp.mean(y * y, axis=-1, keepdims=True)
    o_ref[...] = (y * lax.rsqrt(ms + NORM_EPS) * ng_ref[...]).astype(o_ref.dtype)


def ssd_call(proj, dt_raw, dtt_raw, conv_w, conv_b, dt_bias, a_log, d_skip, norm_g, side_casts=()):
    t = proj.shape[0]
    L = SSD_CHUNK
    gw = SSD_GROUP_W
    xs0 = SSD_INNER // gw
    b0 = 2 * SSD_INNER // SSD_STATE
    c0 = b0 + SSD_GROUPS
    hp = SSD_HEADS_PER_GROUP

    dtb = jnp.zeros((SSD_GROUPS, LANES), F32).at[:, :hp].set(dt_bias.reshape(SSD_GROUPS, hp))
    alog = jnp.zeros((SSD_GROUPS, LANES), F32).at[:, :hp].set(a_log.reshape(SSD_GROUPS, hp))
    dtb = dtb.reshape(1, SSD_GROUPS * LANES)
    alog = alog.reshape(1, SSD_GROUPS * LANES)
    dtbt = dt_bias.reshape(SSD_HEADS, 1)
    alogt = a_log.reshape(SSD_HEADS, 1)
    dskip = jnp.repeat(d_skip, SSD_HEAD_DIM).reshape(1, SSD_INNER)
    expand = (jnp.arange(LANES)[:, None] == (jnp.arange(gw)[None, :] // SSD_HEAD_DIM)).astype(BF16)
    cw = conv_w.astype(F32)
    cbias = conv_b.reshape(1, SSD_CONV_CH).astype(F32)
    xb0 = SSD_INNER // SSD_STATE
    cb0 = xb0 + SSD_GROUPS

    in_specs = [
        pl.BlockSpec((L, gw), lambda g, c: (c, g)),
        pl.BlockSpec((L, gw), lambda g, c: (c, xs0 + g)),
        pl.BlockSpec((L, SSD_STATE), lambda g, c: (c, b0 + g)),
        pl.BlockSpec((L, SSD_STATE), lambda g, c: (c, c0 + g)),
        pl.BlockSpec((L, LANES), lambda g, c: (c, g)),
        pl.BlockSpec((hp, L), lambda g, c: (g, c)),
        pl.BlockSpec((SSD_CONV, gw), lambda g, c: (0, g)),
        pl.BlockSpec((1, gw), lambda g, c: (0, g)),
        pl.BlockSpec((SSD_CONV, SSD_STATE), lambda g, c: (0, xb0 + g)),
        pl.BlockSpec((1, SSD_STATE), lambda g, c: (0, xb0 + g)),
        pl.BlockSpec((SSD_CONV, SSD_STATE), lambda g, c: (0, cb0 + g)),
        pl.BlockSpec((1, SSD_STATE), lambda g, c: (0, cb0 + g)),
        pl.BlockSpec((1, LANES), lambda g, c: (0, g)),
        pl.BlockSpec((hp, 1), lambda g, c: (g, 0)),
        pl.BlockSpec((1, LANES), lambda g, c: (0, g)),
        pl.BlockSpec((hp, 1), lambda g, c: (g, 0)),
        pl.BlockSpec((1, gw), lambda g, c: (0, g)),
        pl.BlockSpec((1, gw), lambda g, c: (0, g)),
        pl.BlockSpec((LANES, gw), lambda g, c: (0, 0)),
    ]
    n_chunks = t // L
    s_in, s_out, s_shapes, s_args = _side_cast_plumbing(side_casts, SSD_GROUPS * n_chunks,
                                                        lambda g, c: g * n_chunks + c)
    outs = pl.pallas_call(
        functools.partial(_ssd_body, n_side=len(side_casts)),
        grid=(SSD_GROUPS, n_chunks),
        in_specs=in_specs + s_in,
        out_specs=[pl.BlockSpec((L, gw), lambda g, c: (c, g))] + s_out,
        out_shape=[jax.ShapeDtypeStruct((t, SSD_INNER), BF16)] + s_shapes,
        scratch_shapes=[pltpu.VMEM((SSD_STATE, gw), F32),
                        pltpu.VMEM((8, gw), F32),
                        pltpu.VMEM((8, SSD_STATE), F32),
                        pltpu.VMEM((8, SSD_STATE), F32)],
        compiler_params=_cparams("arbitrary", "arbitrary"),
        name="ssd",
    )(proj, proj, proj, proj, dt_raw, dtt_raw, cw, cbias, cw, cbias, cw, cbias,
      dtb, dtbt, alog, alogt, dskip, norm_g.reshape(1, SSD_INNER), expand, *s_args)
    return tuple(outs) if side_casts else outs[0]


def _gmlp_body(u_ref, v_ref, g_ref, b_ref, ws_ref, bst_ref, o_ref):
    u = _gelu_erf(u_ref[...].astype(F32))
    v = _gelu_erf(v_ref[...].astype(F32))
    mu = jnp.mean(v, axis=-1, keepdims=True)
    vc = v - mu
    var = jnp.mean(vc * vc, axis=-1, keepdims=True)
    vn = (vc * lax.rsqrt(var + NORM_EPS) * g_ref[...] + b_ref[...]).astype(BF16)
    row = lax.broadcasted_iota(I32, (GM_CHUNK, GM_CHUNK), 0)
    col = lax.broadcasted_iota(I32, (GM_CHUNK, GM_CHUNK), 1)
    causal = row >= col
    for g in range(GM_GROUPS):
        sl = slice(g * GM_GROUP_DIM, (g + 1) * GM_GROUP_DIM)
        w = jnp.where(causal, ws_ref[g], jnp.zeros((), BF16))
        sv = jnp.dot(w, vn[:, sl], preferred_element_type=F32) + bst_ref[:, sl]
        o_ref[:, sl] = (u[:, sl] * sv).astype(o_ref.dtype)


def gmlp_call(proj, ln_g, ln_b, ws, bs):
    t = proj.shape[0]
    ucol = 0
    bst =jnp.repeat(bs.T, GM_GROUP_DIM, axis=1)
    return pl.pallas_call(
        _gmlp_body,
        grid=(t // GM_CHUNK,),
        in_specs=[pl.BlockSpec((GM_CHUNK, GM_INNER), lambda i: (i, ucol)),
                  pl.BlockSpec((GM_CHUNK, GM_INNER), lambda i: (i, ucol + 1)),
                  pl.BlockSpec((1, GM_INNER), lambda i: (0, 0)),
                  pl.BlockSpec((1, GM_INNER), lambda i: (0, 0)),
                  pl.BlockSpec((GM_GROUPS, GM_CHUNK, GM_CHUNK), lambda i: (0, 0, 0)),
                  pl.BlockSpec((GM_CHUNK, GM_INNER), lambda i: (0, 0))],
        out_specs=pl.BlockSpec((GM_CHUNK, GM_INNER), lambda i: (i, 0)),
        out_shape=jax.ShapeDtypeStruct((t, GM_INNER), BF16),
        compiler_params=_cparams("parallel"),
        name="gmlp",
    )(proj, proj, ln_g.reshape(1, GM_INNER), ln_b.reshape(1, GM_INNER), ws.astype(BF16), bst)


def _rope_norm(x, gain, cosf, sinf):
    ms = jnp.mean(x * x, axis=-1, keepdims=True)
    xn = x * lax.rsqrt(ms + NORM_EPS) * gain
    return xn * cosf + pltpu.roll(xn, ATT_HEAD_DIM // 2, axis=1) * sinf


def _qprep_body(x_ref, g_ref, cos_ref, sin_ref, o_ref):
    for r in range(ATT_REP):
        x = x_ref[:, r * ATT_HEAD_DIM:(r + 1) * ATT_HEAD_DIM].astype(F32)
        y = _rope_norm(x, g_ref[...], cos_ref[...], sin_ref[...]) * QK_SCALE_LOG2E
        o_ref[r] = y.T.astype(o_ref.dtype)


def _kprep_body(x_ref, v_ref, g_ref, cos_ref, sin_ref, o_ref, vt_ref, mean_ref):
    y = _rope_norm(x_ref[...].astype(F32), g_ref[...], cos_ref[...], sin_ref[...])
    tb = y.shape[0]
    lane = lax.broadcasted_iota(I32, (tb, LANES), 1)
    o_ref[:, :ATT_HEAD_DIM] = y.astype(o_ref.dtype)
    o_ref[:, ATT_HEAD_DIM:] = jnp.where(lane == pl.program_id(0), 1.0, 0.0).astype(o_ref.dtype)
    mean_ref[...] = jnp.mean(y, axis=0, keepdims=True)
    vt_ref[:ATT_HEAD_DIM, :] = v_ref[...].astype(F32).T.astype(vt_ref.dtype)
    vt_ref[ATT_HEAD_DIM:, :] = jnp.ones((ATT_VT_ROWS - ATT_HEAD_DIM, tb), vt_ref.dtype)


def qk_prep_call(qkv, q_norm, k_norm, cosf, sinf):
    t = qkv.shape[0]
    tb = MOBA_BLOCK
    gw = ATT_REP * ATT_HEAD_DIM
    common = [pl.BlockSpec((1, ATT_HEAD_DIM), lambda i, h: (0, 0)),
              pl.BlockSpec((tb, ATT_HEAD_DIM), lambda i, h: (i, 0)),
              pl.BlockSpec((tb, ATT_HEAD_DIM), lambda i, h: (i, 0))]
    qt = pl.pallas_call(
        _qprep_body,
        grid=(t // tb, ATT_KV_HEADS),
        in_specs=[pl.BlockSpec((tb, gw), lambda i, h: (i, h))] + common,
        out_specs=pl.BlockSpec((ATT_REP, ATT_HEAD_DIM, tb), lambda i, h: (h, 0, i)),
        out_shape=jax.ShapeDtypeStruct((ATT_HEADS, ATT_HEAD_DIM, t), BF16),
        compiler_params=_cparams("parallel", "parallel"),
        name="q_prep",
    )(qkv, q_norm.reshape(1, ATT_HEAD_DIM), cosf, sinf)
    vcol = (ATT_Q_W + ATT_KV_W) // ATT_HEAD_DIM
    k, vt, kmean = pl.pallas_call(
        _kprep_body,
        grid=(t // tb, ATT_KV_HEADS),
        in_specs=[pl.BlockSpec((tb, ATT_HEAD_DIM), lambda i, h: (i, ATT_HEADS + h)),
                  pl.BlockSpec((tb, ATT_HEAD_DIM), lambda i, h: (i, vcol + h))] + common,
        out_specs=[pl.BlockSpec((tb, ATT_KAUG_W), lambda i, h: (i, h)),
                   pl.BlockSpec((None, None, ATT_VT_ROWS, tb), lambda i, h: (h, i, 0, 0)),
                   pl.BlockSpec((None, None, 1, ATT_HEAD_DIM), lambda i, h: (i, h, 0, 0))],
        out_shape=[jax.ShapeDtypeStruct((t, ATT_KV_HEADS * ATT_KAUG_W), BF16),
                   jax.ShapeDtypeStruct((ATT_KV_HEADS, t // tb, ATT_VT_ROWS, tb), BF16),
                   jax.ShapeDtypeStruct((t // tb, ATT_KV_HEADS, 1, ATT_HEAD_DIM), F32)],
        compiler_params=_cparams("parallel", "parallel"),
        name="k_prep",
    )(qkv, qkv, k_norm.reshape(1, ATT_HEAD_DIM), cosf, sinf)
    return qt, k, vt, kmean


def _attn_body(qt_ref, k_ref, vt_ref, km_ref, *rest, nb, n_side):
    o_ref = rest[n_side]
    acc_ref, qa_ref, s_ref = rest[2 * n_side + 1:]
    _side_cast(rest[:n_side], rest[n_side + 1:2 * n_side + 1])
    qblk = pl.program_id(1)
    blk = MOBA_BLOCK
    hd = ATT_HEAD_DIM
    n_blocks = k_ref.shape[0] // blk
    nbp = (n_blocks + 7) // 8 * 8
    own0 = pl.multiple_of(qblk * blk, blk)
    km = km_ref[...]
    km_hi = km.astype(BF16)
    km_lo = (km - km_hi.astype(F32)).astype(BF16)
    bid = lax.broadcasted_iota(I32, (nbp, blk), 0)
    causal = lax.broadcasted_iota(I32, (blk, blk), 0) <= lax.broadcasted_iota(I32, (blk, blk), 1)

    gates, own_scores = [], []
    for r in range(ATT_REP):
        qt = qt_ref[r]
        gates.append((jnp.dot(km_hi, qt, preferred_element_type=F32)
                      + jnp.dot(km_lo, qt, preferred_element_type=F32))[:nbp])
    for r in range(ATT_REP):
        own_scores.append(jnp.dot(k_ref[pl.ds(own0, blk), :hd], qt_ref[r],
                                  preferred_element_type=F32))
    for r in range(ATT_REP):
        gate = jnp.where(bid < qblk, gates[r], -jnp.inf)
        picked = bid < 0
        for kk in range(MOBA_TOPK):
            mx = jnp.max(gate, axis=0, keepdims=True)
            idx = jnp.min(jnp.where(gate == mx, bid, LANES), axis=0, keepdims=True)
            hit = bid == idx
            picked = picked | (hit & (kk < qblk))
            gate = jnp.where(hit, -jnp.inf, gate)
        bias = jnp.where(picked, 0.0, MASK_BIAS).astype(BF16)
        qa_ref[r, :hd, :] = qt_ref[r]
        qa_ref[r, hd:hd + nbp, :] = bias
        if nbp < LANES:
            qa_ref[r, hd + nbp:, :] = jnp.zeros((LANES - nbp, blk), BF16)

    def blocks_of(jj):
        return [jnp.minimum(jj * nb + u, n_blocks - 1) for u in range(nb)]

    def head_scores(r, blocks):
        return [jnp.dot(k_ref[pl.ds(pl.multiple_of(ja * blk, blk), blk), :], qa_ref[r],
                        preferred_element_type=F32) for ja in blocks]

    ahead = ATT_REP - 1
    for r in range(ahead):
        for u, su in enumerate(head_scores(r, blocks_of(0))):
            s_ref[r * nb + u] = su

    m0 = []
    for r in range(ATT_REP):
        s = jnp.where(causal, own_scores[r], -jnp.inf)
        m = jnp.max(s, axis=0, keepdims=True)
        p = jnp.exp2(s - m)
        m0.append(m)
        acc_ref[r] = jnp.dot(vt_ref[qblk], p.astype(BF16), preferred_element_type=F32)

    def past_blocks(jj, ms):
        blocks = blocks_of(jj)
        blocks_next = blocks_of(jj + 1)
        vts = [vt_ref[ja] for ja in blocks]
        m_out = []
        in_flight = {}
        for r in range(ATT_REP):
            scores = in_flight.pop(r) if r in in_flight else [s_ref[r * nb + u] for u in range(nb)]
            r_req = r + ahead
            if r_req < ATT_REP:
                in_flight[r_req] = head_scores(r_req, blocks)
            else:
                for u, su in enumerate(head_scores(r_req - ATT_REP, blocks_next)):
                    s_ref[(r_req - ATT_REP) * nb + u] = su
            m_new = ms[r]
            for su in scores:
                m_new = jnp.maximum(m_new, jnp.max(su, axis=0, keepdims=True))
            alpha = jnp.exp2(ms[r] - m_new)
            pv = None
            for u in range(nb):
                pu = jnp.exp2(scores[u] - m_new).astype(BF16)
                d = jnp.dot(vts[u], pu, preferred_element_type=F32)
                pv = d if pv is None else pv + d
            acc_ref[r] = alpha * acc_ref[r] + pv
            m_out.append(m_new)
        return tuple(m_out)

    lax.fori_loop(0, (qblk + nb - 1) // nb, past_blocks, tuple(m0))

    for r in range(ATT_REP):
        out = acc_ref[r, :hd, :] / acc_ref[r, hd:hd + 1, :]
        o_ref[:, r * hd:(r + 1) * hd] = out.T.astype(o_ref.dtype)


def attention_call(qt, k, vt, kmean, side_casts=()):
    t = k.shape[0]
    tq = MOBA_BLOCK
    n_blocks = t // MOBA_BLOCK
    n_tiles = t // tq
    km = jnp.transpose(kmean.reshape(n_blocks, ATT_KV_HEADS, ATT_HEAD_DIM), (1, 0, 2))
    km = jnp.pad(km, ((0, 0), (0, LANES - n_blocks), (0, 0)))
    s_in, s_out, s_shapes, s_args = _side_cast_plumbing(side_casts, ATT_KV_HEADS * n_tiles,
                                                        lambda g, i: g * n_tiles + i)
    outs = pl.pallas_call(
        functools.partial(_attn_body, nb=ATT_NB, n_side=len(side_casts)),
        grid=(ATT_KV_HEADS, n_tiles),
        in_specs=[pl.BlockSpec((ATT_REP, ATT_HEAD_DIM, tq), lambda g, i: (g, 0, i)),
                  pl.BlockSpec((t, ATT_KAUG_W), lambda g, i: (0, g)),
                  pl.BlockSpec((None, n_blocks, ATT_VT_ROWS, MOBA_BLOCK), lambda g, i: (g, 0, 0, 0)),
                  pl.BlockSpec((None, LANES, ATT_HEAD_DIM), lambda g, i: (g, 0, 0))] + s_in,
        out_specs=[pl.BlockSpec((tq, ATT_REP * ATT_HEAD_DIM), lambda g, i: (i, g))] + s_out,
        out_shape=[jax.ShapeDtypeStruct((t, ATT_Q_W), BF16)] + s_shapes,
        scratch_shapes=[pltpu.VMEM((ATT_REP, ATT_VT_ROWS, tq), F32),
                        pltpu.VMEM((ATT_REP, ATT_KAUG_W, tq), BF16),
                        pltpu.VMEM(((ATT_REP - 1) * ATT_NB, MOBA_BLOCK, tq), F32)],
        compiler_params=_cparams("arbitrary", "arbitrary"),
        name="moba_attention",
    )(qt, k, vt, km, *s_args)
    return tuple(outs) if side_casts else outs[0]


def _router_body(h_ref, g_ref, wr_ref, wrl_ref, br_ref, xn_ref, re_ref, rw_ref, cnt_ref):
    h = h_ref[...]
    ms = jnp.mean(h * h, axis=-1, keepdims=True)
    xn = h * lax.rsqrt(ms + NORM_EPS) * g_ref[...]
    xn_ref[...] = _pack_bf16_pairs(xn)
    xh = xn.astype(BF16)
    xl = (xn - xh.astype(F32)).astype(BF16)
    logits = (jnp.dot(xh, wr_ref[...], preferred_element_type=F32)
              + jnp.dot(xh, wrl_ref[...], preferred_element_type=F32)
              + jnp.dot(xl, wr_ref[...], preferred_element_type=F32)) + br_ref[...]
    lane = lax.broadcasted_iota(I32, logits.shape, 1)
    neg = -jnp.inf

    is_g = lane < MOE_GROUPS
    gl = jnp.where(is_g, logits, neg)
    ge = jnp.exp(gl - jnp.max(gl, axis=1, keepdims=True))
    gp = ge / jnp.sum(ge, axis=1, keepdims=True)
    g_w = jnp.max(gp, axis=1, keepdims=True)
    g_idx = jnp.min(jnp.where(is_g & (gp == g_w), lane, LANES), axis=1, keepdims=True)

    e_lo = MOE_GROUPS + MOE_EPG * g_idx
    is_e = (lane >= e_lo) & (lane < e_lo + MOE_EPG)
    el = jnp.where(is_e, logits, neg)
    ee = jnp.exp(el - jnp.max(el, axis=1, keepdims=True))
    ep = jnp.where(is_e, ee / jnp.sum(ee, axis=1, keepdims=True), -1.0)
    p1 = jnp.max(ep, axis=1, keepdims=True)
    i1 = jnp.min(jnp.where(ep == p1, lane, LANES), axis=1, keepdims=True)
    ep2 = jnp.where(lane == i1, -1.0, ep)
    p2 = jnp.max(ep2, axis=1, keepdims=True)
    i2 = jnp.min(jnp.where(ep2 == p2, lane, LANES), axis=1, keepdims=True)
    den = p1 + p2
    w1 = g_w * (p1 / den)
    w2 = g_w * (p2 / den)
    rw_ref[...] = jnp.where(lane == 0, w1, jnp.where(lane == 1, w2, 0.0))

    @pl.when(pl.program_id(0) == 0)
    def _():
        cnt_ref[...] = jnp.zeros_like(cnt_ref)

    e1, e2 = i1 - MOE_GROUPS, i2 - MOE_GROUPS
    oh1 = (lane == e1).astype(F32)
    oh2 = (lane == e2).astype(F32)
    tm = h.shape[0]
    earlier = (lax.broadcasted_iota(I32, (tm, tm), 1) < lax.broadcasted_iota(I32, (tm, tm), 0)).astype(BF16)
    before = jnp.dot(earlier, (oh1 + oh2).astype(BF16), preferred_element_type=F32) + cnt_ref[...]
    r1 = jnp.sum(oh1 * before, axis=1, keepdims=True).astype(I32)
    r2 = jnp.sum(oh2 * (before + oh1), axis=1, keepdims=True).astype(I32)
    cnt_ref[...] = cnt_ref[...] + jnp.sum(oh1 + oh2, axis=0, keepdims=True)
    re_ref[...] = jnp.where(lane == 0, e1, jnp.where(lane == 1, e2, jnp.where(lane == 2, r1,
                                                                               jnp.where(lane == 3, r2, 0))))


def router_call(h, gain, w_group, b_group, w_expert, b_expert):
    t, d = h.shape
    nr = MOE_GROUPS + MOE_EXPERTS
    wr = jnp.zeros((d, LANES), F32).at[:, :MOE_GROUPS].set(w_group).at[:, MOE_GROUPS:nr].set(w_expert)
    br = jnp.zeros((1, LANES), F32).at[0, :MOE_GROUPS].set(b_group).at[0, MOE_GROUPS:nr].set(b_expert)
    wr_hi = wr.astype(BF16)
    wr_lo = (wr - wr_hi.astype(F32)).astype(BF16)
    return pl.pallas_call(
        _router_body,
        grid=(t // NORM_TM,),
        in_specs=[pl.BlockSpec((NORM_TM, d), lambda i: (i, 0)),
                  pl.BlockSpec((1, d), lambda i: (0, 0)),
                  pl.BlockSpec((d, LANES), lambda i: (0, 0)),
                  pl.BlockSpec((d, LANES), lambda i: (0, 0)),
                  pl.BlockSpec((1, LANES), lambda i: (0, 0))],
        out_specs=[pl.BlockSpec((NORM_TM, d // 2), lambda i: (i, 0)),
                   pl.BlockSpec((NORM_TM, LANES), lambda i: (i, 0)),
                   pl.BlockSpec((NORM_TM, LANES), lambda i: (i, 0)),
                   pl.BlockSpec((1, LANES), lambda i: (0, 0))],
        out_shape=[jax.ShapeDtypeStruct((t, d // 2), jnp.uint32),
                   jax.ShapeDtypeStruct((t, LANES), I32),
                   jax.ShapeDtypeStruct((t, LANES), F32),
                   jax.ShapeDtypeStruct((1, LANES), F32)],
        compiler_params=_cparams("arbitrary"),
        name="norm_router",
    )(h, gain.reshape(1, d), wr_hi, wr_lo, br)


def _row_copy(src_ref, dst_ref, sem, src_row, dst_row):
    return pltpu.make_async_copy(src_ref.at[pl.ds(src_row, 1), :], dst_ref.at[pl.ds(dst_row, 1), :], sem)


def _start_row_gather(idx_ref, src_ref, dst_ref, sem, n):
    for r in range(n):
        _row_copy(src_ref, dst_ref, sem, idx_ref[0, r], r).start(priority=r % 2)


def _wait_row_gather(src_ref, dst_ref, sem, n):
    for r in range(n):
        _row_copy(src_ref, dst_ref, sem, 0, r).wait()


def _ffn_body(be_ref, nv_ref, tok0_ref, tokn_ref, x_hbm, wg_ref, wu_ref, wd_ref, o_ref, xbuf, sems):
    b = pl.program_id(0)
    nv = nv_ref[0]
    last = pl.num_programs(0) - 1
    tm = o_ref.shape[0]
    slot = b % 2
    valid = b < nv

    @pl.when(b == 0)
    def _():
        _start_row_gather(tok0_ref, x_hbm, xbuf.at[0], sems.at[0], tm)

    @pl.when(b <= nv)
    def _():
        _wait_row_gather(x_hbm, xbuf.at[slot], sems.at[slot], tm)

    @pl.when(valid)
    def _():
        _start_row_gather(tokn_ref, x_hbm, xbuf.at[1 - slot], sems.at[1 - slot], tm)
        x = _unpack_bf16_pairs(xbuf[slot])
        gate = jnp.dot(x, wg_ref[...], preferred_element_type=F32)
        up = jnp.dot(x, wu_ref[...], preferred_element_type=F32)
        hid = (_silu(gate) * up).astype(BF16)
        o_ref[...] = _pack_bf16_pairs(jnp.dot(hid, wd_ref[...], preferred_element_type=F32))

    @pl.when(jnp.logical_not(valid))
    def _():
        o_ref[...] = jnp.zeros_like(o_ref)

    @pl.when(valid & (b == last))
    def _():
        _wait_row_gather(x_hbm, xbuf.at[1 - slot], sems.at[1 - slot], tm)


def ffn_call(xn, row_token, blk_expert, n_valid, w_gate, w_up, w_down, layer):
    d, ff = w_gate.shape[2], w_gate.shape[3]
    tm = MOE_TM
    n_rows = row_token.shape[0]
    n_blk = n_rows // tm

    def w_map(b, be, nv):
        return (layer, be[jnp.minimum(b, nv[0] - 1)], 0, 0)

    tok_spec = functools.partial(pl.BlockSpec, (None, 1, tm), memory_space=pltpu.SMEM)
    grid_spec = pltpu.PrefetchScalarGridSpec(
        num_scalar_prefetch=2,
        grid=(n_blk,),
        in_specs=[tok_spec(index_map=lambda b, be, nv: (0, 0, 0)),
                  tok_spec(index_map=lambda b, be, nv: (jnp.minimum(b + 1, n_blk - 1), 0, 0)),
                  pl.BlockSpec(memory_space=pl.ANY),
                  pl.BlockSpec((None, None, d, ff), w_map),
                  pl.BlockSpec((None, None, d, ff), w_map),
                  pl.BlockSpec((None, None, ff, d), w_map)],
        out_specs=pl.BlockSpec((tm, d // 2), lambda b, be, nv: (b, 0)),
        scratch_shapes=[pltpu.VMEM((2, tm, xn.shape[1]), xn.dtype), pltpu.SemaphoreType.DMA((2,))],
    )
    tok = row_token.reshape(n_blk, 1, tm)
    return pl.pallas_call(
        _ffn_body,
        grid_spec=grid_spec,
        out_shape=jax.ShapeDtypeStruct((n_rows, d // 2), jnp.uint32),
        compiler_params=_cparams("arbitrary"),
        name="moe_ffn",
    )(blk_expert, n_valid, tok, tok, xn, w_gate, w_up, w_down)


def _combine_body(pos_ref, posn_ref, y_ref, h_ref, rw_ref, g_ref, *out_refs_and_scratch, with_norm):
    if with_norm:
        ho_ref, xn_ref, buf_ref, sems = out_refs_and_scratch
    else:
        ho_ref, buf_ref, sems = out_refs_and_scratch
    tb = ho_ref.shape[0]
    n = MOE_TOPK * tb
    i = pl.program_id(0)
    slot = i % 2

    @pl.when(i == 0)
    def _():
        _start_row_gather(pos_ref, y_ref, buf_ref.at[0], sems.at[0], n)

    _wait_row_gather(y_ref, buf_ref.at[slot], sems.at[slot], n)

    @pl.when(i + 1 < pl.num_programs(0))
    def _():
        _start_row_gather(posn_ref, y_ref, buf_ref.at[1 - slot], sems.at[1 - slot], n)

    rw = rw_ref[...]
    y = _unpack_bf16_pairs(buf_ref[slot], F32)
    ffn = rw[:, 0:1] * y[0:tb, :] + rw[:, 1:2] * y[tb:2 * tb, :]
    hn = h_ref[...] + ffn
    ho_ref[...] = hn
    if with_norm:
        ms = jnp.mean(hn * hn, axis=-1, keepdims=True)
        xn_ref[...] = (hn * lax.rsqrt(ms + NORM_EPS) * g_ref[...]).astype(xn_ref.dtype)


def combine_call(y, pos, h, route_w, next_gain):
    t, d = h.shape
    tb = NORM_TM // 2
    n_steps = t // tb
    with_norm = next_gain is not None
    gain = (next_gain if with_norm else jnp.ones((d,), F32)).reshape(1, d)
    out_specs = [pl.BlockSpec((tb, d), lambda i: (i, 0))]
    out_shape = [jax.ShapeDtypeStruct((t, d), F32)]
    if with_norm:
        out_specs.append(pl.BlockSpec((tb, d), lambda i: (i, 0)))
        out_shape.append(jax.ShapeDtypeStruct((t, d), BF16))
    pos_spec = functools.partial(pl.BlockSpec, (None, 1, MOE_TOPK * tb), memory_space=pltpu.SMEM)
    outs = pl.pallas_call(
        functools.partial(_combine_body, with_norm=with_norm),
        grid=(n_steps,),
        in_specs=[pos_spec(index_map=lambda i: (i, 0, 0)),
                  pos_spec(index_map=lambda i: (jnp.minimum(i + 1, n_steps - 1), 0, 0)),
                  pl.BlockSpec(memory_space=pl.ANY),
                  pl.BlockSpec((tb, d), lambda i: (i, 0)),
                  pl.BlockSpec((tb, LANES), lambda i: (i, 0)),
                  pl.BlockSpec((1, d), lambda i: (0, 0))],
        out_specs=out_specs,
        out_shape=out_shape,
        scratch_shapes=[pltpu.VMEM((2, MOE_TOPK * tb, d // 2), jnp.uint32), pltpu.SemaphoreType.DMA((2,))],
        compiler_params=_cparams("arbitrary"),
        name="moe_combine",
    )(pos, pos, y, h, route_w, gain)
    return (outs[0], outs[1]) if with_norm else (outs[0], None)


def _dispatch_tables(route_e, counts, t):
    tm = MOE_TM
    n_assign = MOE_TOPK * t
    expert = route_e[:, :MOE_TOPK].reshape(n_assign)
    rank = route_e[:, MOE_TOPK:2 * MOE_TOPK].reshape(n_assign)
    counts = counts[0, :MOE_EXPERTS].astype(I32)
    padded = (counts + tm - 1) // tm * tm
    pend = jnp.cumsum(padded)
    pstart = pend - padded
    onehot = expert[:, None] == jnp.arange(MOE_EXPERTS, dtype=I32)[None, :]
    dest = jnp.sum(jnp.where(onehot, pstart[None, :], 0), axis=1) + rank
    n_rows = n_assign + MOE_EXPERTS * tm
    token = jnp.arange(n_assign, dtype=I32) // MOE_TOPK
    row_token = jnp.zeros((n_rows,), I32).at[dest].set(token)
    n_blk = n_rows // tm
    blk_start = jnp.arange(n_blk, dtype=I32) * tm
    blk_expert = jnp.minimum(jnp.sum((pend[None, :] <= blk_start[:, None]).astype(I32), axis=1),
                             MOE_EXPERTS - 1)
    n_valid = (pend[-1] // tm).astype(I32).reshape(1)
    tb = NORM_TM // 2
    pos = dest.reshape(t // tb, tb, MOE_TOPK).transpose(0, 2, 1).reshape(t // tb, 1, MOE_TOPK * tb)
    return row_token, blk_expert, n_valid, pos.astype(I32)


def moe_layer(h, gain, w_group, b_group, w_expert, b_expert, expert_w16, next_gain):
    t, d = h.shape
    xn, route_e, route_w, counts = router_call(h, gain, w_group, b_group, w_expert, b_expert)
    row_token, blk_expert, n_valid, pos = _dispatch_tables(route_e, counts, t)
    g16, u16, d16 = expert_w16
    y = ffn_call(xn, row_token, blk_expert, n_valid, g16.reshape(1, MOE_EXPERTS, d, -1),
                 u16.reshape(1, MOE_EXPERTS, d, -1), d16.reshape(1, MOE_EXPERTS, -1, d), 0)
    return combine_call(y, pos, h, route_w, next_gain)


def hybrid_layer(h, xn, w_in16_all, j, w_out16, conv_w, conv_b, dt_bias, a_log, d_skip, ssd_norm, ln_g, ln_b,
                 ws, bs, expert_w, layer, next_mixer_w):
    w_in16 = w_in16_all[j]
    wg, wu, wd = expert_w
    hp = SSD_HEADS_PER_GROUP
    w_dt = jnp.zeros((D_MODEL, SSD_GROUPS, LANES), BF16).at[:, :, :hp].set(
        w_in16[:, OFF_DT:OFF_U].reshape(D_MODEL, SSD_GROUPS, hp)).reshape(D_MODEL, SSD_GROUPS * LANES)
    proj_a, g16 = matmul_call([xn], w_in16_all, out_dtype=BF16, n_cols=OFF_DT, layer=j,
                              side_casts=[(wg, layer)], name="hyb_in_proj_ssd")
    proj_b, u16 = matmul_call([xn], w_in16[:, OFF_U:], out_dtype=BF16, side_casts=[(wu, layer)],
                              name="hyb_in_proj_gmlp")
    dt_raw = matmul_call([xn], w_dt, out_dtype=F32, tn=SSD_GROUPS * LANES, name="hyb_dt_proj")
    dtt_raw = dt_raw.reshape(-1, SSD_GROUPS, LANES)[:, :, :hp].reshape(-1, SSD_HEADS).T
    next16 = None
    if next_mixer_w is None:
        y_a = ssd_call(proj_a, dt_raw, dtt_raw, conv_w, conv_b, dt_bias, a_log, d_skip, ssd_norm)
    else:
        stacks, idx = next_mixer_w
        y_a, *next16 = ssd_call(proj_a, dt_raw, dtt_raw, conv_w, conv_b, dt_bias, a_log, d_skip, ssd_norm,
                                side_casts=[(s, idx) for s in stacks])
    y_b = gmlp_call(proj_b, ln_g, ln_b, ws, bs)
    h, d16 = matmul_call([y_a, y_b], w_out16, res=h, side_casts=[(wd, layer)], name="hyb_out_proj")
    return h, (g16, u16, d16), next16


def moba_layer(h, xn, w_qkv16, w_out16, q_norm, k_norm, cosf, sinf, expert_w, layer, next_mixer_w):
    wg, wu, wd = expert_w
    qkv, g16, u16 = matmul_call([xn], w_qkv16, out_dtype=BF16, side_casts=[(wg, layer), (wu, layer)],
                                name="att_qkv_proj")
    qt, k, vt, kmean = qk_prep_call(qkv, q_norm, k_norm, cosf, sinf)
    next16 = None
    if next_mixer_w is None:
        o = attention_call(qt, k, vt, kmean)
    else:
        stacks, idx = next_mixer_w
        o, *next16 = attention_call(qt, k, vt, kmean, side_casts=[(s, idx) for s in stacks])
    h, d16 = matmul_call([o], w_out16, res=h, side_casts=[(wd, layer)], name="att_out_proj")
    return h, (g16, u16, d16), next16


def _rope_tables(t):
    inv = 1.0 / (ROPE_THETA ** (jnp.arange(0, ATT_HEAD_DIM, 2, dtype=F32) / ATT_HEAD_DIM))
    ang = jnp.arange(t, dtype=F32)[:, None] * inv[None, :]
    cos, sin = jnp.cos(ang), jnp.sin(ang)
    return jnp.concatenate([cos, cos], axis=1), jnp.concatenate([-sin, sin], axis=1)


def kernel(x, norm_mix, norm_ffn, hyb_w_in, ssd_conv_w, ssd_conv_b, ssd_dt_bias, ssd_a_log, ssd_d, ssd_norm, gm_ln_g, gm_ln_b, gm_ws, gm_bs, hyb_w_out, att_w_qkv, att_q_norm, att_k_norm, att_w_out, moe_w_group, moe_b_group, moe_w_expert, moe_b_expert, moe_w_gate, moe_w_up, moe_w_down):
    bsz, t, d = x.shape
    depth = norm_mix.shape[0]
    cosf, sinf = _rope_tables(t)
    ff = moe_w_gate.shape[-1]
    expert_w = (moe_w_gate.reshape(depth, MOE_EXPERTS * d, ff), moe_w_up.reshape(depth, MOE_EXPERTS * d, ff),
                moe_w_down.reshape(depth, MOE_EXPERTS * ff, d))
    outs = []
    for b in range(bsz):
        h = x[b]
        xn = rmsnorm_call(h, norm_mix[0])
        hyb_w_in16 = hyb_w_in.astype(BF16)
        mixer16 = [hyb_w_out[0].astype(BF16)]
        for layer in range(depth):
            j = layer // 2
            if layer % 2 == 0:
                next_w = ((att_w_qkv, att_w_out), j) if layer + 1 < depth else None
                h, expert_w16, mixer16 = hybrid_layer(
                    h, xn, hyb_w_in16, j, mixer16[0], ssd_conv_w[j], ssd_conv_b[j], ssd_dt_bias[j], ssd_a_log[j],
                    ssd_d[j], ssd_norm[j], gm_ln_g[j], gm_ln_b[j], gm_ws[j], gm_bs[j], expert_w, layer, next_w)
            else:
                next_w = ((hyb_w_out,), j + 1) if layer + 1 < depth else None
                h, expert_w16, mixer16 = moba_layer(h, xn, mixer16[0], mixer16[1], att_q_norm[j], att_k_norm[j],
                                                    cosf, sinf, expert_w, layer, next_w)
            next_gain = norm_mix[layer + 1] if layer + 1 < depth else None
            h, xn = moe_layer(h, norm_ffn[layer], moe_w_group[layer], moe_b_group[layer],
                              moe_w_expert[layer], moe_b_expert[layer], expert_w16, next_gain)
        outs.append(h)
    return jnp.stack(outs, axis=0)
```

```python
import functools

import jax
import jax.numpy as jnp
from jax import lax
from jax.experimental import pallas as pl
from jax.experimental.pallas import tpu as pltpu

F32 = jnp.float32
BF16 = jnp.bfloat16
I32 = jnp.int32

D_MODEL = 4096
NORM_EPS = 1e-6

SSD_INNER = 2048
SSD_HEAD_DIM = 64
SSD_HEADS = 32
SSD_GROUPS = 4
SSD_STATE = 128
SSD_CONV = 4
SSD_CHUNK = 256
SSD_BC = SSD_GROUPS * SSD_STATE
SSD_CONV_CH = SSD_INNER + 2 * SSD_BC
SSD_GROUP_W = SSD_INNER // SSD_GROUPS
SSD_HEADS_PER_GROUP = SSD_HEADS // SSD_GROUPS

GM_INNER = 2048
GM_CHUNK = 128
GM_GROUPS = 16
GM_GROUP_DIM = 128

OFF_XBC = SSD_INNER
OFF_DT = OFF_XBC + SSD_CONV_CH
OFF_U = OFF_DT + SSD_HEADS
OFF_V = OFF_U + GM_INNER

ATT_HEAD_DIM = 128
ATT_HEADS = 32
ATT_KV_HEADS = 8
ATT_REP = ATT_HEADS // ATT_KV_HEADS
ATT_Q_W = ATT_HEADS * ATT_HEAD_DIM
ATT_KV_W = ATT_KV_HEADS * ATT_HEAD_DIM
ROPE_THETA = 10000.0
MOBA_BLOCK = 256
MOBA_TOPK = 3

MOE_GROUPS = 4
MOE_EPG = 4
MOE_EXPERTS = 16
MOE_TOPK = 2
MOE_FF = 768

LANES = 128
VMEM_LIMIT_BYTES = 56 * 1024 * 1024

MM_TM = 512
MM_TN = 1024
NORM_TM = 256
ATT_KAUG_W = ATT_HEAD_DIM + LANES
ATT_VT_ROWS = ATT_HEAD_DIM + 16
MASK_BIAS = -1e30
KPREP_BLOCKS = 2
ATT_NB = 2
QK_SCALE_LOG2E = ATT_HEAD_DIM ** -0.5 * 1.4426950408889634
MOE_TM = 256


def _cparams(*sem, flags=None):
    return pltpu.CompilerParams(dimension_semantics=sem, vmem_limit_bytes=VMEM_LIMIT_BYTES, flags=flags)


def _silu(x):
    return x / (1.0 + jnp.exp(-x))


def _softplus(x):
    return jnp.maximum(x, 0.0) + jnp.log1p(jnp.exp(-jnp.abs(x)))


def _gelu_erf(x):
    return 0.5 * x * (1.0 + lax.erf(x * (2.0 ** -0.5)))


def _pack_bf16_pairs(x):
    c = x.shape[1] // 2
    bits = pltpu.bitcast(x.astype(BF16).astype(F32), jnp.uint32)
    return (bits[:, :c] >> 16) | bits[:, c:]


def _unpack_bf16_pairs(w, dtype=BF16):
    lo = pltpu.bitcast(w << 16, F32)
    hi = pltpu.bitcast(w & jnp.uint32(0xFFFF0000), F32)
    return jnp.concatenate([lo, hi], axis=1).astype(dtype)


def _dot_f32_by_01(x, m01, f32_on_right=False):
    hi = x.astype(BF16)
    r1 = x - hi.astype(F32)
    mid = r1.astype(BF16)
    lo = (r1 - mid.astype(F32)).astype(BF16)
    if f32_on_right:
        return sum(jnp.dot(m01, t, preferred_element_type=F32) for t in (hi, mid, lo))
    return sum(jnp.dot(t, m01, preferred_element_type=F32) for t in (hi, mid, lo))


def _rmsnorm_body(x_ref, g_ref, o_ref):
    x = x_ref[...]
    ms = jnp.mean(x * x, axis=-1, keepdims=True)
    o_ref[...] = (x * lax.rsqrt(ms + NORM_EPS) * g_ref[...]).astype(o_ref.dtype)


def rmsnorm_call(x, g):
    t, d = x.shape
    return pl.pallas_call(
        _rmsnorm_body,
        grid=(t // NORM_TM,),
        in_specs=[pl.BlockSpec((NORM_TM, d), lambda i: (i, 0)),
                  pl.BlockSpec((1, d), lambda i: (0, 0))],
        out_specs=pl.BlockSpec((NORM_TM, d), lambda i: (i, 0)),
        out_shape=jax.ShapeDtypeStruct((t, d), BF16),
        compiler_params=_cparams("parallel"),
        name="rmsnorm",
    )(x, g.reshape(1, d))


def _matmul_body(*refs, n_parts, has_res, n_side):
    a_refs = refs[:n_parts]
    w_refs = refs[n_parts:2 * n_parts]
    n_in = 2 * n_parts + int(has_res) + n_side
    o_ref = refs[n_in]
    acc = jnp.dot(a_refs[0][...], w_refs[0][...], preferred_element_type=F32)
    for p in range(1, n_parts):
        acc = acc + jnp.dot(a_refs[p][...], w_refs[p][...], preferred_element_type=F32)
    if has_res:
        acc = acc + refs[2 * n_parts][...]
    o_ref[...] = acc.astype(o_ref.dtype)
    _side_cast(refs[n_in - n_side:n_in], refs[n_in + 1:n_in + 1 + n_side])


def _side_blocks(rows, n_steps):
    for nsb in range(min(n_steps, rows // 16), 0, -1):
        if rows % nsb == 0 and (rows // nsb) % 16 == 0:
            return nsb
    return 1


def _side_cast_plumbing(side_casts, n_steps, step_of):
    in_specs, out_specs, out_shapes, args = [], [], [], []
    for src, idx in side_casts:
        _, rows, cols = src.shape
        nsb = _side_blocks(rows, n_steps)

        def slab(*g, nsb=nsb):
            return jnp.minimum(step_of(*g), nsb - 1)

        in_specs.append(pl.BlockSpec((None, rows // nsb, cols),
                                     lambda *g, slab=slab, idx=idx: (idx, slab(*g), 0)))
        out_specs.append(pl.BlockSpec((rows // nsb, cols), lambda *g, slab=slab: (slab(*g), 0)))
        out_shapes.append(jax.ShapeDtypeStruct((rows, cols), BF16))
        args.append(src)
    return in_specs, out_specs, out_shapes, args


def _side_cast(in_refs, out_refs):
    for src, dst in zip(in_refs, out_refs):
        dst[...] = src[...].astype(BF16)


def matmul_call(a_parts, w, res=None, out_dtype=F32, tn=MM_TN, n_cols=None, layer=None, side_casts=(),
                name="matmul"):
    n_parts = len(a_parts)
    m, kp = a_parts[0].shape
    n = w.shape[-1] if n_cols is None else n_cols
    tm = MM_TM
    steps_i = m // tm
    n_steps = (n // tn) * steps_i
    in_specs = [pl.BlockSpec((tm, kp), lambda j, i: (i, 0)) for _ in a_parts]
    if layer is None:
        in_specs += [pl.BlockSpec((kp, tn), functools.partial(lambda j, i, p: (p, j), p=p))
                     for p in range(n_parts)]
    else:
        in_specs += [pl.BlockSpec((None, kp, tn), functools.partial(lambda j, i, p: (layer, p, j), p=p))
                     for p in range(n_parts)]
    args = list(a_parts) + [w] * n_parts
    if res is not None:
        in_specs.append(pl.BlockSpec((tm, tn), lambda j, i: (i, j)))
        args.append(res)
    out_specs = [pl.BlockSpec((tm, tn), lambda j, i: (i, j))]
    out_shape = [jax.ShapeDtypeStruct((m, n), out_dtype)]
    s_in, s_out, s_shapes, s_args = _side_cast_plumbing(side_casts, n_steps, lambda j, i: j * steps_i + i)
    in_specs += s_in
    out_specs += s_out
    out_shape += s_shapes
    args += s_args
    outs = pl.pallas_call(
        functools.partial(_matmul_body, n_parts=n_parts, has_res=res is not None, n_side=len(side_casts)),
        grid=(n // tn, m // tm),
        in_specs=in_specs,
        out_specs=out_specs,
        out_shape=out_shape,
        compiler_params=_cparams("arbitrary", "arbitrary"),
        name=name,
    )(*args)
    return tuple(outs) if side_casts else outs[0]


def _ssd_body(z_ref, xs_ref, b_ref, c_ref, dt_ref, dtt_ref,
              wx_ref, bx_ref, wb_ref, bb_ref, wc_ref, bc_ref,
              dtb_ref, dtbt_ref, alog_ref, alogt_ref, dskip_ref, ng_ref, expand_ref, *rest, n_side):
    o_ref = rest[n_side]
    state_ref, tx_ref, tb_ref, tc_ref = rest[2 * n_side + 1:]
    _side_cast(rest[:n_side], rest[n_side + 1:2 * n_side + 1])
    c = pl.program_id(1)
    L = SSD_CHUNK
    HP = SSD_HEADS_PER_GROUP

    @pl.when(c == 0)
    def _():
        state_ref[...] = jnp.zeros_like(state_ref)
        tx_ref[...] = jnp.zeros_like(tx_ref)
        tb_ref[...] = jnp.zeros_like(tb_ref)
        tc_ref[...] = jnp.zeros_like(tc_ref)

    def conv_silu(cur_ref, tail_ref, w_ref, bias_ref):
        cur = cur_ref[...].astype(F32)
        ext = jnp.concatenate([tail_ref[...], cur], axis=0)
        w = w_ref[...]
        acc = bias_ref[...] + w[SSD_CONV - 1:SSD_CONV, :] * cur
        for j in range(1, SSD_CONV):
            acc = acc + w[SSD_CONV - 1 - j:SSD_CONV - j, :] * ext[8 - j:8 - j + L, :]
        tail_ref[...] = cur[L - 8:L, :]
        return _silu(acc)

    xs = conv_silu(xs_ref, tx_ref, wx_ref, bx_ref)
    bm = conv_silu(b_ref, tb_ref, wb_ref, bb_ref)
    cm = conv_silu(c_ref, tc_ref, wc_ref, bc_ref)

    dt = _softplus(dt_ref[...] + dtb_ref[...])
    da = dt * (-jnp.exp(alog_ref[...]))
    dtt = _softplus(dtt_ref[...] + dtbt_ref[...])
    dat = dtt * (-jnp.exp(alogt_ref[...]))

    row = lax.broadcasted_iota(I32, (L, L), 0)
    col = lax.broadcasted_iota(I32, (L, L), 1)
    causal = row >= col
    cs = _dot_f32_by_01(da, causal.astype(BF16), f32_on_right=True)
    cst = _dot_f32_by_01(dat, (row <= col).astype(BF16))
    cs_last = cs[L - 1:L, :]
    to_end = jnp.exp(cs_last - cs)
    ecs = jnp.exp(cs)

    expand = expand_ref[...]
    dt_e = _dot_f32_by_01(dt, expand)
    to_end_e = _dot_f32_by_01(to_end, expand)
    ecs_e = _dot_f32_by_01(ecs, expand)

    x = xs * dt_e
    bm16 = bm.astype(BF16)
    cm16 = cm.astype(BF16)
    cb = lax.dot_general(cm16, bm16, (((1,), (1,)), ((), ())), preferred_element_type=F32)

    prev = state_ref[...]
    y = jnp.dot(cm16, prev.astype(BF16), preferred_element_type=F32) * ecs_e

    x16 = x.astype(BF16)
    heads_per_tile = LANES // SSD_HEAD_DIM
    first_head = lax.broadcasted_iota(I32, (L, LANES), 1) < SSD_HEAD_DIM
    tiles = []
    for tt in range(SSD_GROUP_W // LANES):
        x_tile = x16[:, tt * LANES:(tt + 1) * LANES]
        y_tile = None
        for hh in range(heads_per_tile):
            h = tt * heads_per_tile + hh
            seg = cs[:, h:h + 1] - cst[h:h + 1, :]
            decay = jnp.exp(jnp.where(causal, seg, -jnp.inf))
            m16 = (cb * decay).astype(BF16)
            xh = jnp.where(first_head == (hh == 0), x_tile, jnp.zeros((), BF16))
            d = jnp.dot(m16, xh, preferred_element_type=F32)
            y_tile = d if y_tile is None else y_tile + d
        tiles.append(y_tile)
    y = y + jnp.concatenate(tiles, axis=1)

    xw = (x * to_end_e).astype(BF16)
    state_ref[...] = prev * ecs_e[L - 1:L, :] + jnp.dot(bm.T.astype(BF16), xw,
                                                        preferred_element_type=F32)

    y = y + xs * dskip_ref[...]
    y = y * _silu(z_ref[...].astype(F32))
    ms = jnp.mean(y * y, axis=-1, keepdims=True)
    o_ref[...] = (y * lax.rsqrt(ms + NORM_EPS) * ng_ref[...]).astype(o_ref.dtype)


def ssd_call(proj, dt_raw, dtt_raw, conv_w, conv_b, dt_bias, a_log, d_skip, norm_g, side_casts=()):
    t = proj.shape[0]
    L = SSD_CHUNK
    gw = SSD_GROUP_W
    xs0 = SSD_INNER // gw
    b0 = 2 * SSD_INNER // SSD_STATE
    c0 = b0 + SSD_GROUPS
    hp = SSD_HEADS_PER_GROUP

    dtb = jnp.zeros((SSD_GROUPS, LANES), F32).at[:, :hp].set(dt_bias.reshape(SSD_GROUPS, hp))
    alog = jnp.zeros((SSD_GROUPS, LANES), F32).at[:, :hp].set(a_log.reshape(SSD_GROUPS, hp))
    dtb = dtb.reshape(1, SSD_GROUPS * LANES)
    alog = alog.reshape(1, SSD_GROUPS * LANES)
    dtbt = dt_bias.reshape(SSD_HEADS, 1)
    alogt = a_log.reshape(SSD_HEADS, 1)
    dskip = jnp.repeat(d_skip, SSD_HEAD_DIM).reshape(1, SSD_INNER)
    expand = (jnp.arange(LANES)[:, None] == (jnp.arange(gw)[None, :] // SSD_HEAD_DIM)).astype(BF16)
    cw = conv_w.astype(F32)
    cbias = conv_b.reshape(1, SSD_CONV_CH).astype(F32)
    xb0 = SSD_INNER // SSD_STATE
    cb0 = xb0 + SSD_GROUPS

    in_specs = [
        pl.BlockSpec((L, gw), lambda g, c: (c, g)),
        pl.BlockSpec((L, gw), lambda g, c: (c, xs0 + g)),
        pl.BlockSpec((L, SSD_STATE), lambda g, c: (c, b0 + g)),
        pl.BlockSpec((L, SSD_STATE), lambda g, c: (c, c0 + g)),
        pl.BlockSpec((L, LANES), lambda g, c: (c, g)),
        pl.BlockSpec((hp, L), lambda g, c: (g, c)),
        pl.BlockSpec((SSD_CONV, gw), lambda g, c: (0, g)),
        pl.BlockSpec((1, gw), lambda g, c: (0, g)),
        pl.BlockSpec((SSD_CONV, SSD_STATE), lambda g, c: (0, xb0 + g)),
        pl.BlockSpec((1, SSD_STATE), lambda g, c: (0, xb0 + g)),
        pl.BlockSpec((SSD_CONV, SSD_STATE), lambda g, c: (0, cb0 + g)),
        pl.BlockSpec((1, SSD_STATE), lambda g, c: (0, cb0 + g)),
        pl.BlockSpec((1, LANES), lambda g, c: (0, g)),
        pl.BlockSpec((hp, 1), lambda g, c: (g, 0)),
        pl.BlockSpec((1, LANES), lambda g, c: (0, g)),
        pl.BlockSpec((hp, 1), lambda g, c: (g, 0)),
        pl.BlockSpec((1, gw), lambda g, c: (0, g)),
        pl.BlockSpec((1, gw), lambda g, c: (0, g)),
        pl.BlockSpec((LANES, gw), lambda g, c: (0, 0)),
    ]
    n_chunks = t // L
    s_in, s_out, s_shapes, s_args = _side_cast_plumbing(side_casts, SSD_GROUPS * n_chunks,
                                                        lambda g, c: g * n_chunks + c)
    outs = pl.pallas_call(
        functools.partial(_ssd_body, n_side=len(side_casts)),
        grid=(SSD_GROUPS, n_chunks),
        in_specs=in_specs + s_in,
        out_specs=[pl.BlockSpec((L, gw), lambda g, c: (c, g))] + s_out,
        out_shape=[jax.ShapeDtypeStruct((t, SSD_INNER), BF16)] + s_shapes,
        scratch_shapes=[pltpu.VMEM((SSD_STATE, gw), F32),
                        pltpu.VMEM((8, gw), F32),
                        pltpu.VMEM((8, SSD_STATE), F32),
                        pltpu.VMEM((8, SSD_STATE), F32)],
        compiler_params=_cparams("arbitrary", "arbitrary"),
        name="ssd",
    )(proj, proj, proj, proj, dt_raw, dtt_raw, cw, cbias, cw, cbias, cw, cbias,
      dtb, dtbt, alog, alogt, dskip, norm_g.reshape(1, SSD_INNER), expand, *s_args)
    return tuple(outs) if side_casts else outs[0]


def _gmlp_body(u_ref, v_ref, g_ref, b_ref, ws_ref, bst_ref, o_ref):
    u = _gelu_erf(u_ref[...].astype(F32))
    v = _gelu_erf(v_ref[...].astype(F32))
    mu = jnp.mean(v, axis=-1, keepdims=True)
    vc = v - mu
    var = jnp.mean(vc * vc, axis=-1, keepdims=True)
    vn = (vc * lax.rsqrt(var + NORM_EPS) * g_ref[...] + b_ref[...]).astype(BF16)
    row = lax.broadcasted_iota(I32, (GM_CHUNK, GM_CHUNK), 0)
    col = lax.broadcasted_iota(I32, (GM_CHUNK, GM_CHUNK), 1)
    causal = row >= col
    for g in range(GM_GROUPS):
        sl = slice(g * GM_GROUP_DIM, (g + 1) * GM_GROUP_DIM)
        w = jnp.where(causal, ws_ref[g], jnp.zeros((), BF16))
        sv = jnp.dot(w, vn[:, sl], preferred_element_type=F32) + bst_ref[:, sl]
        o_ref[:, sl] = (u[:, sl] * sv).astype(o_ref.dtype)


def gmlp_call(proj, ln_g, ln_b, ws, bs):
    t = proj.shape[0]
    ucol = 0
    bst =jnp.repeat(bs.T, GM_GROUP_DIM, axis=1)
    return pl.pallas_call(
        _gmlp_body,
        grid=(t // GM_CHUNK,),
        in_specs=[pl.BlockSpec((GM_CHUNK, GM_INNER), lambda i: (i, ucol)),
                  pl.BlockSpec((GM_CHUNK, GM_INNER), lambda i: (i, ucol + 1)),
                  pl.BlockSpec((1, GM_INNER), lambda i: (0, 0)),
                  pl.BlockSpec((1, GM_INNER), lambda i: (0, 0)),
                  pl.BlockSpec((GM_GROUPS, GM_CHUNK, GM_CHUNK), lambda i: (0, 0, 0)),
                  pl.BlockSpec((GM_CHUNK, GM_INNER), lambda i: (0, 0))],
        out_specs=pl.BlockSpec((GM_CHUNK, GM_INNER), lambda i: (i, 0)),
        out_shape=jax.ShapeDtypeStruct((t, GM_INNER), BF16),
        compiler_params=_cparams("parallel"),
        name="gmlp",
    )(proj, proj, ln_g.reshape(1, GM_INNER), ln_b.reshape(1, GM_INNER), ws.astype(BF16), bst)


def _rope_norm(x, gain, cosf, sinf):
    ms = jnp.mean(x * x, axis=-1, keepdims=True)
    xn = x * lax.rsqrt(ms + NORM_EPS) * gain
    return xn * cosf + pltpu.roll(xn, ATT_HEAD_DIM // 2, axis=1) * sinf


def _qprep_body(x_ref, g_ref, cos_ref, sin_ref, o_ref):
    for r in range(ATT_REP):
        x = x_ref[:, r * ATT_HEAD_DIM:(r + 1) * ATT_HEAD_DIM].astype(F32)
        y = _rope_norm(x, g_ref[...], cos_ref[...], sin_ref[...]) * QK_SCALE_LOG2E
        o_ref[r] = y.T.astype(o_ref.dtype)


def _kprep_body(x_ref, v_ref, g_ref, cos_ref, sin_ref, o_ref, vt_ref, mean_ref):
    tb = MOBA_BLOCK
    lane = lax.broadcasted_iota(I32, (tb, LANES), 1)
    for s in range(KPREP_BLOCKS):
        rows = slice(s * tb, (s + 1) * tb)
        y = _rope_norm(x_ref[rows, :].astype(F32), g_ref[...], cos_ref[rows, :], sin_ref[rows, :])
        block = pl.program_id(0) * KPREP_BLOCKS + s
        o_ref[rows, :ATT_HEAD_DIM] = y.astype(o_ref.dtype)
        o_ref[rows, ATT_HEAD_DIM:] = jnp.where(lane == block, 1.0, 0.0).astype(o_ref.dtype)
        mean_ref[s] = jnp.mean(y, axis=0, keepdims=True)
        vt_ref[s, :ATT_HEAD_DIM, :] = v_ref[rows, :].astype(F32).T.astype(vt_ref.dtype)
        vt_ref[s, ATT_HEAD_DIM:, :] = jnp.ones((ATT_VT_ROWS - ATT_HEAD_DIM, tb), vt_ref.dtype)


def qk_prep_call(qkv, q_norm, k_norm, cosf, sinf):
    t = qkv.shape[0]
    tb = MOBA_BLOCK
    gw = ATT_REP * ATT_HEAD_DIM
    common = [pl.BlockSpec((1, ATT_HEAD_DIM), lambda i, h: (0, 0)),
              pl.BlockSpec((tb, ATT_HEAD_DIM), lambda i, h: (i, 0)),
              pl.BlockSpec((tb, ATT_HEAD_DIM), lambda i, h: (i, 0))]
    qt = pl.pallas_call(
        _qprep_body,
        grid=(t // tb, ATT_KV_HEADS),
        in_specs=[pl.BlockSpec((tb, gw), lambda i, h: (i, h))] + common,
        out_specs=pl.BlockSpec((ATT_REP, ATT_HEAD_DIM, tb), lambda i, h: (h, 0, i)),
        out_shape=jax.ShapeDtypeStruct((ATT_HEADS, ATT_HEAD_DIM, t), BF16),
        compiler_params=_cparams("parallel", "parallel"),
        name="q_prep",
    )(qkv, q_norm.reshape(1, ATT_HEAD_DIM), cosf, sinf)
    vcol = (ATT_Q_W + ATT_KV_W) // ATT_HEAD_DIM
    kb = KPREP_BLOCKS
    k, vt, kmean = pl.pallas_call(
        _kprep_body,
        grid=(t // (kb * tb), ATT_KV_HEADS),
        in_specs=[pl.BlockSpec((kb * tb, ATT_HEAD_DIM), lambda i, h: (i, ATT_HEADS + h)),
                  pl.BlockSpec((kb * tb, ATT_HEAD_DIM), lambda i, h: (i, vcol + h)),
                  pl.BlockSpec((1, ATT_HEAD_DIM), lambda i, h: (0, 0)),
                  pl.BlockSpec((kb * tb, ATT_HEAD_DIM), lambda i, h: (i, 0)),
                  pl.BlockSpec((kb * tb, ATT_HEAD_DIM), lambda i, h: (i, 0))],
        out_specs=[pl.BlockSpec((kb * tb, ATT_KAUG_W), lambda i, h: (i, h)),
                   pl.BlockSpec((None, kb, ATT_VT_ROWS, tb), lambda i, h: (h, i, 0, 0)),
                   pl.BlockSpec((kb, None, 1, ATT_HEAD_DIM), lambda i, h: (i, h, 0, 0))],
        out_shape=[jax.ShapeDtypeStruct((t, ATT_KV_HEADS * ATT_KAUG_W), BF16),
                   jax.ShapeDtypeStruct((ATT_KV_HEADS, t // tb, ATT_VT_ROWS, tb), BF16),
                   jax.ShapeDtypeStruct((t // tb, ATT_KV_HEADS, 1, ATT_HEAD_DIM), F32)],
        compiler_params=_cparams("parallel", "parallel"),
        name="k_prep",
    )(qkv, qkv, k_norm.reshape(1, ATT_HEAD_DIM), cosf, sinf)
    return qt, k, vt, kmean


def _attn_body(qt_ref, k_ref, vt_ref, km_ref, *rest, nb, n_side):
    o_ref = rest[n_side]
    acc_ref, qa_ref, s_ref = rest[2 * n_side + 1:]
    _side_cast(rest[:n_side], rest[n_side + 1:2 * n_side + 1])
    qblk = pl.program_id(1)
    blk = MOBA_BLOCK
    hd = ATT_HEAD_DIM
    n_blocks = k_ref.shape[0] // blk
    nbp = (n_blocks + 7) // 8 * 8
    own0 = pl.multiple_of(qblk * blk, blk)
    km = km_ref[...]
    km_hi = km.astype(BF16)
    km_lo = (km - km_hi.astype(F32)).astype(BF16)
    bid = lax.broadcasted_iota(I32, (nbp, blk), 0)
    causal = lax.broadcasted_iota(I32, (blk, blk), 0) <= lax.broadcasted_iota(I32, (blk, blk), 1)

    gates, own_scores = [], []
    for r in range(ATT_REP):
        qt = qt_ref[r]
        gates.append((jnp.dot(km_hi, qt, preferred_element_type=F32)
                      + jnp.dot(km_lo, qt, preferred_element_type=F32))[:nbp])
    for r in range(ATT_REP):
        own_scores.append(jnp.dot(k_ref[pl.ds(own0, blk), :hd], qt_ref[r],
                                  preferred_element_type=F32))
    for r in range(ATT_REP):
        gate = jnp.where(bid < qblk, gates[r], -jnp.inf)
        picked = bid < 0
        for kk in range(MOBA_TOPK):
            mx = jnp.max(gate, axis=0, keepdims=True)
            idx = jnp.min(jnp.where(gate == mx, bid, LANES), axis=0, keepdims=True)
            hit = bid == idx
            picked = picked | (hit & (kk < qblk))
            gate = jnp.where(hit, -jnp.inf, gate)
        bias = jnp.where(picked, 0.0, MASK_BIAS).astype(BF16)
        qa_ref[r, :hd, :] = qt_ref[r]
        qa_ref[r, hd:hd + nbp, :] = bias
        if nbp < LANES:
            qa_ref[r, hd + nbp:, :] = jnp.zeros((LANES - nbp, blk), BF16)

    def blocks_of(jj):
        return [jnp.minimum(jj * nb + u, n_blocks - 1) for u in range(nb)]

    def head_scores(r, blocks):
        return [jnp.dot(k_ref[pl.ds(pl.multiple_of(ja * blk, blk), blk), :], qa_ref[r],
                        preferred_element_type=F32) for ja in blocks]

    ahead = ATT_REP - 1
    for r in range(ahead):
        for u, su in enumerate(head_scores(r, blocks_of(0))):
            s_ref[r * nb + u] = su

    m0 = []
    for r in range(ATT_REP):
        s = jnp.where(causal, own_scores[r], -jnp.inf)
        m = jnp.max(s, axis=0, keepdims=True)
        p = jnp.exp2(s - m)
        m0.append(m)
        acc_ref[r] = jnp.dot(vt_ref[qblk], p.astype(BF16), preferred_element_type=F32)

    def past_blocks(jj, ms):
        blocks = blocks_of(jj)
        blocks_next = blocks_of(jj + 1)
        vts = [vt_ref[ja] for ja in blocks]
        m_out = []
        in_flight = {}
        for r in range(ATT_REP):
            scores = in_flight.pop(r) if r in in_flight else [s_ref[r * nb + u] for u in range(nb)]
            r_req = r + ahead
            if r_req < ATT_REP:
                in_flight[r_req] = head_scores(r_req, blocks)
            else:
                for u, su in enumerate(head_scores(r_req - ATT_REP, blocks_next)):
                    s_ref[(r_req - ATT_REP) * nb + u] = su
            m_new = ms[r]
            for su in scores:
                m_new = jnp.maximum(m_new, jnp.max(su, axis=0, keepdims=True))
            alpha = jnp.exp2(ms[r] - m_new)
            pv = None
            for u in range(nb):
                pu = jnp.exp2(scores[u] - m_new).astype(BF16)
                d = jnp.dot(vts[u], pu, preferred_element_type=F32)
                pv = d if pv is None else pv + d
            acc_ref[r] = alpha * acc_ref[r] + pv
            m_out.append(m_new)
        return tuple(m_out)

    lax.fori_loop(0, (qblk + nb - 1) // nb, past_blocks, tuple(m0))

    for r in range(ATT_REP):
        out = acc_ref[r, :hd, :] / acc_ref[r, hd:hd + 1, :]
        o_ref[:, r * hd:(r + 1) * hd] = out.T.astype(o_ref.dtype)


def attention_call(qt, k, vt, kmean, side_casts=()):
    t = k.shape[0]
    tq = MOBA_BLOCK
    n_blocks = t // MOBA_BLOCK
    n_tiles = t // tq
    km = jnp.transpose(kmean.reshape(n_blocks, ATT_KV_HEADS, ATT_HEAD_DIM), (1, 0, 2))
    km = jnp.pad(km, ((0, 0), (0, LANES - n_blocks), (0, 0)))
    s_in, s_out, s_shapes, s_args = _side_cast_plumbing(side_casts, ATT_KV_HEADS * n_tiles,
                                                        lambda g, i: g * n_tiles + i)
    outs = pl.pallas_call(
        functools.partial(_attn_body, nb=ATT_NB, n_side=len(side_casts)),
        grid=(ATT_KV_HEADS, n_tiles),
        in_specs=[pl.BlockSpec((ATT_REP, ATT_HEAD_DIM, tq), lambda g, i: (g, 0, i)),
                  pl.BlockSpec((t, ATT_KAUG_W), lambda g, i: (0, g)),
                  pl.BlockSpec((None, n_blocks, ATT_VT_ROWS, MOBA_BLOCK), lambda g, i: (g, 0, 0, 0)),
                  pl.BlockSpec((None, LANES, ATT_HEAD_DIM), lambda g, i: (g, 0, 0))] + s_in,
        out_specs=[pl.BlockSpec((tq, ATT_REP * ATT_HEAD_DIM), lambda g, i: (i, g))] + s_out,
        out_shape=[jax.ShapeDtypeStruct((t, ATT_Q_W), BF16)] + s_shapes,
        scratch_shapes=[pltpu.VMEM((ATT_REP, ATT_VT_ROWS, tq), F32),
                        pltpu.VMEM((ATT_REP, ATT_KAUG_W, tq), BF16),
                        pltpu.VMEM(((ATT_REP - 1) * ATT_NB, MOBA_BLOCK, tq), F32)],
        compiler_params=_cparams("arbitrary", "arbitrary"),
        name="moba_attention",
    )(qt, k, vt, km, *s_args)
    return tuple(outs) if side_casts else outs[0]


def _router_body(h_ref, g_ref, wr_ref, wrl_ref, br_ref, xn_ref, re_ref, rw_ref, cnt_ref):
    h = h_ref[...]
    ms = jnp.mean(h * h, axis=-1, keepdims=True)
    xn = h * lax.rsqrt(ms + NORM_EPS) * g_ref[...]
    xn_ref[...] = _pack_bf16_pairs(xn)
    xh = xn.astype(BF16)
    xl = (xn - xh.astype(F32)).astype(BF16)
    logits = (jnp.dot(xh, wr_ref[...], preferred_element_type=F32)
              + jnp.dot(xh, wrl_ref[...], preferred_element_type=F32)
              + jnp.dot(xl, wr_ref[...], preferred_element_type=F32)) + br_ref[...]
    lane = lax.broadcasted_iota(I32, logits.shape, 1)
    neg = -jnp.inf

    is_g = lane < MOE_GROUPS
    gl = jnp.where(is_g, logits, neg)
    ge = jnp.exp(gl - jnp.max(gl, axis=1, keepdims=True))
    gp = ge / jnp.sum(ge, axis=1, keepdims=True)
    g_w = jnp.max(gp, axis=1, keepdims=True)
    g_idx = jnp.min(jnp.where(is_g & (gp == g_w), lane, LANES), axis=1, keepdims=True)

    e_lo = MOE_GROUPS + MOE_EPG * g_idx
    is_e = (lane >= e_lo) & (lane < e_lo + MOE_EPG)
    el = jnp.where(is_e, logits, neg)
    ee = jnp.exp(el - jnp.max(el, axis=1, keepdims=True))
    ep = jnp.where(is_e, ee / jnp.sum(ee, axis=1, keepdims=True), -1.0)
    p1 = jnp.max(ep, axis=1, keepdims=True)
    i1 = jnp.min(jnp.where(ep == p1, lane, LANES), axis=1, keepdims=True)
    ep2 = jnp.where(lane == i1, -1.0, ep)
    p2 = jnp.max(ep2, axis=1, keepdims=True)
    i2 = jnp.min(jnp.where(ep2 == p2, lane, LANES), axis=1, keepdims=True)
    den = p1 + p2
    w1 = g_w * (p1 / den)
    w2 = g_w * (p2 / den)
    rw_ref[...] = jnp.where(lane == 0, w1, jnp.where(lane == 1, w2, 0.0))

    @pl.when(pl.program_id(0) == 0)
    def _():
        cnt_ref[...] = jnp.zeros_like(cnt_ref)

    e1, e2 = i1 - MOE_GROUPS, i2 - MOE_GROUPS
    oh1 = (lane == e1).astype(F32)
    oh2 = (lane == e2).astype(F32)
    tm = h.shape[0]
    earlier = (lax.broadcasted_iota(I32, (tm, tm), 1) < lax.broadcasted_iota(I32, (tm, tm), 0)).astype(BF16)
    before = jnp.dot(earlier, (oh1 + oh2).astype(BF16), preferred_element_type=F32) + cnt_ref[...]
    r1 = jnp.sum(oh1 * before, axis=1, keepdims=True).astype(I32)
    r2 = jnp.sum(oh2 * (before + oh1), axis=1, keepdims=True).astype(I32)
    cnt_ref[...] = cnt_ref[...] + jnp.sum(oh1 + oh2, axis=0, keepdims=True)
    re_ref[...] = jnp.where(lane == 0, e1, jnp.where(lane == 1, e2, jnp.where(lane == 2, r1,
                                                                               jnp.where(lane == 3, r2, 0))))


def router_call(h, gain, w_group, b_group, w_expert, b_expert):
    t, d = h.shape
    nr = MOE_GROUPS + MOE_EXPERTS
    wr = jnp.zeros((d, LANES), F32).at[:, :MOE_GROUPS].set(w_group).at[:, MOE_GROUPS:nr].set(w_expert)
    br = jnp.zeros((1, LANES), F32).at[0, :MOE_GROUPS].set(b_group).at[0, MOE_GROUPS:nr].set(b_expert)
    wr_hi = wr.astype(BF16)
    wr_lo = (wr - wr_hi.astype(F32)).astype(BF16)
    return pl.pallas_call(
        _router_body,
        grid=(t // NORM_TM,),
        in_specs=[pl.BlockSpec((NORM_TM, d), lambda i: (i, 0)),
                  pl.BlockSpec((1, d), lambda i: (0, 0)),
                  pl.BlockSpec((d, LANES), lambda i: (0, 0)),
                  pl.BlockSpec((d, LANES), lambda i: (0, 0)),
                  pl.BlockSpec((1, LANES), lambda i: (0, 0))],
        out_specs=[pl.BlockSpec((NORM_TM, d // 2), lambda i: (i, 0)),
                   pl.BlockSpec((NORM_TM, LANES), lambda i: (i, 0)),
                   pl.BlockSpec((NORM_TM, LANES), lambda i: (i, 0)),
                   pl.BlockSpec((1, LANES), lambda i: (0, 0))],
        out_shape=[jax.ShapeDtypeStruct((t, d // 2), jnp.uint32),
                   jax.ShapeDtypeStruct((t, LANES), I32),
                   jax.ShapeDtypeStruct((t, LANES), F32),
                   jax.ShapeDtypeStruct((1, LANES), F32)],
        compiler_params=_cparams("arbitrary"),
        name="norm_router",
    )(h, gain.reshape(1, d), wr_hi, wr_lo, br)


def _row_copy(src_ref, dst_ref, sem, src_row, dst_row):
    return pltpu.make_async_copy(src_ref.at[pl.ds(src_row, 1), :], dst_ref.at[pl.ds(dst_row, 1), :], sem)


def _start_row_gather(idx_ref, src_ref, dst_ref, sem, n):
    for r in range(n):
        _row_copy(src_ref, dst_ref, sem, idx_ref[0, r], r).start(priority=r % 2)


def _wait_row_gather(src_ref, dst_ref, sem, n):
    for r in range(n):
        _row_copy(src_ref, dst_ref, sem, 0, r).wait()


def _ffn_body(be_ref, nv_ref, tok0_ref, tokn_ref, x_hbm, wg_ref, wu_ref, wd_ref, o_ref, xbuf, sems):
    b = pl.program_id(0)
    nv = nv_ref[0]
    last = pl.num_programs(0) - 1
    tm = o_ref.shape[0]
    slot = b % 2
    valid = b < nv

    @pl.when(b == 0)
    def _():
        _start_row_gather(tok0_ref, x_hbm, xbuf.at[0], sems.at[0], tm)

    @pl.when(b <= nv)
    def _():
        _wait_row_gather(x_hbm, xbuf.at[slot], sems.at[slot], tm)

    @pl.when(valid)
    def _():
        _start_row_gather(tokn_ref, x_hbm, xbuf.at[1 - slot], sems.at[1 - slot], tm)
        x = _unpack_bf16_pairs(xbuf[slot])
        gate = jnp.dot(x, wg_ref[...], preferred_element_type=F32)
        up = jnp.dot(x, wu_ref[...], preferred_element_type=F32)
        hid = (_silu(gate) * up).astype(BF16)
        o_ref[...] = _pack_bf16_pairs(jnp.dot(hid, wd_ref[...], preferred_element_type=F32))

    @pl.when(jnp.logical_not(valid))
    def _():
        o_ref[...] = jnp.zeros_like(o_ref)

    @pl.when(valid & (b == last))
    def _():
        _wait_row_gather(x_hbm, xbuf.at[1 - slot], sems.at[1 - slot], tm)


def ffn_call(xn, row_token, blk_expert, n_valid, w_gate, w_up, w_down, layer):
    d, ff = w_gate.shape[2], w_gate.shape[3]
    tm = MOE_TM
    n_rows = row_token.shape[0]
    n_blk = n_rows // tm

    def w_map(b, be, nv):
        return (layer, be[jnp.minimum(b, nv[0] - 1)], 0, 0)

    tok_spec = functools.partial(pl.BlockSpec, (None, 1, tm), memory_space=pltpu.SMEM)
    grid_spec = pltpu.PrefetchScalarGridSpec(
        num_scalar_prefetch=2,
        grid=(n_blk,),
        in_specs=[tok_spec(index_map=lambda b, be, nv: (0, 0, 0)),
                  tok_spec(index_map=lambda b, be, nv: (jnp.minimum(b + 1, n_blk - 1), 0, 0)),
                  pl.BlockSpec(memory_space=pl.ANY),
                  pl.BlockSpec((None, None, d, ff), w_map),
                  pl.BlockSpec((None, None, d, ff), w_map),
                  pl.BlockSpec((None, None, ff, d), w_map)],
        out_specs=pl.BlockSpec((tm, d // 2), lambda b, be, nv: (b, 0)),
        scratch_shapes=[pltpu.VMEM((2, tm, xn.shape[1]), xn.dtype), pltpu.SemaphoreType.DMA((2,))],
    )
    tok = row_token.reshape(n_blk, 1, tm)
    return pl.pallas_call(
        _ffn_body,
        grid_spec=grid_spec,
        out_shape=jax.ShapeDtypeStruct((n_rows, d // 2), jnp.uint32),
        compiler_params=_cparams("arbitrary"),
        name="moe_ffn",
    )(blk_expert, n_valid, tok, tok, xn, w_gate, w_up, w_down)


def _combine_body(pos_ref, posn_ref, y_ref, h_ref, rw_ref, g_ref, *out_refs_and_scratch, with_norm):
    if with_norm:
        ho_ref, xn_ref, buf_ref, sems = out_refs_and_scratch
    else:
        ho_ref, buf_ref, sems = out_refs_and_scratch
    tb = ho_ref.shape[0]
    n = MOE_TOPK * tb
    i = pl.program_id(0)
    slot = i % 2

    @pl.when(i == 0)
    def _():
        _start_row_gather(pos_ref, y_ref, buf_ref.at[0], sems.at[0], n)

    _wait_row_gather(y_ref, buf_ref.at[slot], sems.at[slot], n)

    @pl.when(i + 1 < pl.num_programs(0))
    def _():
        _start_row_gather(posn_ref, y_ref, buf_ref.at[1 - slot], sems.at[1 - slot], n)

    rw = rw_ref[...]
    y = _unpack_bf16_pairs(buf_ref[slot], F32)
    ffn = rw[:, 0:1] * y[0:tb, :] + rw[:, 1:2] * y[tb:2 * tb, :]
    hn = h_ref[...] + ffn
    ho_ref[...] = hn
    if with_norm:
        ms = jnp.mean(hn * hn, axis=-1, keepdims=True)
        xn_ref[...] = (hn * lax.rsqrt(ms + NORM_EPS) * g_ref[...]).astype(xn_ref.dtype)


def combine_call(y, pos, h, route_w, next_gain):
    t, d = h.shape
    tb = NORM_TM // 2
    n_steps = t // tb
    with_norm = next_gain is not None
    gain = (next_gain if with_norm else jnp.ones((d,), F32)).reshape(1, d)
    out_specs = [pl.BlockSpec((tb, d), lambda i: (i, 0))]
    out_shape = [jax.ShapeDtypeStruct((t, d), F32)]
    if with_norm:
        out_specs.append(pl.BlockSpec((tb, d), lambda i: (i, 0)))
        out_shape.append(jax.ShapeDtypeStruct((t, d), BF16))
    pos_spec = functools.partial(pl.BlockSpec, (None, 1, MOE_TOPK * tb), memory_space=pltpu.SMEM)
    outs = pl.pallas_call(
        functools.partial(_combine_body, with_norm=with_norm),
        grid=(n_steps,),
        in_specs=[pos_spec(index_map=lambda i: (i, 0, 0)),
                  pos_spec(index_map=lambda i: (jnp.minimum(i + 1, n_steps - 1), 0, 0)),
                  pl.BlockSpec(memory_space=pl.ANY),
                  pl.BlockSpec((tb, d), lambda i: (i, 0)),
                  pl.BlockSpec((tb, LANES), lambda i: (i, 0)),
                  pl.BlockSpec((1, d), lambda i: (0, 0))],
        out_specs=out_specs,
        out_shape=out_shape,
        scratch_shapes=[pltpu.VMEM((2, MOE_TOPK * tb, d // 2), jnp.uint32), pltpu.SemaphoreType.DMA((2,))],
        compiler_params=_cparams("arbitrary"),
        name="moe_combine",
    )(pos, pos, y, h, route_w, gain)
    return (outs[0], outs[1]) if with_norm else (outs[0], None)


def _dispatch_tables(route_e, counts, t):
    tm = MOE_TM
    n_assign = MOE_TOPK * t
    expert = route_e[:, :MOE_TOPK].reshape(n_assign)
    rank = route_e[:, MOE_TOPK:2 * MOE_TOPK].reshape(n_assign)
    counts = counts[0, :MOE_EXPERTS].astype(I32)
    padded = (counts + tm - 1) // tm * tm
    pend = jnp.cumsum(padded)
    pstart = pend - padded
    onehot = expert[:, None] == jnp.arange(MOE_EXPERTS, dtype=I32)[None, :]
    dest = jnp.sum(jnp.where(onehot, pstart[None, :], 0), axis=1) + rank
    n_rows = n_assign + MOE_EXPERTS * tm
    token = jnp.arange(n_assign, dtype=I32) // MOE_TOPK
    row_token = jnp.zeros((n_rows,), I32).at[dest].set(token)
    n_blk = n_rows // tm
    blk_start = jnp.arange(n_blk, dtype=I32) * tm
    blk_expert = jnp.minimum(jnp.sum((pend[None, :] <= blk_start[:, None]).astype(I32), axis=1),
                             MOE_EXPERTS - 1)
    n_valid = (pend[-1] // tm).astype(I32).reshape(1)
    tb = NORM_TM // 2
    pos = dest.reshape(t // tb, tb, MOE_TOPK).transpose(0, 2, 1).reshape(t // tb, 1, MOE_TOPK * tb)
    return row_token, blk_expert, n_valid, pos.astype(I32)


def moe_layer(h, gain, w_group, b_group, w_expert, b_expert, expert_w16, next_gain):
    t, d = h.shape
    xn, route_e, route_w, counts = router_call(h, gain, w_group, b_group, w_expert, b_expert)
    row_token, blk_expert, n_valid, pos = _dispatch_tables(route_e, counts, t)
    g16, u16, d16 = expert_w16
    y = ffn_call(xn, row_token, blk_expert, n_valid, g16.reshape(1, MOE_EXPERTS, d, -1),
                 u16.reshape(1, MOE_EXPERTS, d, -1), d16.reshape(1, MOE_EXPERTS, -1, d), 0)
    return combine_call(y, pos, h, route_w, next_gain)


def hybrid_layer(h, xn, w_in16_all, j, w_out16, conv_w, conv_b, dt_bias, a_log, d_skip, ssd_norm, ln_g, ln_b,
                 ws, bs, expert_w, layer, next_mixer_w):
    w_in16 = w_in16_all[j]
    wg, wu, wd = expert_w
    hp = SSD_HEADS_PER_GROUP
    w_dt = jnp.zeros((D_MODEL, SSD_GROUPS, LANES), BF16).at[:, :, :hp].set(
        w_in16[:, OFF_DT:OFF_U].reshape(D_MODEL, SSD_GROUPS, hp)).reshape(D_MODEL, SSD_GROUPS * LANES)
    proj_a, g16 = matmul_call([xn], w_in16_all, out_dtype=BF16, n_cols=OFF_DT, layer=j,
                              side_casts=[(wg, layer)], name="hyb_in_proj_ssd")
    proj_b, u16 = matmul_call([xn], w_in16[:, OFF_U:], out_dtype=BF16, side_casts=[(wu, layer)],
                              name="hyb_in_proj_gmlp")
    dt_raw = matmul_call([xn], w_dt, out_dtype=F32, tn=SSD_GROUPS * LANES, name="hyb_dt_proj")
    dtt_raw = dt_raw.reshape(-1, SSD_GROUPS, LANES)[:, :, :hp].reshape(-1, SSD_HEADS).T
    next16 = None
    if next_mixer_w is None:
        y_a = ssd_call(proj_a, dt_raw, dtt_raw, conv_w, conv_b, dt_bias, a_log, d_skip, ssd_norm)
    else:
        stacks, idx = next_mixer_w
        y_a, *next16 = ssd_call(proj_a, dt_raw, dtt_raw, conv_w, conv_b, dt_bias, a_log, d_skip, ssd_norm,
                                side_casts=[(s, idx) for s in stacks])
    y_b = gmlp_call(proj_b, ln_g, ln_b, ws, bs)
    h, d16 = matmul_call([y_a, y_b], w_out16, res=h, side_casts=[(wd, layer)], name="hyb_out_proj")
    return h, (g16, u16, d16), next16


def moba_layer(h, xn, w_qkv16, w_out16, q_norm, k_norm, cosf, sinf, expert_w, layer, next_mixer_w):
    wg, wu, wd = expert_w
    qkv, g16, u16 = matmul_call([xn], w_qkv16, out_dtype=BF16, side_casts=[(wg, layer), (wu, layer)],
                                name="att_qkv_proj")
    qt, k, vt, kmean = qk_prep_call(qkv, q_norm, k_norm, cosf, sinf)
    next16 = None
    if next_mixer_w is None:
        o = attention_call(qt, k, vt, kmean)
    else:
        stacks, idx = next_mixer_w
        o, *next16 = attention_call(qt, k, vt, kmean, side_casts=[(s, idx) for s in stacks])
    h, d16 = matmul_call([o], w_out16, res=h, side_casts=[(wd, layer)], name="att_out_proj")
    return h, (g16, u16, d16), next16


def _rope_tables(t):
    inv = 1.0 / (ROPE_THETA ** (jnp.arange(0, ATT_HEAD_DIM, 2, dtype=F32) / ATT_HEAD_DIM))
    ang = jnp.arange(t, dtype=F32)[:, None] * inv[None, :]
    cos, sin = jnp.cos(ang), jnp.sin(ang)
    return jnp.concatenate([cos, cos], axis=1), jnp.concatenate([-sin, sin], axis=1)


def kernel(x, norm_mix, norm_ffn, hyb_w_in, ssd_conv_w, ssd_conv_b, ssd_dt_bias, ssd_a_log, ssd_d, ssd_norm, gm_ln_g, gm_ln_b, gm_ws, gm_bs, hyb_w_out, att_w_qkv, att_q_norm, att_k_norm, att_w_out, moe_w_group, moe_b_group, moe_w_expert, moe_b_expert, moe_w_gate, moe_w_up, moe_w_down):
    bsz, t, d = x.shape
    depth = norm_mix.shape[0]
    cosf, sinf = _rope_tables(t)
    ff = moe_w_gate.shape[-1]
    expert_w = (moe_w_gate.reshape(depth, MOE_EXPERTS * d, ff), moe_w_up.reshape(depth, MOE_EXPERTS * d, ff),
                moe_w_down.reshape(depth, MOE_EXPERTS * ff, d))
    outs = []
    for b in range(bsz):
        h = x[b]
        xn = rmsnorm_call(h, norm_mix[0])
        hyb_w_in16 = hyb_w_in.astype(BF16)
        mixer16 = [hyb_w_out[0].astype(BF16)]
        for layer in range(depth):
            j = layer // 2
            if layer % 2 == 0:
                next_w = ((att_w_qkv, att_w_out), j) if layer + 1 < depth else None
                h, expert_w16, mixer16 = hybrid_layer(
                    h, xn, hyb_w_in16, j, mixer16[0], ssd_conv_w[j], ssd_conv_b[j], ssd_dt_bias[j], ssd_a_log[j],
                    ssd_d[j], ssd_norm[j], gm_ln_g[j], gm_ln_b[j], gm_ws[j], gm_bs[j], expert_w, layer, next_w)
            else:
                next_w = ((hyb_w_out,), j + 1) if layer + 1 < depth else None
                h, expert_w16, mixer16 = moba_layer(h, xn, mixer16[0], mixer16[1], att_q_norm[j], att_k_norm[j],
                                                    cosf, sinf, expert_w, layer, next_w)
            next_gain = norm_mix[layer + 1] if layer + 1 < depth else None
            h, xn = moe_layer(h, norm_ffn[layer], moe_w_group[layer], moe_b_group[layer],
                              moe_w_expert[layer], moe_b_expert[layer], expert_w16, next_gain)
        outs.append(h)
    return jnp.stack(outs, axis=0)
```

```python
import functools

import jax
import jax.numpy as jnp
from jax import lax
from jax.experimental import pallas as pl
from jax.experimental.pallas import tpu as pltpu

F32 = jnp.float32
BF16 = jnp.bfloat16
I32 = jnp.int32

D_MODEL = 4096
NORM_EPS = 1e-6

SSD_INNER = 2048
SSD_HEAD_DIM = 64
SSD_HEADS = 32
SSD_GROUPS = 4
SSD_STATE = 128
SSD_CONV = 4
SSD_CHUNK = 256
SSD_BC = SSD_GROUPS * SSD_STATE
SSD_CONV_CH = SSD_INNER + 2 * SSD_BC
SSD_GROUP_W = SSD_INNER // SSD_GROUPS
SSD_HEADS_PER_GROUP = SSD_HEADS // SSD_GROUPS

GM_INNER = 2048
GM_CHUNK = 128
GM_GROUPS = 16
GM_GROUP_DIM = 128

OFF_XBC = SSD_INNER
OFF_DT = OFF_XBC + SSD_CONV_CH
OFF_U = OFF_DT + SSD_HEADS
OFF_V = OFF_U + GM_INNER

ATT_HEAD_DIM = 128
ATT_HEADS = 32
ATT_KV_HEADS = 8
ATT_REP = ATT_HEADS // ATT_KV_HEADS
ATT_Q_W = ATT_HEADS * ATT_HEAD_DIM
ATT_KV_W = ATT_KV_HEADS * ATT_HEAD_DIM
ROPE_THETA = 10000.0
MOBA_BLOCK = 256
MOBA_TOPK = 3

MOE_GROUPS = 4
MOE_EPG = 4
MOE_EXPERTS = 16
MOE_TOPK = 2
MOE_FF = 768

LANES = 128
VMEM_LIMIT_BYTES = 56 * 1024 * 1024

MM_TM = 512
MM_TN = 1024
NORM_TM = 256
ATT_KAUG_W = ATT_HEAD_DIM + LANES
ATT_VT_ROWS = ATT_HEAD_DIM
MASK_BIAS = -1e30
KPREP_BLOCKS = 2
ATT_NB = 2
QK_SCALE_LOG2E = ATT_HEAD_DIM ** -0.5 * 1.4426950408889634
MOE_TM = 256


def _cparams(*sem, flags=None):
    return pltpu.CompilerParams(dimension_semantics=sem, vmem_limit_bytes=VMEM_LIMIT_BYTES, flags=flags)


def _silu(x):
    return x / (1.0 + jnp.exp(-x))


def _softplus(x):
    return jnp.maximum(x, 0.0) + jnp.log1p(jnp.exp(-jnp.abs(x)))


def _gelu_erf(x):
    return 0.5 * x * (1.0 + lax.erf(x * (2.0 ** -0.5)))


def _pack_bf16_pairs(x):
    c = x.shape[1] // 2
    bits = pltpu.bitcast(x.astype(BF16).astype(F32), jnp.uint32)
    return (bits[:, :c] >> 16) | bits[:, c:]


def _unpack_bf16_pairs(w, dtype=BF16):
    lo = pltpu.bitcast(w << 16, F32)
    hi = pltpu.bitcast(w & jnp.uint32(0xFFFF0000), F32)
    return jnp.concatenate([lo, hi], axis=1).astype(dtype)


def _dot_f32_by_01(x, m01, f32_on_right=False):
    hi = x.astype(BF16)
    r1 = x - hi.astype(F32)
    mid = r1.astype(BF16)
    lo = (r1 - mid.astype(F32)).astype(BF16)
    if f32_on_right:
        return sum(jnp.dot(m01, t, preferred_element_type=F32) for t in (hi, mid, lo))
    return sum(jnp.dot(t, m01, preferred_element_type=F32) for t in (hi, mid, lo))


def _rmsnorm_body(x_ref, g_ref, o_ref):
    x = x_ref[...]
    ms = jnp.mean(x * x, axis=-1, keepdims=True)
    o_ref[...] = (x * lax.rsqrt(ms + NORM_EPS) * g_ref[...]).astype(o_ref.dtype)


def rmsnorm_call(x, g):
    t, d = x.shape
    return pl.pallas_call(
        _rmsnorm_body,
        grid=(t // NORM_TM,),
        in_specs=[pl.BlockSpec((NORM_TM, d), lambda i: (i, 0)),
                  pl.BlockSpec((1, d), lambda i: (0, 0))],
        out_specs=pl.BlockSpec((NORM_TM, d), lambda i: (i, 0)),
        out_shape=jax.ShapeDtypeStruct((t, d), BF16),
        compiler_params=_cparams("parallel"),
        name="rmsnorm",
    )(x, g.reshape(1, d))


def _matmul_body(*refs, n_parts, has_res, n_side):
    a_refs = refs[:n_parts]
    w_refs = refs[n_parts:2 * n_parts]
    n_in = 2 * n_parts + int(has_res) + n_side
    o_ref = refs[n_in]
    acc = jnp.dot(a_refs[0][...], w_refs[0][...], preferred_element_type=F32)
    for p in range(1, n_parts):
        acc = acc + jnp.dot(a_refs[p][...], w_refs[p][...], preferred_element_type=F32)
    if has_res:
        acc = acc + refs[2 * n_parts][...]
    o_ref[...] = acc.astype(o_ref.dtype)
    _side_cast(refs[n_in - n_side:n_in], refs[n_in + 1:n_in + 1 + n_side])


def _side_blocks(rows, n_steps):
    for nsb in range(min(n_steps, rows // 16), 0, -1):
        if rows % nsb == 0 and (rows // nsb) % 16 == 0:
            return nsb
    return 1


def _side_cast_plumbing(side_casts, n_steps, step_of):
    in_specs, out_specs, out_shapes, args = [], [], [], []
    for src, idx in side_casts:
        _, rows, cols = src.shape
        nsb = _side_blocks(rows, n_steps)

        def slab(*g, nsb=nsb):
            return jnp.minimum(step_of(*g), nsb - 1)

        in_specs.append(pl.BlockSpec((None, rows // nsb, cols),
                                     lambda *g, slab=slab, idx=idx: (idx, slab(*g), 0)))
        out_specs.append(pl.BlockSpec((rows // nsb, cols), lambda *g, slab=slab: (slab(*g), 0)))
        out_shapes.append(jax.ShapeDtypeStruct((rows, cols), BF16))
        args.append(src)
    return in_specs, out_specs, out_shapes, args


def _side_cast(in_refs, out_refs):
    for src, dst in zip(in_refs, out_refs):
        dst[...] = src[...].astype(BF16)


def matmul_call(a_parts, w, res=None, out_dtype=F32, tn=MM_TN, n_cols=None, layer=None, side_casts=(),
                name="matmul"):
    n_parts = len(a_parts)
    m, kp = a_parts[0].shape
    n = w.shape[-1] if n_cols is None else n_cols
    tm = MM_TM
    steps_i = m // tm
    n_steps = (n // tn) * steps_i
    in_specs = [pl.BlockSpec((tm, kp), lambda j, i: (i, 0)) for _ in a_parts]
    if layer is None:
        in_specs += [pl.BlockSpec((kp, tn), functools.partial(lambda j, i, p: (p, j), p=p))
                     for p in range(n_parts)]
    else:
        in_specs += [pl.BlockSpec((None, kp, tn), functools.partial(lambda j, i, p: (layer, p, j), p=p))
                     for p in range(n_parts)]
    args = list(a_parts) + [w] * n_parts
    if res is not None:
        in_specs.append(pl.BlockSpec((tm, tn), lambda j, i: (i, j)))
        args.append(res)
    out_specs = [pl.BlockSpec((tm, tn), lambda j, i: (i, j))]
    out_shape = [jax.ShapeDtypeStruct((m, n), out_dtype)]
    s_in, s_out, s_shapes, s_args = _side_cast_plumbing(side_casts, n_steps, lambda j, i: j * steps_i + i)
    in_specs += s_in
    out_specs += s_out
    out_shape += s_shapes
    args += s_args
    outs = pl.pallas_call(
        functools.partial(_matmul_body, n_parts=n_parts, has_res=res is not None, n_side=len(side_casts)),
        grid=(n // tn, m // tm),
        in_specs=in_specs,
        out_specs=out_specs,
        out_shape=out_shape,
        compiler_params=_cparams("arbitrary", "arbitrary"),
        name=name,
    )(*args)
    return tuple(outs) if side_casts else outs[0]


def _ssd_body(z_ref, xs_ref, b_ref, c_ref, dt_ref, dtt_ref,
              wx_ref, bx_ref, wb_ref, bb_ref, wc_ref, bc_ref,
              dtb_ref, dtbt_ref, alog_ref, alogt_ref, dskip_ref, ng_ref, expand_ref, *rest, n_side):
    o_ref = rest[n_side]
    state_ref, tx_ref, tb_ref, tc_ref = rest[2 * n_side + 1:]
    _side_cast(rest[:n_side], rest[n_side + 1:2 * n_side + 1])
    c = pl.program_id(1)
    L = SSD_CHUNK
    HP = SSD_HEADS_PER_GROUP

    @pl.when(c == 0)
    def _():
        state_ref[...] = jnp.zeros_like(state_ref)
        tx_ref[...] = jnp.zeros_like(tx_ref)
        tb_ref[...] = jnp.zeros_like(tb_ref)
        tc_ref[...] = jnp.zeros_like(tc_ref)

    def conv_silu(cur_ref, tail_ref, w_ref, bias_ref):
        cur = cur_ref[...].astype(F32)
        ext = jnp.concatenate([tail_ref[...], cur], axis=0)
        w = w_ref[...]
        acc = bias_ref[...] + w[SSD_CONV - 1:SSD_CONV, :] * cur
        for j in range(1, SSD_CONV):
            acc = acc + w[SSD_CONV - 1 - j:SSD_CONV - j, :] * ext[8 - j:8 - j + L, :]
        tail_ref[...] = cur[L - 8:L, :]
        return _silu(acc)

    xs = conv_silu(xs_ref, tx_ref, wx_ref, bx_ref)
    bm = conv_silu(b_ref, tb_ref, wb_ref, bb_ref)
    cm = conv_silu(c_ref, tc_ref, wc_ref, bc_ref)

    dt = _softplus(dt_ref[...] + dtb_ref[...])
    da = dt * (-jnp.exp(alog_ref[...]))
    dtt = _softplus(dtt_ref[...] + dtbt_ref[...])
    dat = dtt * (-jnp.exp(alogt_ref[...]))

    row = lax.broadcasted_iota(I32, (L, L), 0)
    col = lax.broadcasted_iota(I32, (L, L), 1)
    causal = row >= col
    cs = _dot_f32_by_01(da, causal.astype(BF16), f32_on_right=True)
    cst = _dot_f32_by_01(dat, (row <= col).astype(BF16))
    cs_last = cs[L - 1:L, :]
    to_end = jnp.exp(cs_last - cs)
    ecs = jnp.exp(cs)

    expand = expand_ref[...]
    dt_e = _dot_f32_by_01(dt, expand)
    to_end_e = _dot_f32_by_01(to_end, expand)
    ecs_e = _dot_f32_by_01(ecs, expand)

    x = xs * dt_e
    bm16 = bm.astype(BF16)
    cm16 = cm.astype(BF16)
    cb = lax.dot_general(cm16, bm16, (((1,), (1,)), ((), ())), preferred_element_type=F32)

    prev = state_ref[...]
    y = jnp.dot(cm16, prev.astype(BF16), preferred_element_type=F32) * ecs_e

    x16 = x.astype(BF16)
    heads_per_tile = LANES // SSD_HEAD_DIM
    first_head = lax.broadcasted_iota(I32, (L, LANES), 1) < SSD_HEAD_DIM
    tiles = []
    for tt in range(SSD_GROUP_W // LANES):
        x_tile = x16[:, tt * LANES:(tt + 1) * LANES]
        y_tile = None
        for hh in range(heads_per_tile):
            h = tt * heads_per_tile + hh
            seg = cs[:, h:h + 1] - cst[h:h + 1, :]
            decay = jnp.exp(jnp.where(causal, seg, -jnp.inf))
            m16 = (cb * decay).astype(BF16)
            xh = jnp.where(first_head == (hh == 0), x_tile, jnp.zeros((), BF16))
            d = jnp.dot(m16, xh, preferred_element_type=F32)
            y_tile = d if y_tile is None else y_tile + d
        tiles.append(y_tile)
    y = y + jnp.concatenate(tiles, axis=1)

    xw = (x * to_end_e).astype(BF16)
    state_ref[...] = prev * ecs_e[L - 1:L, :] + jnp.dot(bm.T.astype(BF16), xw,
                                                        preferred_element_type=F32)

    y = y + xs * dskip_ref[...]
    y = y * _silu(z_ref[...].astype(F32))
    ms = jnp.mean(y * y, axis=-1, keepdims=True)
    o_ref[...] = (y * lax.rsqrt(ms + NORM_EPS) * ng_ref[...]).astype(o_ref.dtype)


def ssd_call(proj, dt_raw, dtt_raw, conv_w, conv_b, dt_bias, a_log, d_skip, norm_g, side_casts=()):
    t = proj.shape[0]
    L = SSD_CHUNK
    gw = SSD_GROUP_W
    xs0 = SSD_INNER // gw
    b0 = 2 * SSD_INNER // SSD_STATE
    c0 = b0 + SSD_GROUPS
    hp = SSD_HEADS_PER_GROUP

    dtb = jnp.zeros((SSD_GROUPS, LANES), F32).at[:, :hp].set(dt_bias.reshape(SSD_GROUPS, hp))
    alog = jnp.zeros((SSD_GROUPS, LANES), F32).at[:, :hp].set(a_log.reshape(SSD_GROUPS, hp))
    dtb = dtb.reshape(1, SSD_GROUPS * LANES)
    alog = alog.reshape(1, SSD_GROUPS * LANES)
    dtbt = dt_bias.reshape(SSD_HEADS, 1)
    alogt = a_log.reshape(SSD_HEADS, 1)
    dskip = jnp.repeat(d_skip, SSD_HEAD_DIM).reshape(1, SSD_INNER)
    expand = (jnp.arange(LANES)[:, None] == (jnp.arange(gw)[None, :] // SSD_HEAD_DIM)).astype(BF16)
    cw = conv_w.astype(F32)
    cbias = conv_b.reshape(1, SSD_CONV_CH).astype(F32)
    xb0 = SSD_INNER // SSD_STATE
    cb0 = xb0 + SSD_GROUPS

    in_specs = [
        pl.BlockSpec((L, gw), lambda g, c: (c, g)),
        pl.BlockSpec((L, gw), lambda g, c: (c, xs0 + g)),
        pl.BlockSpec((L, SSD_STATE), lambda g, c: (c, b0 + g)),
        pl.BlockSpec((L, SSD_STATE), lambda g, c: (c, c0 + g)),
        pl.BlockSpec((L, LANES), lambda g, c: (c, g)),
        pl.BlockSpec((hp, L), lambda g, c: (g, c)),
        pl.BlockSpec((SSD_CONV, gw), lambda g, c: (0, g)),
        pl.BlockSpec((1, gw), lambda g, c: (0, g)),
        pl.BlockSpec((SSD_CONV, SSD_STATE), lambda g, c: (0, xb0 + g)),
        pl.BlockSpec((1, SSD_STATE), lambda g, c: (0, xb0 + g)),
        pl.BlockSpec((SSD_CONV, SSD_STATE), lambda g, c: (0, cb0 + g)),
        pl.BlockSpec((1, SSD_STATE), lambda g, c: (0, cb0 + g)),
        pl.BlockSpec((1, LANES), lambda g, c: (0, g)),
        pl.BlockSpec((hp, 1), lambda g, c: (g, 0)),
        pl.BlockSpec((1, LANES), lambda g, c: (0, g)),
        pl.BlockSpec((hp, 1), lambda g, c: (g, 0)),
        pl.BlockSpec((1, gw), lambda g, c: (0, g)),
        pl.BlockSpec((1, gw), lambda g, c: (0, g)),
        pl.BlockSpec((LANES, gw), lambda g, c: (0, 0)),
    ]
    n_chunks = t // L
    s_in, s_out, s_shapes, s_args = _side_cast_plumbing(side_casts, SSD_GROUPS * n_chunks,
                                                        lambda g, c: g * n_chunks + c)
    outs = pl.pallas_call(
        functools.partial(_ssd_body, n_side=len(side_casts)),
        grid=(SSD_GROUPS, n_chunks),
        in_specs=in_specs + s_in,
        out_specs=[pl.BlockSpec((L, gw), lambda g, c: (c, g))] + s_out,
        out_shape=[jax.ShapeDtypeStruct((t, SSD_INNER), BF16)] + s_shapes,
        scratch_shapes=[pltpu.VMEM((SSD_STATE, gw), F32),
                        pltpu.VMEM((8, gw), F32),
                        pltpu.VMEM((8, SSD_STATE), F32),
                        pltpu.VMEM((8, SSD_STATE), F32)],
        compiler_params=_cparams("arbitrary", "arbitrary"),
        name="ssd",
    )(proj, proj, proj, proj, dt_raw, dtt_raw, cw, cbias, cw, cbias, cw, cbias,
      dtb, dtbt, alog, alogt, dskip, norm_g.reshape(1, SSD_INNER), expand, *s_args)
    return tuple(outs) if side_casts else outs[0]


def _gmlp_body(u_ref, v_ref, g_ref, b_ref, ws_ref, bst_ref, o_ref):
    u = _gelu_erf(u_ref[...].astype(F32))
    v = _gelu_erf(v_ref[...].astype(F32))
    mu = jnp.mean(v, axis=-1, keepdims=True)
    vc = v - mu
    var = jnp.mean(vc * vc, axis=-1, keepdims=True)
    vn = (vc * lax.rsqrt(var + NORM_EPS) * g_ref[...] + b_ref[...]).astype(BF16)
    row = lax.broadcasted_iota(I32, (GM_CHUNK, GM_CHUNK), 0)
    col = lax.broadcasted_iota(I32, (GM_CHUNK, GM_CHUNK), 1)
    causal = row >= col
    for g in range(GM_GROUPS):
        sl = slice(g * GM_GROUP_DIM, (g + 1) * GM_GROUP_DIM)
        w = jnp.where(causal, ws_ref[g], jnp.zeros((), BF16))
        sv = jnp.dot(w, vn[:, sl], preferred_element_type=F32) + bst_ref[:, sl]
        o_ref[:, sl] = (u[:, sl] * sv).astype(o_ref.dtype)


def gmlp_call(proj, ln_g, ln_b, ws, bs):
    t = proj.shape[0]
    ucol = 0
    bst =jnp.repeat(bs.T, GM_GROUP_DIM, axis=1)
    return pl.pallas_call(
        _gmlp_body,
        grid=(t // GM_CHUNK,),
        in_specs=[pl.BlockSpec((GM_CHUNK, GM_INNER), lambda i: (i, ucol)),
                  pl.BlockSpec((GM_CHUNK, GM_INNER), lambda i: (i, ucol + 1)),
                  pl.BlockSpec((1, GM_INNER), lambda i: (0, 0)),
                  pl.BlockSpec((1, GM_INNER), lambda i: (0, 0)),
                  pl.BlockSpec((GM_GROUPS, GM_CHUNK, GM_CHUNK), lambda i: (0, 0, 0)),
                  pl.BlockSpec((GM_CHUNK, GM_INNER), lambda i: (0, 0))],
        out_specs=pl.BlockSpec((GM_CHUNK, GM_INNER), lambda i: (i, 0)),
        out_shape=jax.ShapeDtypeStruct((t, GM_INNER), BF16),
        compiler_params=_cparams("parallel"),
        name="gmlp",
    )(proj, proj, ln_g.reshape(1, GM_INNER), ln_b.reshape(1, GM_INNER), ws.astype(BF16), bst)


def _rope_norm(x, gain, cosf, sinf):
    ms = jnp.mean(x * x, axis=-1, keepdims=True)
    xn = x * lax.rsqrt(ms + NORM_EPS) * gain
    return xn * cosf + pltpu.roll(xn, ATT_HEAD_DIM // 2, axis=1) * sinf


def _qprep_body(x_ref, g_ref, cos_ref, sin_ref, o_ref):
    for r in range(ATT_REP):
        x = x_ref[:, r * ATT_HEAD_DIM:(r + 1) * ATT_HEAD_DIM].astype(F32)
        y = _rope_norm(x, g_ref[...], cos_ref[...], sin_ref[...]) * QK_SCALE_LOG2E
        o_ref[r] = y.T.astype(o_ref.dtype)


def _kprep_body(x_ref, v_ref, g_ref, cos_ref, sin_ref, o_ref, vt_ref, mean_ref):
    tb = MOBA_BLOCK
    lane = lax.broadcasted_iota(I32, (tb, LANES), 1)
    for s in range(KPREP_BLOCKS):
        rows = slice(s * tb, (s + 1) * tb)
        y = _rope_norm(x_ref[rows, :].astype(F32), g_ref[...], cos_ref[rows, :], sin_ref[rows, :])
        block = pl.program_id(0) * KPREP_BLOCKS + s
        o_ref[rows, :ATT_HEAD_DIM] = y.astype(o_ref.dtype)
        o_ref[rows, ATT_HEAD_DIM:] = jnp.where(lane == block, 1.0, 0.0).astype(o_ref.dtype)
        mean_ref[s] = jnp.mean(y, axis=0, keepdims=True)
        vt_ref[s] = v_ref[rows, :].astype(F32).T.astype(vt_ref.dtype)


def qk_prep_call(qkv, q_norm, k_norm, cosf, sinf):
    t = qkv.shape[0]
    tb = MOBA_BLOCK
    gw = ATT_REP * ATT_HEAD_DIM
    common = [pl.BlockSpec((1, ATT_HEAD_DIM), lambda i, h: (0, 0)),
              pl.BlockSpec((tb, ATT_HEAD_DIM), lambda i, h: (i, 0)),
              pl.BlockSpec((tb, ATT_HEAD_DIM), lambda i, h: (i, 0))]
    qt = pl.pallas_call(
        _qprep_body,
        grid=(t // tb, ATT_KV_HEADS),
        in_specs=[pl.BlockSpec((tb, gw), lambda i, h: (i, h))] + common,
        out_specs=pl.BlockSpec((ATT_REP, ATT_HEAD_DIM, tb), lambda i, h: (h, 0, i)),
        out_shape=jax.ShapeDtypeStruct((ATT_HEADS, ATT_HEAD_DIM, t), BF16),
        compiler_params=_cparams("parallel", "parallel"),
        name="q_prep",
    )(qkv, q_norm.reshape(1, ATT_HEAD_DIM), cosf, sinf)
    vcol = (ATT_Q_W + ATT_KV_W) // ATT_HEAD_DIM
    kb = KPREP_BLOCKS
    k, vt, kmean = pl.pallas_call(
        _kprep_body,
        grid=(t // (kb * tb), ATT_KV_HEADS),
        in_specs=[pl.BlockSpec((kb * tb, ATT_HEAD_DIM), lambda i, h: (i, ATT_HEADS + h)),
                  pl.BlockSpec((kb * tb, ATT_HEAD_DIM), lambda i, h: (i, vcol + h)),
                  pl.BlockSpec((1, ATT_HEAD_DIM), lambda i, h: (0, 0)),
                  pl.BlockSpec((kb * tb, ATT_HEAD_DIM), lambda i, h: (i, 0)),
                  pl.BlockSpec((kb * tb, ATT_HEAD_DIM), lambda i, h: (i, 0))],
        out_specs=[pl.BlockSpec((kb * tb, ATT_KAUG_W), lambda i, h: (i, h)),
                   pl.BlockSpec((None, kb, ATT_VT_ROWS, tb), lambda i, h: (h, i, 0, 0)),
                   pl.BlockSpec((kb, None, 1, ATT_HEAD_DIM), lambda i, h: (i, h, 0, 0))],
        out_shape=[jax.ShapeDtypeStruct((t, ATT_KV_HEADS * ATT_KAUG_W), BF16),
                   jax.ShapeDtypeStruct((ATT_KV_HEADS, t // tb, ATT_VT_ROWS, tb), BF16),
                   jax.ShapeDtypeStruct((t // tb, ATT_KV_HEADS, 1, ATT_HEAD_DIM), F32)],
        compiler_params=_cparams("parallel", "parallel"),
        name="k_prep",
    )(qkv, qkv, k_norm.reshape(1, ATT_HEAD_DIM), cosf, sinf)
    return qt, k, vt, kmean


def _attn_body(qt_ref, k_ref, vt_ref, km_ref, *rest, nb, n_side):
    o_ref = rest[n_side]
    acc_ref, qa_ref, s_ref = rest[2 * n_side + 1:]
    _side_cast(rest[:n_side], rest[n_side + 1:2 * n_side + 1])
    qblk = pl.program_id(1)
    blk = MOBA_BLOCK
    hd = ATT_HEAD_DIM
    n_blocks = k_ref.shape[0] // blk
    nbp = (n_blocks + 7) // 8 * 8
    own0 = pl.multiple_of(qblk * blk, blk)
    km = km_ref[...]
    km_hi = km.astype(BF16)
    km_lo = (km - km_hi.astype(F32)).astype(BF16)
    bid = lax.broadcasted_iota(I32, (nbp, blk), 0)
    causal = lax.broadcasted_iota(I32, (blk, blk), 0) <= lax.broadcasted_iota(I32, (blk, blk), 1)

    gates, own_scores = [], []
    for r in range(ATT_REP):
        qt = qt_ref[r]
        gates.append((jnp.dot(km_hi, qt, preferred_element_type=F32)
                      + jnp.dot(km_lo, qt, preferred_element_type=F32))[:nbp])
    for r in range(ATT_REP):
        own_scores.append(jnp.dot(k_ref[pl.ds(own0, blk), :hd], qt_ref[r],
                                  preferred_element_type=F32))
    for r in range(ATT_REP):
        gate = jnp.where(bid < qblk, gates[r], -jnp.inf)
        picked = bid < 0
        for kk in range(MOBA_TOPK):
            mx = jnp.max(gate, axis=0, keepdims=True)
            idx = jnp.min(jnp.where(gate == mx, bid, LANES), axis=0, keepdims=True)
            hit = bid == idx
            picked = picked | (hit & (kk < qblk))
            gate = jnp.where(hit, -jnp.inf, gate)
        bias = jnp.where(picked, 0.0, MASK_BIAS).astype(BF16)
        qa_ref[r, :hd, :] = qt_ref[r]
        qa_ref[r, hd:hd + nbp, :] = bias
        if nbp < LANES:
            qa_ref[r, hd + nbp:, :] = jnp.zeros((LANES - nbp, blk), BF16)

    def blocks_of(jj):
        return [jnp.minimum(jj * nb + u, n_blocks - 1) for u in range(nb)]

    def head_scores(r, blocks):
        return [jnp.dot(k_ref[pl.ds(pl.multiple_of(ja * blk, blk), blk), :], qa_ref[r],
                        preferred_element_type=F32) for ja in blocks]

    ahead = ATT_REP - 1
    for r in range(ahead):
        for u, su in enumerate(head_scores(r, blocks_of(0))):
            s_ref[r * nb + u] = su

    m0, l0 = [], []
    for r in range(ATT_REP):
        s = jnp.where(causal, own_scores[r], -jnp.inf)
        m = jnp.max(s, axis=0, keepdims=True)
        p = jnp.exp2(s - m)
        m0.append(m)
        l0.append(jnp.sum(p, axis=0, keepdims=True))
        acc_ref[r] = jnp.dot(vt_ref[qblk], p.astype(BF16), preferred_element_type=F32)

    def past_blocks(jj, carry):
        ms, ls = carry
        blocks = blocks_of(jj)
        blocks_next = blocks_of(jj + 1)
        vts = [vt_ref[ja] for ja in blocks]
        m_out, l_out = [], []
        in_flight = {}
        for r in range(ATT_REP):
            scores = in_flight.pop(r) if r in in_flight else [s_ref[r * nb + u] for u in range(nb)]
            r_req = r + ahead
            if r_req < ATT_REP:
                in_flight[r_req] = head_scores(r_req, blocks)
            else:
                for u, su in enumerate(head_scores(r_req - ATT_REP, blocks_next)):
                    s_ref[(r_req - ATT_REP) * nb + u] = su
            m_new = ms[r]
            for su in scores:
                m_new = jnp.maximum(m_new, jnp.max(su, axis=0, keepdims=True))
            alpha = jnp.exp2(ms[r] - m_new)
            l_new = alpha * ls[r]
            pv = None
            for u in range(nb):
                pu = jnp.exp2(scores[u] - m_new)
                l_new = l_new + jnp.sum(pu, axis=0, keepdims=True)
                d = jnp.dot(vts[u], pu.astype(BF16), preferred_element_type=F32)
                pv = d if pv is None else pv + d
            acc_ref[r] = alpha * acc_ref[r] + pv
            m_out.append(m_new)
            l_out.append(l_new)
        return tuple(m_out), tuple(l_out)

    _, l_fin = lax.fori_loop(0, (qblk + nb - 1) // nb, past_blocks, (tuple(m0), tuple(l0)))

    for r in range(ATT_REP):
        out = acc_ref[r] / l_fin[r]
        o_ref[:, r * hd:(r + 1) * hd] = out.T.astype(o_ref.dtype)


def attention_call(qt, k, vt, kmean, side_casts=()):
    t = k.shape[0]
    tq = MOBA_BLOCK
    n_blocks = t // MOBA_BLOCK
    n_tiles = t // tq
    km = jnp.transpose(kmean.reshape(n_blocks, ATT_KV_HEADS, ATT_HEAD_DIM), (1, 0, 2))
    km = jnp.pad(km, ((0, 0), (0, LANES - n_blocks), (0, 0)))
    s_in, s_out, s_shapes, s_args = _side_cast_plumbing(side_casts, ATT_KV_HEADS * n_tiles,
                                                        lambda g, i: g * n_tiles + i)
    outs = pl.pallas_call(
        functools.partial(_attn_body, nb=ATT_NB, n_side=len(side_casts)),
        grid=(ATT_KV_HEADS, n_tiles),
        in_specs=[pl.BlockSpec((ATT_REP, ATT_HEAD_DIM, tq), lambda g, i: (g, 0, i)),
                  pl.BlockSpec((t, ATT_KAUG_W), lambda g, i: (0, g)),
                  pl.BlockSpec((None, n_blocks, ATT_VT_ROWS, MOBA_BLOCK), lambda g, i: (g, 0, 0, 0)),
                  pl.BlockSpec((None, LANES, ATT_HEAD_DIM), lambda g, i: (g, 0, 0))] + s_in,
        out_specs=[pl.BlockSpec((tq, ATT_REP * ATT_HEAD_DIM), lambda g, i: (i, g))] + s_out,
        out_shape=[jax.ShapeDtypeStruct((t, ATT_Q_W), BF16)] + s_shapes,
        scratch_shapes=[pltpu.VMEM((ATT_REP, ATT_VT_ROWS, tq), F32),
                        pltpu.VMEM((ATT_REP, ATT_KAUG_W, tq), BF16),
                        pltpu.VMEM(((ATT_REP - 1) * ATT_NB, MOBA_BLOCK, tq), F32)],
        compiler_params=_cparams("arbitrary", "arbitrary"),
        name="moba_attention",
    )(qt, k, vt, km, *s_args)
    return tuple(outs) if side_casts else outs[0]


def _router_body(h_ref, g_ref, wr_ref, wrl_ref, br_ref, xn_ref, re_ref, rw_ref, cnt_ref):
    h = h_ref[...]
    ms = jnp.mean(h * h, axis=-1, keepdims=True)
    xn = h * lax.rsqrt(ms + NORM_EPS) * g_ref[...]
    xn_ref[...] = _pack_bf16_pairs(xn)
    xh = xn.astype(BF16)
    xl = (xn - xh.astype(F32)).astype(BF16)
    logits = (jnp.dot(xh, wr_ref[...], preferred_element_type=F32)
              + jnp.dot(xh, wrl_ref[...], preferred_element_type=F32)
              + jnp.dot(xl, wr_ref[...], preferred_element_type=F32)) + br_ref[...]
    lane = lax.broadcasted_iota(I32, logits.shape, 1)
    neg = -jnp.inf

    is_g = lane < MOE_GROUPS
    gl = jnp.where(is_g, logits, neg)
    ge = jnp.exp(gl - jnp.max(gl, axis=1, keepdims=True))
    gp = ge / jnp.sum(ge, axis=1, keepdims=True)
    g_w = jnp.max(gp, axis=1, keepdims=True)
    g_idx = jnp.min(jnp.where(is_g & (gp == g_w), lane, LANES), axis=1, keepdims=True)

    e_lo = MOE_GROUPS + MOE_EPG * g_idx
    is_e = (lane >= e_lo) & (lane < e_lo + MOE_EPG)
    el = jnp.where(is_e, logits, neg)
    ee = jnp.exp(el - jnp.max(el, axis=1, keepdims=True))
    ep = jnp.where(is_e, ee / jnp.sum(ee, axis=1, keepdims=True), -1.0)
    p1 = jnp.max(ep, axis=1, keepdims=True)
    i1 = jnp.min(jnp.where(ep == p1, lane, LANES), axis=1, keepdims=True)
    ep2 = jnp.where(lane == i1, -1.0, ep)
    p2 = jnp.max(ep2, axis=1, keepdims=True)
    i2 = jnp.min(jnp.where(ep2 == p2, lane, LANES), axis=1, keepdims=True)
    den = p1 + p2
    w1 = g_w * (p1 / den)
    w2 = g_w * (p2 / den)
    rw_ref[...] = jnp.where(lane == 0, w1, jnp.where(lane == 1, w2, 0.0))

    @pl.when(pl.program_id(0) == 0)
    def _():
        cnt_ref[...] = jnp.zeros_like(cnt_ref)

    e1, e2 = i1 - MOE_GROUPS, i2 - MOE_GROUPS
    oh1 = (lane == e1).astype(F32)
    oh2 = (lane == e2).astype(F32)
    tm = h.shape[0]
    earlier = (lax.broadcasted_iota(I32, (tm, tm), 1) < lax.broadcasted_iota(I32, (tm, tm), 0)).astype(BF16)
    before = jnp.dot(earlier, (oh1 + oh2).astype(BF16), preferred_element_type=F32) + cnt_ref[...]
    r1 = jnp.sum(oh1 * before, axis=1, keepdims=True).astype(I32)
    r2 = jnp.sum(oh2 * (before + oh1), axis=1, keepdims=True).astype(I32)
    cnt_ref[...] = cnt_ref[...] + jnp.sum(oh1 + oh2, axis=0, keepdims=True)
    re_ref[...] = jnp.where(lane == 0, e1, jnp.where(lane == 1, e2, jnp.where(lane == 2, r1,
                                                                               jnp.where(lane == 3, r2, 0))))


def router_call(h, gain, w_group, b_group, w_expert, b_expert):
    t, d = h.shape
    nr = MOE_GROUPS + MOE_EXPERTS
    wr = jnp.zeros((d, LANES), F32).at[:, :MOE_GROUPS].set(w_group).at[:, MOE_GROUPS:nr].set(w_expert)
    br = jnp.zeros((1, LANES), F32).at[0, :MOE_GROUPS].set(b_group).at[0, MOE_GROUPS:nr].set(b_expert)
    wr_hi = wr.astype(BF16)
    wr_lo = (wr - wr_hi.astype(F32)).astype(BF16)
    return pl.pallas_call(
        _router_body,
        grid=(t // NORM_TM,),
        in_specs=[pl.BlockSpec((NORM_TM, d), lambda i: (i, 0)),
                  pl.BlockSpec((1, d), lambda i: (0, 0)),
                  pl.BlockSpec((d, LANES), lambda i: (0, 0)),
                  pl.BlockSpec((d, LANES), lambda i: (0, 0)),
                  pl.BlockSpec((1, LANES), lambda i: (0, 0))],
        out_specs=[pl.BlockSpec((NORM_TM, d // 2), lambda i: (i, 0)),
                   pl.BlockSpec((NORM_TM, LANES), lambda i: (i, 0)),
                   pl.BlockSpec((NORM_TM, LANES), lambda i: (i, 0)),
                   pl.BlockSpec((1, LANES), lambda i: (0, 0))],
        out_shape=[jax.ShapeDtypeStruct((t, d // 2), jnp.uint32),
                   jax.ShapeDtypeStruct((t, LANES), I32),
                   jax.ShapeDtypeStruct((t, LANES), F32),
                   jax.ShapeDtypeStruct((1, LANES), F32)],
        compiler_params=_cparams("arbitrary"),
        name="norm_router",
    )(h, gain.reshape(1, d), wr_hi, wr_lo, br)


def _row_copy(src_ref, dst_ref, sem, src_row, dst_row):
    return pltpu.make_async_copy(src_ref.at[pl.ds(src_row, 1), :], dst_ref.at[pl.ds(dst_row, 1), :], sem)


def _start_row_gather(idx_ref, src_ref, dst_ref, sem, n):
    for r in range(n):
        _row_copy(src_ref, dst_ref, sem, idx_ref[0, r], r).start(priority=r % 2)


def _wait_row_gather(src_ref, dst_ref, sem, n):
    for r in range(n):
        _row_copy(src_ref, dst_ref, sem, 0, r).wait()


def _ffn_body(be_ref, nv_ref, tok0_ref, tokn_ref, x_hbm, wg_ref, wu_ref, wd_ref, o_ref, xbuf, sems):
    b = pl.program_id(0)
    nv = nv_ref[0]
    last = pl.num_programs(0) - 1
    tm = o_ref.shape[0]
    slot = b % 2
    valid = b < nv

    @pl.when(b == 0)
    def _():
        _start_row_gather(tok0_ref, x_hbm, xbuf.at[0], sems.at[0], tm)

    @pl.when(b <= nv)
    def _():
        _wait_row_gather(x_hbm, xbuf.at[slot], sems.at[slot], tm)

    @pl.when(valid)
    def _():
        _start_row_gather(tokn_ref, x_hbm, xbuf.at[1 - slot], sems.at[1 - slot], tm)
        x = _unpack_bf16_pairs(xbuf[slot])
        gate = jnp.dot(x, wg_ref[...], preferred_element_type=F32)
        up = jnp.dot(x, wu_ref[...], preferred_element_type=F32)
        hid = (_silu(gate) * up).astype(BF16)
        o_ref[...] = _pack_bf16_pairs(jnp.dot(hid, wd_ref[...], preferred_element_type=F32))

    @pl.when(jnp.logical_not(valid))
    def _():
        o_ref[...] = jnp.zeros_like(o_ref)

    @pl.when(valid & (b == last))
    def _():
        _wait_row_gather(x_hbm, xbuf.at[1 - slot], sems.at[1 - slot], tm)


def ffn_call(xn, row_token, blk_expert, n_valid, w_gate, w_up, w_down, layer):
    d, ff = w_gate.shape[2], w_gate.shape[3]
    tm = MOE_TM
    n_rows = row_token.shape[0]
    n_blk = n_rows // tm

    def w_map(b, be, nv):
        return (layer, be[jnp.minimum(b, nv[0] - 1)], 0, 0)

    tok_spec = functools.partial(pl.BlockSpec, (None, 1, tm), memory_space=pltpu.SMEM)
    grid_spec = pltpu.PrefetchScalarGridSpec(
        num_scalar_prefetch=2,
        grid=(n_blk,),
        in_specs=[tok_spec(index_map=lambda b, be, nv: (0, 0, 0)),
                  tok_spec(index_map=lambda b, be, nv: (jnp.minimum(b + 1, n_blk - 1), 0, 0)),
                  pl.BlockSpec(memory_space=pl.ANY),
                  pl.BlockSpec((None, None, d, ff), w_map),
                  pl.BlockSpec((None, None, d, ff), w_map),
                  pl.BlockSpec((None, None, ff, d), w_map)],
        out_specs=pl.BlockSpec((tm, d // 2), lambda b, be, nv: (b, 0)),
        scratch_shapes=[pltpu.VMEM((2, tm, xn.shape[1]), xn.dtype), pltpu.SemaphoreType.DMA((2,))],
    )
    tok = row_token.reshape(n_blk, 1, tm)
    return pl.pallas_call(
        _ffn_body,
        grid_spec=grid_spec,
        out_shape=jax.ShapeDtypeStruct((n_rows, d // 2), jnp.uint32),
        compiler_params=_cparams("arbitrary"),
        name="moe_ffn",
    )(blk_expert, n_valid, tok, tok, xn, w_gate, w_up, w_down)


def _combine_body(pos_ref, posn_ref, y_ref, h_ref, rw_ref, g_ref, *out_refs_and_scratch, with_norm):
    if with_norm:
        ho_ref, xn_ref, buf_ref, sems = out_refs_and_scratch
    else:
        ho_ref, buf_ref, sems = out_refs_and_scratch
    tb = ho_ref.shape[0]
    n = MOE_TOPK * tb
    i = pl.program_id(0)
    slot = i % 2

    @pl.when(i == 0)
    def _():
        _start_row_gather(pos_ref, y_ref, buf_ref.at[0], sems.at[0], n)

    _wait_row_gather(y_ref, buf_ref.at[slot], sems.at[slot], n)

    @pl.when(i + 1 < pl.num_programs(0))
    def _():
        _start_row_gather(posn_ref, y_ref, buf_ref.at[1 - slot], sems.at[1 - slot], n)

    rw = rw_ref[...]
    y = _unpack_bf16_pairs(buf_ref[slot], F32)
    ffn = rw[:, 0:1] * y[0:tb, :] + rw[:, 1:2] * y[tb:2 * tb, :]
    hn = h_ref[...] + ffn
    ho_ref[...] = hn
    if with_norm:
        ms = jnp.mean(hn * hn, axis=-1, keepdims=True)
        xn_ref[...] = (hn * lax.rsqrt(ms + NORM_EPS) * g_ref[...]).astype(xn_ref.dtype)


def combine_call(y, pos, h, route_w, next_gain):
    t, d = h.shape
    tb = NORM_TM // 2
    n_steps = t // tb
    with_norm = next_gain is not None
    gain = (next_gain if with_norm else jnp.ones((d,), F32)).reshape(1, d)
    out_specs = [pl.BlockSpec((tb, d), lambda i: (i, 0))]
    out_shape = [jax.ShapeDtypeStruct((t, d), F32)]
    if with_norm:
        out_specs.append(pl.BlockSpec((tb, d), lambda i: (i, 0)))
        out_shape.append(jax.ShapeDtypeStruct((t, d), BF16))
    pos_spec = functools.partial(pl.BlockSpec, (None, 1, MOE_TOPK * tb), memory_space=pltpu.SMEM)
    outs = pl.pallas_call(
        functools.partial(_combine_body, with_norm=with_norm),
        grid=(n_steps,),
        in_specs=[pos_spec(index_map=lambda i: (i, 0, 0)),
                  pos_spec(index_map=lambda i: (jnp.minimum(i + 1, n_steps - 1), 0, 0)),
                  pl.BlockSpec(memory_space=pl.ANY),
                  pl.BlockSpec((tb, d), lambda i: (i, 0)),
                  pl.BlockSpec((tb, LANES), lambda i: (i, 0)),
                  pl.BlockSpec((1, d), lambda i: (0, 0))],
        out_specs=out_specs,
        out_shape=out_shape,
        scratch_shapes=[pltpu.VMEM((2, MOE_TOPK * tb, d // 2), jnp.uint32), pltpu.SemaphoreType.DMA((2,))],
        compiler_params=_cparams("arbitrary"),
        name="moe_combine",
    )(pos, pos, y, h, route_w, gain)
    return (outs[0], outs[1]) if with_norm else (outs[0], None)


def _dispatch_tables(route_e, counts, t):
    tm = MOE_TM
    n_assign = MOE_TOPK * t
    expert = route_e[:, :MOE_TOPK].reshape(n_assign)
    rank = route_e[:, MOE_TOPK:2 * MOE_TOPK].reshape(n_assign)
    counts = counts[0, :MOE_EXPERTS].astype(I32)
    padded = (counts + tm - 1) // tm * tm
    pend = jnp.cumsum(padded)
    pstart = pend - padded
    onehot = expert[:, None] == jnp.arange(MOE_EXPERTS, dtype=I32)[None, :]
    dest = jnp.sum(jnp.where(onehot, pstart[None, :], 0), axis=1) + rank
    n_rows = n_assign + MOE_EXPERTS * tm
    token = jnp.arange(n_assign, dtype=I32) // MOE_TOPK
    row_token = jnp.zeros((n_rows,), I32).at[dest].set(token)
    n_blk = n_rows // tm
    blk_start = jnp.arange(n_blk, dtype=I32) * tm
    blk_expert = jnp.minimum(jnp.sum((pend[None, :] <= blk_start[:, None]).astype(I32), axis=1),
                             MOE_EXPERTS - 1)
    n_valid = (pend[-1] // tm).astype(I32).reshape(1)
    tb = NORM_TM // 2
    pos = dest.reshape(t // tb, tb, MOE_TOPK).transpose(0, 2, 1).reshape(t // tb, 1, MOE_TOPK * tb)
    return row_token, blk_expert, n_valid, pos.astype(I32)


def moe_layer(h, gain, w_group, b_group, w_expert, b_expert, expert_w16, next_gain):
    t, d = h.shape
    xn, route_e, route_w, counts = router_call(h, gain, w_group, b_group, w_expert, b_expert)
    row_token, blk_expert, n_valid, pos = _dispatch_tables(route_e, counts, t)
    g16, u16, d16 = expert_w16
    y = ffn_call(xn, row_token, blk_expert, n_valid, g16.reshape(1, MOE_EXPERTS, d, -1),
                 u16.reshape(1, MOE_EXPERTS, d, -1), d16.reshape(1, MOE_EXPERTS, -1, d), 0)
    return combine_call(y, pos, h, route_w, next_gain)


def hybrid_layer(h, xn, w_in16_all, j, w_out16, conv_w, conv_b, dt_bias, a_log, d_skip, ssd_norm, ln_g, ln_b,
                 ws, bs, expert_w, layer, next_mixer_w):
    w_in16 = w_in16_all[j]
    wg, wu, wd = expert_w
    hp = SSD_HEADS_PER_GROUP
    w_dt = jnp.zeros((D_MODEL, SSD_GROUPS, LANES), BF16).at[:, :, :hp].set(
        w_in16[:, OFF_DT:OFF_U].reshape(D_MODEL, SSD_GROUPS, hp)).reshape(D_MODEL, SSD_GROUPS * LANES)
    proj_a, g16 = matmul_call([xn], w_in16_all, out_dtype=BF16, n_cols=OFF_DT, layer=j,
                              side_casts=[(wg, layer)], name="hyb_in_proj_ssd")
    proj_b, u16 = matmul_call([xn], w_in16[:, OFF_U:], out_dtype=BF16, side_casts=[(wu, layer)],
                              name="hyb_in_proj_gmlp")
    dt_raw = matmul_call([xn], w_dt, out_dtype=F32, tn=SSD_GROUPS * LANES, name="hyb_dt_proj")
    dtt_raw = dt_raw.reshape(-1, SSD_GROUPS, LANES)[:, :, :hp].reshape(-1, SSD_HEADS).T
    next16 = None
    if next_mixer_w is None:
        y_a = ssd_call(proj_a, dt_raw, dtt_raw, conv_w, conv_b, dt_bias, a_log, d_skip, ssd_norm)
    else:
        stacks, idx = next_mixer_w
        y_a, *next16 = ssd_call(proj_a, dt_raw, dtt_raw, conv_w, conv_b, dt_bias, a_log, d_skip, ssd_norm,
                                side_casts=[(s, idx) for s in stacks])
    y_b = gmlp_call(proj_b, ln_g, ln_b, ws, bs)
    h, d16 = matmul_call([y_a, y_b], w_out16, res=h, side_casts=[(wd, layer)], name="hyb_out_proj")
    return h, (g16, u16, d16), next16


def moba_layer(h, xn, w_qkv16, w_out16, q_norm, k_norm, cosf, sinf, expert_w, layer, next_mixer_w):
    wg, wu, wd = expert_w
    qkv, g16, u16 = matmul_call([xn], w_qkv16, out_dtype=BF16, side_casts=[(wg, layer), (wu, layer)],
                                name="att_qkv_proj")
    qt, k, vt, kmean = qk_prep_call(qkv, q_norm, k_norm, cosf, sinf)
    next16 = None
    if next_mixer_w is None:
        o = attention_call(qt, k, vt, kmean)
    else:
        stacks, idx = next_mixer_w
        o, *next16 = attention_call(qt, k, vt, kmean, side_casts=[(s, idx) for s in stacks])
    h, d16 = matmul_call([o], w_out16, res=h, side_casts=[(wd, layer)], name="att_out_proj")
    return h, (g16, u16, d16), next16


def _rope_tables(t):
    inv = 1.0 / (ROPE_THETA ** (jnp.arange(0, ATT_HEAD_DIM, 2, dtype=F32) / ATT_HEAD_DIM))
    ang = jnp.arange(t, dtype=F32)[:, None] * inv[None, :]
    cos, sin = jnp.cos(ang), jnp.sin(ang)
    return jnp.concatenate([cos, cos], axis=1), jnp.concatenate([-sin, sin], axis=1)


def kernel(x, norm_mix, norm_ffn, hyb_w_in, ssd_conv_w, ssd_conv_b, ssd_dt_bias, ssd_a_log, ssd_d, ssd_norm, gm_ln_g, gm_ln_b, gm_ws, gm_bs, hyb_w_out, att_w_qkv, att_q_norm, att_k_norm, att_w_out, moe_w_group, moe_b_group, moe_w_expert, moe_b_expert, moe_w_gate, moe_w_up, moe_w_down):
    bsz, t, d = x.shape
    depth = norm_mix.shape[0]
    cosf, sinf = _rope_tables(t)
    ff = moe_w_gate.shape[-1]
    expert_w = (moe_w_gate.reshape(depth, MOE_EXPERTS * d, ff), moe_w_up.reshape(depth, MOE_EXPERTS * d, ff),
                moe_w_down.reshape(depth, MOE_EXPERTS * ff, d))
    outs = []
    for b in range(bsz):
        h = x[b]
        xn = rmsnorm_call(h, norm_mix[0])
        hyb_w_in16 = hyb_w_in.astype(BF16)
        mixer16 = [hyb_w_out[0].astype(BF16)]
        for layer in range(depth):
            j = layer // 2
            if layer % 2 == 0:
                next_w = ((att_w_qkv, att_w_out), j) if layer + 1 < depth else None
                h, expert_w16, mixer16 = hybrid_layer(
                    h, xn, hyb_w_in16, j, mixer16[0], ssd_conv_w[j], ssd_conv_b[j], ssd_dt_bias[j], ssd_a_log[j],
                    ssd_d[j], ssd_norm[j], gm_ln_g[j], gm_ln_b[j], gm_ws[j], gm_bs[j], expert_w, layer, next_w)
            else:
                next_w = ((hyb_w_out,), j + 1) if layer + 1 < depth else None
                h, expert_w16, mixer16 = moba_layer(h, xn, mixer16[0], mixer16[1], att_q_norm[j], att_k_norm[j],
                                                    cosf, sinf, expert_w, layer, next_w)
            next_gain = norm_mix[layer + 1] if layer + 1 < depth else None
            h, xn = moe_layer(h, norm_ffn[layer], moe_w_group[layer], moe_b_group[layer],
                              moe_w_expert[layer], moe_b_expert[layer], expert_w16, next_gain)
        outs.append(h)
    return jnp.stack(outs, axis=0)
```
